```python
import jax
import jax.numpy as jnp
from jax import lax
import numpy as np

D_MODEL = 2048
BATCH = 8
SEQ = 2048
DEPTH = 2

GRID_W = 64
CTX_LEN = 256
EPS = 1e-6
ROPE_THETA = 10000.0
Q_BLOCK = 128
HEAD_DIM = 128
N_Q_HEADS = 8
N_KV_HEADS = 2
GQA_GROUP = N_Q_HEADS // N_KV_HEADS
ATTN_WIDTH = N_Q_HEADS * HEAD_DIM
KV_WIDTH = N_KV_HEADS * HEAD_DIM
CONV_WIDTH = D_MODEL - ATTN_WIDTH
CONV_K = 31
IN0_SPLITS = [ATTN_WIDTH, ATTN_WIDTH + KV_WIDTH, ATTN_WIDTH + 2 * KV_WIDTH,
              ATTN_WIDTH + 2 * KV_WIDTH + CONV_WIDTH]
IN0_WIDTH = ATTN_WIDTH + 2 * KV_WIDTH + 2 * CONV_WIDTH
GQA_SCALE = HEAD_DIM ** -0.5
MLA_HEADS = 16
Q_LORA = 1536
KV_LORA = 512
NOPE_DIM = 128
ROPE_DIM = 64
V_DIM = 128
QK_DIM = NOPE_DIM + ROPE_DIM
MLA_SCALE = QK_DIM ** -0.5
N_EXPERTS = 16
D_EXPERT = 1024
EC_FACTOR = 2

kernel_name = 'hybrid_conv_gqa_mla_ec_moe_dit'


def rmsnorm(x, g):
    xf = x.astype(jnp.float32)
    y = xf * lax.rsqrt(jnp.mean(xf * xf, axis=-1, keepdims=True) + EPS)
    return (y * g.astype(jnp.float32)).astype(x.dtype)


def layernorm(x, g, b):
    xf = x.astype(jnp.float32)
    mu = jnp.mean(xf, axis=-1, keepdims=True)
    xc = xf - mu
    y = xc * lax.rsqrt(jnp.mean(xc * xc, axis=-1, keepdims=True) + EPS)
    return (y * g.astype(jnp.float32) + b.astype(jnp.float32)).astype(x.dtype)


def modulate(h, shift, scale):
    return h * (1 + scale) + shift


def ada_params(cond, w, b):
    return jnp.split(jax.nn.silu(cond) @ w + b, 6, axis=-1)


def axial_rope_tables(n_tokens, d_rot):
    rows = n_tokens // GRID_W
    row = jnp.repeat(jnp.arange(rows, dtype=jnp.float32), GRID_W)
    col = jnp.tile(jnp.arange(GRID_W, dtype=jnp.float32), rows)
    n_axis = d_rot // 4
    inv_freq = ROPE_THETA ** (-jnp.arange(n_axis, dtype=jnp.float32) / n_axis)
    ang = jnp.concatenate([row[:, None] * inv_freq, col[:, None] * inv_freq], axis=-1)
    return jnp.cos(ang), jnp.sin(ang)


def apply_rope(x, cos, sin):
    half = x.shape[-1] // 2
    x1, x2 = x[..., :half], x[..., half:]
    c = cos[None, :, None, :].astype(x.dtype)
    s = sin[None, :, None, :].astype(x.dtype)
    return jnp.concatenate([x1 * c - x2 * s, x1 * s + x2 * c], axis=-1)


def sdpa(q, k, v, scale):
    s = jnp.einsum('bqhgd,bkhd->bhgqk', q, k).astype(jnp.float32) * scale
    p = jax.nn.softmax(s, axis=-1).astype(v.dtype)
    return jnp.einsum('bhgqk,bkhd->bqhgd', p, v)


def attend_latent(q_lat, k_lat, v_lat, k_ctx, v_ctx, scale):
    b, s = q_lat.shape[:2]
    keys = jnp.concatenate([k_ctx, k_lat], axis=1)
    vals = jnp.concatenate([v_ctx, v_lat], axis=1)
    nb = s // Q_BLOCK
    qb = jnp.moveaxis(q_lat.reshape((b, nb, Q_BLOCK) + q_lat.shape[2:]), 1, 0)
    ob = lax.map(lambda qq: sdpa(qq, keys, vals, scale), qb)
    return jnp.moveaxis(ob, 0, 1).reshape(b, s, -1)


def depthwise_conv(x, w, b):
    y = lax.conv_general_dilated(x, w.astype(x.dtype), window_strides=(1,),
                                 padding=[(CONV_K // 2, CONV_K // 2)],
                                 dimension_numbers=('NWC', 'WIO', 'NWC'),
                                 feature_group_count=x.shape[-1])
    return y + b


def conformer_branch(u, dw_w, dw_b, ln_g, ln_b):
    return jax.nn.silu(layernorm(depthwise_conv(u, dw_w, dw_b), ln_g, ln_b))


def mixer_ab(h_lat, h_ctx, need_ctx, w_in, q_norm_g, k_norm_g, dw_w, dw_b, ln_g, ln_b, w_out):
    cos, sin = axial_rope_tables(h_lat.shape[1], HEAD_DIM)

    def project(h, rope):
        b, l, _ = h.shape
        q, k, v, u, gt = jnp.split(h @ w_in, IN0_SPLITS, axis=-1)
        q = rmsnorm(q.reshape(b, l, N_Q_HEADS, HEAD_DIM), q_norm_g)
        k = rmsnorm(k.reshape(b, l, N_KV_HEADS, HEAD_DIM), k_norm_g)
        v = v.reshape(b, l, N_KV_HEADS, HEAD_DIM)
        if rope:
            q = apply_rope(q, cos, sin)
            k = apply_rope(k, cos, sin)
        q = q.reshape(b, l, N_KV_HEADS, GQA_GROUP, HEAD_DIM)
        return q, k, v, u * jax.nn.sigmoid(gt)

    q_l, k_l, v_l, u_l = project(h_lat, True)
    q_c, k_c, v_c, u_c = project(h_ctx, False)
    a_lat = attend_latent(q_l, k_l, v_l, k_c, v_c, GQA_SCALE)
    c_lat = conformer_branch(u_l, dw_w, dw_b, ln_g, ln_b)
    out_lat = jnp.concatenate([a_lat, c_lat], axis=-1) @ w_out
    out_ctx = None
    if need_ctx:
        b, l = h_ctx.shape[:2]
        a_ctx = sdpa(q_c, k_c, v_c, GQA_SCALE).reshape(b, l, -1)
        c_ctx_br = conformer_branch(u_c, dw_w, dw_b, ln_g, ln_b)
        out_ctx = jnp.concatenate([a_ctx, c_ctx_br], axis=-1) @ w_out
    return out_lat, out_ctx


def mixer_mla(h_lat, h_ctx, need_ctx, w_dqkv, q_lora_g, w_uq, kv_lora_g, w_ukv, w_out):
    cos, sin = axial_rope_tables(h_lat.shape[1], ROPE_DIM)

    def queries(h, rope):
        b, l, _ = h.shape
        cq = rmsnorm(h @ w_dqkv[:, :Q_LORA], q_lora_g)
        q = (cq @ w_uq).reshape(b, l, MLA_HEADS, QK_DIM)
        q_nope, q_pe = q[..., :NOPE_DIM], q[..., NOPE_DIM:]
        if rope:
            q_pe = apply_rope(q_pe, cos, sin)
        return jnp.concatenate([q_nope, q_pe], axis=-1)[:, :, :, None, :]

    def keys_values(h, rope):
        b, l, _ = h.shape
        z = h @ w_dqkv[:, Q_LORA:]
        ckv, k_pe = z[..., :KV_LORA], z[..., KV_LORA:]
        kv = (rmsnorm(ckv, kv_lora_g) @ w_ukv).reshape(b, l, MLA_HEADS, NOPE_DIM + V_DIM)
        k_nope, v = kv[..., :NOPE_DIM], kv[..., NOPE_DIM:]
        k_pe = k_pe[:, :, None, :]
        if rope:
            k_pe = apply_rope(k_pe, cos, sin)
        k = jnp.concatenate([k_nope, jnp.broadcast_to(k_pe, (b, l, MLA_HEADS, ROPE_DIM))], axis=-1)
        return k, v

    k_l, v_l = keys_values(h_lat, True)
    k_c, v_c = keys_values(h_ctx, False)
    out_lat = attend_latent(queries(h_lat, True), k_l, v_l, k_c, v_c, MLA_SCALE) @ w_out
    out_ctx = None
    if need_ctx:
        b, l = h_ctx.shape[:2]
        out_ctx = sdpa(queries(h_ctx, False), k_c, v_c, MLA_SCALE).reshape(b, l, -1) @ w_out
    return out_lat, out_ctx


def ec_moe(h, router, w_gate, w_up, w_down):
    n, d = h.shape[1], h.shape[2]
    cap = EC_FACTOR * n // N_EXPERTS
    aff = jax.nn.softmax((h @ router).astype(jnp.float32), axis=-1)
    gate, idx = lax.top_k(jnp.swapaxes(aff, 1, 2), cap)
    xs = jax.vmap(lambda hb, ib: hb[ib])(h, idx)
    hid = jax.nn.silu(jnp.einsum('becd,edf->becf', xs, w_gate)) * jnp.einsum('becd,edf->becf', xs, w_up)
    out = jnp.einsum('becf,efd->becd', hid, w_down) * gate[..., None].astype(h.dtype)
    return jax.vmap(lambda ob, ib: jnp.zeros((n, d), h.dtype).at[ib.reshape(-1)].add(ob.reshape(-1, d)))(out, idx)


def setup_inputs(seed: int = 0) -> dict:
    key = jax.random.key(seed)
    ks = iter(jax.random.split(key, 64))
    f32 = jnp.float32

    def w(shape, fan_in, mult=1.0):
        return jax.random.normal(next(ks), shape, f32) * (mult * fan_in ** -0.5)

    def gain(shape):
        return 1.0 + 0.02 * jax.random.normal(next(ks), shape, f32)

    def bias(shape):
        return 0.01 * jax.random.normal(next(ks), shape, f32)

    D, E, F = D_MODEL, N_EXPERTS, D_EXPERT
    return {
        'x': jax.random.normal(next(ks), (BATCH, SEQ, D), f32),
        'c': jax.random.normal(next(ks), (BATCH, D), f32),
        'ctx': jax.random.normal(next(ks), (BATCH, CTX_LEN, D), f32),
        'c_ctx': jax.random.normal(next(ks), (D,), f32),
        'l0_mod_w': w((D, 6 * D), D, 0.5),
        'l0_mod_b': bias((6 * D,)),
        'l0_norm1_g': gain((D,)),
        'l0_w_in': w((D, IN0_WIDTH), D),
        'l0_q_norm_g': gain((HEAD_DIM,)),
        'l0_k_norm_g': gain((HEAD_DIM,)),
        'l0_dw_w': w((CONV_K, 1, CONV_WIDTH), CONV_K),
        'l0_dw_b': bias((CONV_WIDTH,)),
        'l0_conv_ln_g': gain((CONV_WIDTH,)),
        'l0_conv_ln_b': bias((CONV_WIDTH,)),
        'l0_w_out': w((D, D), D),
        'l0_norm2_g': gain((D,)),
        'l0_router': w((D, E), D),
        'l0_w_gate': w((E, D, F), D),
        'l0_w_up': w((E, D, F), D),
        'l0_w_down': w((E, F, D), F),
        'l1_mod_w': w((D, 6 * D), D, 0.5),
        'l1_mod_b': bias((6 * D,)),
        'l1_norm1_g': gain((D,)),
        'l1_w_dqkv': w((D, Q_LORA + KV_LORA + ROPE_DIM), D),
        'l1_q_lora_norm_g': gain((Q_LORA,)),
        'l1_w_uq': w((Q_LORA, MLA_HEADS * QK_DIM), Q_LORA),
        'l1_kv_lora_norm_g': gain((KV_LORA,)),
        'l1_w_ukv': w((KV_LORA, MLA_HEADS * (NOPE_DIM + V_DIM)), KV_LORA),
        'l1_w_out': w((MLA_HEADS * V_DIM, D), MLA_HEADS * V_DIM),
        'l1_norm2_g': gain((D,)),
        'l1_router': w((D, E), D),
        'l1_w_gate': w((E, D, F), D),
        'l1_w_up': w((E, D, F), D),
        'l1_w_down': w((E, F, D), F),
        'final_norm_g': gain((D,)),
    }


def reference(x, c, ctx, c_ctx,
              l0_mod_w, l0_mod_b, l0_norm1_g, l0_w_in, l0_q_norm_g, l0_k_norm_g, l0_dw_w, l0_dw_b,
              l0_conv_ln_g, l0_conv_ln_b, l0_w_out, l0_norm2_g, l0_router, l0_w_gate, l0_w_up, l0_w_down,
              l1_mod_w, l1_mod_b, l1_norm1_g, l1_w_dqkv, l1_q_lora_norm_g, l1_w_uq, l1_kv_lora_norm_g,
              l1_w_ukv, l1_w_out, l1_norm2_g, l1_router, l1_w_gate, l1_w_up, l1_w_down,
              final_norm_g):
    layers = [
        (mixer_ab, (l0_w_in, l0_q_norm_g, l0_k_norm_g, l0_dw_w, l0_dw_b, l0_conv_ln_g, l0_conv_ln_b, l0_w_out),
         (l0_mod_w, l0_mod_b, l0_norm1_g, l0_norm2_g, l0_router, l0_w_gate, l0_w_up, l0_w_down)),
        (mixer_mla, (l1_w_dqkv, l1_q_lora_norm_g, l1_w_uq, l1_kv_lora_norm_g, l1_w_ukv, l1_w_out),
         (l1_mod_w, l1_mod_b, l1_norm1_g, l1_norm2_g, l1_router, l1_w_gate, l1_w_up, l1_w_down)),
    ]
    xc = ctx
    for i in range(DEPTH):
        mixer, mix_p, (mod_w, mod_b, g1, g2, router, wg, wu, wd) = layers[i]
        need_ctx = i < DEPTH - 1
        sh1, sc1, ga1, sh2, sc2, ga2 = [m[:, None, :] for m in ada_params(c, mod_w, mod_b)]
        csh1, csc1, cga1, csh2, csc2, cga2 = ada_params(c_ctx, mod_w, mod_b)
        h_lat = modulate(rmsnorm(x, g1), sh1, sc1)
        h_ctx = modulate(rmsnorm(xc, g1), csh1, csc1)
        m_lat, m_ctx = mixer(h_lat, h_ctx, need_ctx, *mix_p)
        x = x + ga1 * m_lat
        x = x + ga2 * ec_moe(modulate(rmsnorm(x, g2), sh2, sc2), router, wg, wu, wd)
        if need_ctx:
            xc = xc + cga1 * m_ctx
            xc = xc + cga2 * ec_moe(modulate(rmsnorm(xc, g2), csh2, csc2), router, wg, wu, wd)
    return rmsnorm(x, final_norm_g)
```

```python
import functools

import jax
import jax.numpy as jnp
from jax import lax
from jax.experimental import pallas as pl
from jax.experimental.pallas import tpu as pltpu

F32 = jnp.float32
BF16 = jnp.bfloat16

D_MODEL = 2048
GRID_W = 64
EPS = 1e-6
ROPE_THETA = 10000.0
HEAD_DIM = 128
N_Q_HEADS = 8
N_KV_HEADS = 2
ATTN_WIDTH = N_Q_HEADS * HEAD_DIM
KV_WIDTH = N_KV_HEADS * HEAD_DIM
CONV_WIDTH = D_MODEL - ATTN_WIDTH
CONV_K = 31
GQA_SCALE = HEAD_DIM ** -0.5
MLA_HEADS = 16
Q_LORA = 1536
KV_LORA = 512
NOPE_DIM = 128
ROPE_DIM = 64
V_DIM = 128
QK_DIM = NOPE_DIM + ROPE_DIM
MLA_SCALE = QK_DIM ** -0.5
N_EXPERTS = 16
D_EXPERT = 1024
EC_FACTOR = 2

LANES = 128
VMEM_LIMIT = 56 * 1024 * 1024
CONV_PAD = 16


def _cparams(*sem):
    return pltpu.CompilerParams(dimension_semantics=sem, vmem_limit_bytes=VMEM_LIMIT)


def _dot(a, b):
    return jnp.dot(a, b, preferred_element_type=F32)


def _sigmoid(x):
    return 1.0 / (1.0 + jnp.exp(-x))


def _prenorm(x, g, sh, sc):
    ms = jnp.mean(x * x, axis=-1, keepdims=True)
    y = x * lax.rsqrt(ms + EPS) * g
    return y * (1.0 + sc) + sh


def _mod_spec(arr):
    if arr.shape[0] > 1:
        return pl.BlockSpec((1, 1, arr.shape[2]), lambda b, *_: (b, 0, 0))
    return pl.BlockSpec((1, 1, arr.shape[2]), lambda b, *_: (0, 0, 0))


def _full_spec(arr):
    nd = arr.ndim
    return pl.BlockSpec(arr.shape, lambda *_: (0,) * nd)


def _ada_kernel(c_ref, w_ref, b_ref, o_ref):
    c = c_ref[...]
    s = c * _sigmoid(c)
    s_hi = s.astype(BF16)
    s_lo = (s - s_hi.astype(F32)).astype(BF16)
    w = w_ref[...]
    w_hi = w.astype(BF16)
    w_lo = (w - w_hi.astype(F32)).astype(BF16)
    o_ref[...] = _dot(s_hi, w_hi) + _dot(s_lo, w_hi) + _dot(s_hi, w_lo) + b_ref[...]


def _ada(cond, w, b):
    m, d = cond.shape
    n = w.shape[1]
    tn = 512
    return pl.pallas_call(
        _ada_kernel,
        out_shape=jax.ShapeDtypeStruct((m, n), F32),
        grid=(n // tn,),
        in_specs=[pl.BlockSpec((m, d), lambda j: (0, 0)),
                  pl.BlockSpec((d, tn), lambda j: (0, j)),
                  pl.BlockSpec((1, tn), lambda j: (0, j))],
        out_specs=pl.BlockSpec((m, tn), lambda j: (0, j)),
        compiler_params=_cparams("parallel"),
        name="ada_params",
    )(cond, w, b.reshape(1, n))


def _rope_half(y, cos, sin):
    return y * cos + pltpu.roll(y, HEAD_DIM // 2, 1) * sin


def _l0_inproj_kernel(x_ref, g_ref, sh_ref, sc_ref, w_ref, qg_ref, kg_ref, cos_ref, sin_ref,
                      q_ref, ug_ref, kv_ref, *, rope):
    h = _prenorm(x_ref[0], g_ref[...], sh_ref[0], sc_ref[0]).astype(BF16)
    cos = cos_ref[...]
    sin = sin_ref[...]

    def head(y, gain, scale):
        ms = jnp.mean(y * y, axis=-1, keepdims=True)
        y = y * lax.rsqrt(ms + EPS) * gain
        if rope:
            y = _rope_half(y, cos, sin)
        return y * scale if scale != 1.0 else y

    cw = 4 * HEAD_DIM
    for c in range(ATTN_WIDTH // cw):
        y = _dot(h, w_ref[:, c * cw:(c + 1) * cw])
        parts = [head(y[:, j * HEAD_DIM:(j + 1) * HEAD_DIM], qg_ref[...], GQA_SCALE) for j in range(4)]
        q_ref[0, :, c * cw:(c + 1) * cw] = jnp.concatenate(parts, axis=1).astype(BF16)
    u0 = ATTN_WIDTH
    g0 = ATTN_WIDTH + CONV_WIDTH
    for c in range(CONV_WIDTH // cw):
        u = _dot(h, w_ref[:, u0 + c * cw:u0 + (c + 1) * cw])
        gt = _dot(h, w_ref[:, g0 + c * cw:g0 + (c + 1) * cw])
        ug_ref[0, :, c * cw:(c + 1) * cw] = (u * _sigmoid(gt)).astype(BF16)
    k0 = ATTN_WIDTH + 2 * CONV_WIDTH
    y = _dot(h, w_ref[:, k0:k0 + 2 * KV_WIDTH])
    parts = [head(y[:, j * HEAD_DIM:(j + 1) * HEAD_DIM], kg_ref[...], 1.0) for j in range(N_KV_HEADS)]
    parts.append(y[:, KV_WIDTH:])
    kv_ref[0] = jnp.concatenate(parts, axis=1).astype(BF16)


def _l0_inproj(x, g, sh, sc, w, qg, kg, cos, sin, *, rope, tm):
    b, l, d = x.shape
    n = w.shape[1]
    row = lambda bi, i: (bi, i, 0)
    return pl.pallas_call(
        functools.partial(_l0_inproj_kernel, rope=rope),
        out_shape=(jax.ShapeDtypeStruct((b, l, ATTN_WIDTH), BF16),
                   jax.ShapeDtypeStruct((b, l, CONV_WIDTH), BF16),
                   jax.ShapeDtypeStruct((b, l, 2 * KV_WIDTH), BF16)),
        grid=(b, l // tm),
        in_specs=[pl.BlockSpec((1, tm, d), row), _full_spec(g), _mod_spec(sh), _mod_spec(sc),
                  _full_spec(w), _full_spec(qg), _full_spec(kg),
                  pl.BlockSpec((tm, LANES), lambda bi, i: (i, 0)),
                  pl.BlockSpec((tm, LANES), lambda bi, i: (i, 0))],
        out_specs=(pl.BlockSpec((1, tm, ATTN_WIDTH), row),
                   pl.BlockSpec((1, tm, CONV_WIDTH), row),
                   pl.BlockSpec((1, tm, 2 * KV_WIDTH), row)),
        compiler_params=_cparams("parallel", "parallel"),
        name="l0_inproj",
    )(x, g, sh, sc, w, qg, kg, cos, sin)


def _attend(q, k, v):
    s = lax.dot_general(q, k, (((1,), (1,)), ((), ())), preferred_element_type=F32)
    m = jnp.max(s, axis=-1, keepdims=True)
    p = jnp.exp(s - m)
    l = jnp.sum(p, axis=-1, keepdims=True)
    return _dot(p.astype(BF16), v) / l


def _gqa_kernel(q_ref, k_ref, v_ref, o_ref):
    k = k_ref[0]
    v = v_ref[0]
    for j in range(N_Q_HEADS // N_KV_HEADS):
        sl = slice(j * HEAD_DIM, (j + 1) * HEAD_DIM)
        o_ref[0, :, sl] = _attend(q_ref[0, :, sl], k, v).astype(BF16)


def _gqa_attention(q, kv, *, tq):
    b, l, _ = q.shape
    lk = kv.shape[1]
    gw = ATTN_WIDTH // N_KV_HEADS
    return pl.pallas_call(
        _gqa_kernel,
        out_shape=jax.ShapeDtypeStruct((b, l, ATTN_WIDTH), BF16),
        grid=(b, N_KV_HEADS, l // tq),
        in_specs=[pl.BlockSpec((1, tq, gw), lambda bi, h, i: (bi, i, h)),
                  pl.BlockSpec((1, lk, HEAD_DIM), lambda bi, h, i: (bi, 0, h)),
                  pl.BlockSpec((1, lk, HEAD_DIM), lambda bi, h, i: (bi, 0, N_KV_HEADS + h))],
        out_specs=pl.BlockSpec((1, tq, gw), lambda bi, h, i: (bi, i, h)),
        compiler_params=_cparams("parallel", "parallel", "arbitrary"),
        name="gqa_attention",
    )(q, kv, kv)


MLA_HB = 2


def _mla_kernel(qn_ref, qp_ref, k_ref, v_ref, kpe_ref, o_ref, kcat_ref):
    @pl.when(pl.program_id(2) == 0)
    def _():
        for i in range(MLA_HB):
            kcat_ref[i, :, 0:NOPE_DIM] = k_ref[0, :, i * NOPE_DIM:(i + 1) * NOPE_DIM]
            kcat_ref[i, :, NOPE_DIM:2 * NOPE_DIM] = kpe_ref[0]

    qp = qp_ref[0]
    lane = lax.broadcasted_iota(jnp.int32, qp.shape, 1)
    for i in range(MLA_HB):
        mine = (lane >= i * ROPE_DIM) & (lane < (i + 1) * ROPE_DIM)
        qcat = jnp.concatenate(
            [qn_ref[0, :, i * NOPE_DIM:(i + 1) * NOPE_DIM], jnp.where(mine, qp, jnp.zeros_like(qp))], axis=1)
        o = _attend(qcat, kcat_ref[i], v_ref[0, :, i * V_DIM:(i + 1) * V_DIM])
        o_ref[0, :, i * V_DIM:(i + 1) * V_DIM] = o.astype(BF16)


def _mla_attention(qn, qp, kv, kpe, *, tq):
    b, l, _ = qn.shape
    lk = kv.shape[1]
    wn = MLA_HB * NOPE_DIM
    nv0 = MLA_HEADS * NOPE_DIM // wn
    return pl.pallas_call(
        _mla_kernel,
        out_shape=jax.ShapeDtypeStruct((b, l, MLA_HEADS * V_DIM), BF16),
        grid=(b, MLA_HEADS // MLA_HB, l // tq),
        in_specs=[pl.BlockSpec((1, tq, wn), lambda bi, h, i: (bi, i, h)),
                  pl.BlockSpec((1, tq, MLA_HB * ROPE_DIM), lambda bi, h, i: (bi, i, h)),
                  pl.BlockSpec((1, lk, wn), lambda bi, h, i: (bi, 0, h)),
                  pl.BlockSpec((1, lk, wn), lambda bi, h, i: (bi, 0, nv0 + h)),
                  pl.BlockSpec((1, lk, LANES), lambda bi, h, i: (bi, 0, 0))],
        out_specs=pl.BlockSpec((1, tq, wn), lambda bi, h, i: (bi, i, h)),
        scratch_shapes=[pltpu.VMEM((MLA_HB, lk, 2 * NOPE_DIM), BF16)],
        compiler_params=_cparams("parallel", "parallel", "arbitrary"),
        name="mla_attention",
    )(qn, qp, kv, kv, kpe)


CONV_ROWS = 32
CONV_COLS = 256


def _conv_kernel(ug_ref, w_ref, b_ref, lg_ref, lb_ref, o_ref, pad_ref, y_ref):
    l, c = ug_ref.shape[1], ug_ref.shape[2]
    pad_ref[0:CONV_PAD, :] = jnp.zeros((CONV_PAD, c), F32)
    pad_ref[CONV_PAD + l:2 * CONV_PAD + l, :] = jnp.zeros((CONV_PAD, c), F32)
    pad_ref[CONV_PAD:CONV_PAD + l, :] = ug_ref[0].astype(F32)
    off = CONV_PAD - CONV_K // 2

    def body(r, carry):
        r0 = pl.multiple_of(r * CONV_ROWS, CONV_ROWS)
        for cb in range(c // CONV_COLS):
            cs = slice(cb * CONV_COLS, (cb + 1) * CONV_COLS)
            win = pad_ref[pl.ds(r0, CONV_ROWS + 2 * CONV_PAD), cs]
            acc = jnp.zeros((CONV_ROWS, CONV_COLS), F32) + b_ref[:, cs]
            for k in range(CONV_K):
                acc = acc + w_ref[k:k + 1, cs] * win[off + k:off + k + CONV_ROWS, :]
            y_ref[:, cs] = acc
        y = y_ref[...]
        mu = jnp.mean(y, axis=-1, keepdims=True)
        yc = y - mu
        var = jnp.mean(yc * yc, axis=-1, keepdims=True)
        z = yc * lax.rsqrt(var + EPS) * lg_ref[...] + lb_ref[...]
        o_ref[0, pl.ds(r0, CONV_ROWS), :] = (z * _sigmoid(z)).astype(BF16)
        return carry

    lax.fori_loop(0, l // CONV_ROWS, body, 0)


def _conv_branch(ug, w, b, lg, lb):
    bsz, l, c = ug.shape
    return pl.pallas_call(
        _conv_kernel,
        out_shape=jax.ShapeDtypeStruct((bsz, l, c), BF16),
        grid=(bsz,),
        in_specs=[pl.BlockSpec((1, l, c), lambda bi: (bi, 0, 0)),
                  _full_spec(w), _full_spec(b), _full_spec(lg), _full_spec(lb)],
        out_specs=pl.BlockSpec((1, l, c), lambda bi: (bi, 0, 0)),
        scratch_shapes=[pltpu.VMEM((l + 2 * CONV_PAD, c), F32), pltpu.VMEM((CONV_ROWS, c), F32)],
        compiler_params=_cparams("parallel"),
        name="conv_branch",
    )(ug, w, b, lg, lb)


def _outproj_kernel(*refs, n_in):
    x_ref = refs[0]
    a_refs = refs[1:1 + n_in]
    w_refs = refs[1 + n_in:1 + 2 * n_in]
    ga_ref, g2_ref, sh2_ref, sc2_ref, r_ref, xo_ref, h2_ref, lg_ref = refs[1 + 2 * n_in:]
    acc = _dot(a_refs[0][0], w_refs[0][...])
    for a_ref, w_ref in zip(a_refs[1:], w_refs[1:]):
        acc = acc + _dot(a_ref[0], w_ref[...])
    xn = x_ref[0] + ga_ref[0] * acc
    xo_ref[0] = xn
    h2 = _prenorm(xn, g2_ref[...], sh2_ref[0], sc2_ref[0]).astype(BF16)
    h2_ref[0] = h2
    lg_ref[0] = _dot(h2, r_ref[...])


def _outproj(x, acts, ws, ga, g2, sh2, sc2, router, *, tm):
    b, l, d = x.shape
    n_in = len(acts)
    row = lambda bi, i: (bi, i, 0)
    in_specs = [pl.BlockSpec((1, tm, d), row)]
    in_specs += [pl.BlockSpec((1, tm, a.shape[2]), row) for a in acts]
    in_specs += [_full_spec(w) for w in ws]
    in_specs += [_mod_spec(ga), _full_spec(g2), _mod_spec(sh2), _mod_spec(sc2), _full_spec(router)]
    return pl.pallas_call(
        functools.partial(_outproj_kernel, n_in=n_in),
        out_shape=(jax.ShapeDtypeStruct((b, l, d), F32),
                   jax.ShapeDtypeStruct((b, l, d), BF16),
                   jax.ShapeDtypeStruct((b, l, LANES), F32)),
        grid=(b, l // tm),
        in_specs=in_specs,
        out_specs=(pl.BlockSpec((1, tm, d), row), pl.BlockSpec((1, tm, d), row),
                   pl.BlockSpec((1, tm, LANES), row)),
        compiler_params=_cparams("parallel", "parallel"),
        name="outproj",
    )(x, *acts, *ws, ga, g2, sh2, sc2, router)


MOE_TM = 256


def _moe_kernel(xs_ref, gate_ref, wg_ref, wu_ref, wd_ref, o_ref):
    x = xs_ref[0]
    g = _dot(x, wg_ref[0])
    u = _dot(x, wu_ref[0])
    hid = (g * _sigmoid(g) * u).astype(BF16)
    o_ref[0] = _dot(hid, wd_ref[0]) * gate_ref[0]


def _moe_ffn(xs, gates, wg, wu, wd):
    e, m, d = xs.shape
    f = wg.shape[2]
    return pl.pallas_call(
        _moe_kernel,
        out_shape=jax.ShapeDtypeStruct((e, m, d), F32),
        grid=(e, m // MOE_TM),
        in_specs=[pl.BlockSpec((1, MOE_TM, d), lambda ei, i: (ei, i, 0)),
                  pl.BlockSpec((1, MOE_TM, 1), lambda ei, i: (ei, i, 0)),
                  pl.BlockSpec((1, d, f), lambda ei, i: (ei, 0, 0)),
                  pl.BlockSpec((1, d, f), lambda ei, i: (ei, 0, 0)),
                  pl.BlockSpec((1, f, d), lambda ei, i: (ei, 0, 0))],
        out_specs=pl.BlockSpec((1, MOE_TM, d), lambda ei, i: (ei, i, 0)),
        compiler_params=_cparams("parallel", "arbitrary"),
        name="moe_ffn",
    )(xs, gates, wg, wu, wd)


def _ec_moe(h2, logits, wg, wu, wd):
    b, n, d = h2.shape
    cap = EC_FACTOR * n // N_EXPERTS
    aff = jax.nn.softmax(logits[..., :N_EXPERTS], axis=-1)
    gate, idx = lax.top_k(jnp.swapaxes(aff, 1, 2), cap)
    idx_e = jnp.swapaxes(idx, 0, 1)
    bidx = jnp.arange(b, dtype=idx.dtype)[None, :, None]
    xs = h2[bidx, idx_e].reshape(N_EXPERTS, b * cap, d)
    gates = jnp.swapaxes(gate, 0, 1).reshape(N_EXPERTS, b * cap, 1)
    ys = _moe_ffn(xs, gates, wg, wu, wd).reshape(N_EXPERTS, b, cap, d)
    return jnp.zeros((b, n, d), F32).at[bidx, idx_e].add(ys)


def _rope_quarter(y, cos, sin, first):
    rot = jnp.where(first, pltpu.roll(y, LANES - ROPE_DIM // 2, 1), pltpu.roll(y, ROPE_DIM // 2, 1))
    return y * cos + rot * sin


def _l1_dqkv_kernel(*refs, with_q, residual, rope):
    it = iter(refs)
    x_ref = next(it)
    if residual:
        moe_ref, ga2_ref = next(it), next(it)
    g_ref, sh_ref, sc_ref, w_ref, qg_ref, kvg_ref, cos_ref, sin_ref = (next(it) for _ in range(8))
    if residual:
        xo_ref = next(it)
    if with_q:
        cq_ref = next(it)
    ckv_ref, kpe_ref = next(it), next(it)

    x = x_ref[0]
    if residual:
        x = x + ga2_ref[0] * moe_ref[0]
        xo_ref[0] = x
    h = _prenorm(x, g_ref[...], sh_ref[0], sc_ref[0]).astype(BF16)
    if with_q:
        cq = _dot(h, w_ref[:, 0:Q_LORA])
        ms = jnp.mean(cq * cq, axis=-1, keepdims=True)
        cq_ref[0] = (cq * lax.rsqrt(ms + EPS) * qg_ref[...]).astype(BF16)
    z = _dot(h, w_ref[:, Q_LORA:Q_LORA + KV_LORA + LANES])
    ckv = z[:, 0:KV_LORA]
    ms = jnp.mean(ckv * ckv, axis=-1, keepdims=True)
    ckv_ref[0] = (ckv * lax.rsqrt(ms + EPS) * kvg_ref[...]).astype(BF16)
    pe = z[:, KV_LORA:KV_LORA + LANES]
    if rope:
        lane = lax.broadcasted_iota(jnp.int32, pe.shape, 1)
        pe = _rope_quarter(pe, cos_ref[...], sin_ref[...], (lane % ROPE_DIM) < ROPE_DIM // 2)
    kpe_ref[0] = (pe + pltpu.roll(pe, ROPE_DIM, 1)).astype(BF16)


def _l1_dqkv(x, moe, ga2, g, sh, sc, w, qg, kvg, cos, sin, *, rope, with_q, tm):
    b, l, d = x.shape
    residual = moe is not None
    row = lambda bi, i: (bi, i, 0)
    args, in_specs = [x], [pl.BlockSpec((1, tm, d), row)]
    if residual:
        args += [moe, ga2]
        in_specs += [pl.BlockSpec((1, tm, d), row), _mod_spec(ga2)]
    args += [g, sh, sc, w, qg, kvg, cos, sin]
    in_specs += [_full_spec(g), _mod_spec(sh), _mod_spec(sc), _full_spec(w), _full_spec(qg), _full_spec(kvg),
                 pl.BlockSpec((tm, LANES), lambda bi, i: (i, 0)), pl.BlockSpec((tm, LANES), lambda bi, i: (i, 0))]
    out_shape, out_specs = [], []
    if residual:
        out_shape.append(jax.ShapeDtypeStruct((b, l, d), F32))
        out_specs.append(pl.BlockSpec((1, tm, d), row))
    if with_q:
        out_shape.append(jax.ShapeDtypeStruct((b, l, Q_LORA), BF16))
        out_specs.append(pl.BlockSpec((1, tm, Q_LORA), row))
    out_shape += [jax.ShapeDtypeStruct((b, l, KV_LORA), BF16), jax.ShapeDtypeStruct((b, l, LANES), BF16)]
    out_specs += [pl.BlockSpec((1, tm, KV_LORA), row), pl.BlockSpec((1, tm, LANES), row)]

    return pl.pallas_call(
        functools.partial(_l1_dqkv_kernel, with_q=with_q, residual=residual, rope=rope),
        out_shape=tuple(out_shape),
        grid=(b, l // tm),
        in_specs=in_specs,
        out_specs=tuple(out_specs),
        compiler_params=_cparams("parallel", "parallel"),
        name="l1_dqkv",
    )(*args)


def _l1_q_kernel(cq_ref, w_ref, cos_ref, sin_ref, qn_ref, qp_ref):
    a = cq_ref[0]
    cw = 4 * LANES
    n_nope = MLA_HEADS * NOPE_DIM
    for c in range(n_nope // cw):
        qn_ref[0, :, c * cw:(c + 1) * cw] = (_dot(a, w_ref[:, c * cw:(c + 1) * cw]) * MLA_SCALE).astype(BF16)
    cos = cos_ref[...]
    sin = sin_ref[...]
    lane = lax.broadcasted_iota(jnp.int32, cos.shape, 1)
    first = (lane % ROPE_DIM) < ROPE_DIM // 2
    for c in range(MLA_HEADS * ROPE_DIM // cw):
        y = _dot(a, w_ref[:, n_nope + c * cw:n_nope + (c + 1) * cw])
        parts = [_rope_quarter(y[:, j * LANES:(j + 1) * LANES], cos, sin, first) * MLA_SCALE for j in range(4)]
        qp_ref[0, :, c * cw:(c + 1) * cw] = jnp.concatenate(parts, axis=1).astype(BF16)


def _l1_q(cq, w, cos, sin, *, tm):
    b, l, k = cq.shape
    row = lambda bi, i: (bi, i, 0)
    return pl.pallas_call(
        _l1_q_kernel,
        out_shape=(jax.ShapeDtypeStruct((b, l, MLA_HEADS * NOPE_DIM), BF16),
                   jax.ShapeDtypeStruct((b, l, MLA_HEADS * ROPE_DIM), BF16)),
        grid=(b, l // tm),
        in_specs=[pl.BlockSpec((1, tm, k), row), _full_spec(w),
                  pl.BlockSpec((tm, LANES), lambda bi, i: (i, 0)), pl.BlockSpec((tm, LANES), lambda bi, i: (i, 0))],
        out_specs=(pl.BlockSpec((1, tm, MLA_HEADS * NOPE_DIM), row),
                   pl.BlockSpec((1, tm, MLA_HEADS * ROPE_DIM), row)),
        compiler_params=_cparams("parallel", "parallel"),
        name="l1_q",
    )(cq, w, cos, sin)


def _mm_kernel(a_ref, w_ref, o_ref):
    a = a_ref[0]
    cw = 4 * LANES
    for c in range(w_ref.shape[1] // cw):
        o_ref[0, :, c * cw:(c + 1) * cw] = _dot(a, w_ref[:, c * cw:(c + 1) * cw]).astype(o_ref.dtype)


def _mm(a, w, *, tm):
    b, l, k = a.shape
    n = w.shape[1]
    row = lambda bi, i: (bi, i, 0)
    return pl.pallas_call(
        _mm_kernel,
        out_shape=jax.ShapeDtypeStruct((b, l, n), BF16),
        grid=(b, l // tm),
        in_specs=[pl.BlockSpec((1, tm, k), row), _full_spec(w)],
        out_specs=pl.BlockSpec((1, tm, n), row),
        compiler_params=_cparams("parallel", "parallel"),
        name="l1_kv_up",
    )(a, w)


def _final_kernel(x_ref, moe_ref, ga_ref, g_ref, o_ref):
    x = x_ref[0] + ga_ref[0] * moe_ref[0]
    ms = jnp.mean(x * x, axis=-1, keepdims=True)
    o_ref[0] = x * lax.rsqrt(ms + EPS) * g_ref[...]


def _final(x, moe, ga, g, *, tm):
    b, l, d = x.shape
    row = lambda bi, i: (bi, i, 0)
    return pl.pallas_call(
        _final_kernel,
        out_shape=jax.ShapeDtypeStruct((b, l, d), F32),
        grid=(b, l // tm),
        in_specs=[pl.BlockSpec((1, tm, d), row), pl.BlockSpec((1, tm, d), row), _mod_spec(ga), _full_spec(g)],
        out_specs=pl.BlockSpec((1, tm, d), row),
        compiler_params=_cparams("parallel", "parallel"),
        name="final_norm",
    )(x, moe, ga, g)


def _residual_kernel(x_ref, moe_ref, ga_ref, o_ref):
    o_ref[0] = x_ref[0] + ga_ref[0] * moe_ref[0]


def _residual(x, moe, ga, *, tm):
    b, l, d = x.shape
    row = lambda bi, i: (bi, i, 0)
    return pl.pallas_call(
        _residual_kernel,
        out_shape=jax.ShapeDtypeStruct((b, l, d), F32),
        grid=(b, l // tm),
        in_specs=[pl.BlockSpec((1, tm, d), row), pl.BlockSpec((1, tm, d), row), _mod_spec(ga)],
        out_specs=pl.BlockSpec((1, tm, d), row),
        compiler_params=_cparams("parallel", "parallel"),
        name="residual",
    )(x, moe, ga)


def _rope_tables(n_tokens, d_rot):
    rows = n_tokens // GRID_W
    row = jnp.repeat(jnp.arange(rows, dtype=F32), GRID_W)
    col = jnp.tile(jnp.arange(GRID_W, dtype=F32), rows)
    n_axis = d_rot // 4
    inv_freq = ROPE_THETA ** (-jnp.arange(n_axis, dtype=F32) / n_axis)
    ang = jnp.concatenate([row[:, None] * inv_freq, col[:, None] * inv_freq], axis=-1)
    cos, sin = jnp.cos(ang), jnp.sin(ang)
    cos_t = jnp.concatenate([cos, cos], axis=-1)
    sin_t = jnp.concatenate([-sin, sin], axis=-1)
    reps = LANES // d_rot
    return jnp.tile(cos_t, (1, reps)), jnp.tile(sin_t, (1, reps))


def _mods(mod, lo, hi):
    return [mod[lo:hi, None, i * D_MODEL:(i + 1) * D_MODEL] for i in range(6)]


def _pad_router(router):
    return jnp.pad(router, ((0, 0), (0, LANES - router.shape[1]))).astype(BF16)


def kernel(x, c, ctx, c_ctx, l0_mod_w, l0_mod_b, l0_norm1_g, l0_w_in, l0_q_norm_g, l0_k_norm_g, l0_dw_w, l0_dw_b, l0_conv_ln_g, l0_conv_ln_b, l0_w_out, l0_norm2_g, l0_router, l0_w_gate, l0_w_up, l0_w_down, l1_mod_w, l1_mod_b, l1_norm1_g, l1_w_dqkv, l1_q_lora_norm_g, l1_w_uq, l1_kv_lora_norm_g, l1_w_ukv, l1_w_out, l1_norm2_g, l1_router, l1_w_gate, l1_w_up, l1_w_down, final_norm_g):
    bsz, seq, d = x.shape
    n_ctx = ctx.shape[1]
    row2 = lambda v: v.reshape(1, -1)

    cond = jnp.zeros((16, d), F32).at[:bsz].set(c).at[bsz].set(c_ctx)
    mod0 = _ada(cond, l0_mod_w, l0_mod_b)
    mod1 = _ada(cond, l1_mod_w, l1_mod_b)

    sh1, sc1, ga1, sh2, sc2, ga2 = _mods(mod0, 0, bsz)
    csh1, csc1, cga1, csh2, csc2, cga2 = _mods(mod0, bsz, bsz + 1)
    s_q, s_k, s_v, s_u = ATTN_WIDTH, ATTN_WIDTH + KV_WIDTH, ATTN_WIDTH + 2 * KV_WIDTH, ATTN_WIDTH + 2 * KV_WIDTH + CONV_WIDTH
    w_in = jnp.concatenate([l0_w_in[:, :s_q], l0_w_in[:, s_v:s_u], l0_w_in[:, s_u:],
                            l0_w_in[:, s_q:s_k], l0_w_in[:, s_k:s_v]], axis=1).astype(BF16)
    cos0, sin0 = _rope_tables(seq, HEAD_DIM)
    g1 = row2(l0_norm1_g)
    qg, kg = row2(l0_q_norm_g), row2(l0_k_norm_g)
    q_l, ug_l, kv_l = _l0_inproj(x, g1, sh1, sc1, w_in, qg, kg, cos0, sin0, rope=True, tm=512)
    q_c, ug_c, kv_c = _l0_inproj(ctx, g1, csh1, csc1, w_in, qg, kg, cos0, sin0, rope=False, tm=n_ctx)
    kv_all = jnp.concatenate([kv_c, kv_l], axis=1)
    a_l = _gqa_attention(q_l, kv_all, tq=256)
    a_c = _gqa_attention(q_c, kv_c, tq=n_ctx)
    dw_w = l0_dw_w.reshape(CONV_K, CONV_WIDTH)
    dw_b, ln_g, ln_b = row2(l0_dw_b), row2(l0_conv_ln_g), row2(l0_conv_ln_b)
    cb_l = _conv_branch(ug_l, dw_w, dw_b, ln_g, ln_b)
    cb_c = _conv_branch(ug_c, dw_w, dw_b, ln_g, ln_b)
    w_out0 = l0_w_out.astype(BF16)
    wo_a, wo_c = w_out0[:ATTN_WIDTH], w_out0[ATTN_WIDTH:]
    g2 = row2(l0_norm2_g)
    router0 = _pad_router(l0_router)
    x1, h2_l, lg_l = _outproj(x, [a_l, cb_l], [wo_a, wo_c], ga1, g2, sh2, sc2, router0, tm=256)
    xc1, h2_c, lg_c = _outproj(ctx, [a_c, cb_c], [wo_a, wo_c], cga1, g2, csh2, csc2, router0, tm=n_ctx)
    wg0, wu0, wd0 = l0_w_gate.astype(BF16), l0_w_up.astype(BF16), l0_w_down.astype(BF16)
    moe_l = _ec_moe(h2_l, lg_l, wg0, wu0, wd0)
    moe_c = _ec_moe(h2_c, lg_c, wg0, wu0, wd0)
    xc2 = _residual(xc1, moe_c, cga2, tm=n_ctx)

    sh1, sc1, ga1, sh2, sc2, ga2_1 = _mods(mod1, 0, bsz)
    csh1, csc1 = _mods(mod1, bsz, bsz + 1)[:2]
    w_dqkv = jnp.pad(l1_w_dqkv, ((0, 0), (0, LANES - ROPE_DIM))).astype(BF16)
    cos1, sin1 = _rope_tables(seq, ROPE_DIM)
    g1 = row2(l1_norm1_g)
    qlg, kvlg = row2(l1_q_lora_norm_g), row2(l1_kv_lora_norm_g)
    x2, cq, ckv_l, kpe_l = _l1_dqkv(x1, moe_l, ga2, g1, sh1, sc1, w_dqkv, qlg, kvlg, cos1, sin1,
                                    rope=True, with_q=True, tm=256)
    ckv_c, kpe_c = _l1_dqkv(xc2, None, None, g1, csh1, csc1, w_dqkv, qlg, kvlg, cos1, sin1,
                            rope=False, with_q=False, tm=n_ctx)
    w_uq = l1_w_uq.reshape(Q_LORA, MLA_HEADS, QK_DIM)
    w_uq = jnp.concatenate([w_uq[:, :, :NOPE_DIM].reshape(Q_LORA, -1),
                            w_uq[:, :, NOPE_DIM:].reshape(Q_LORA, -1)], axis=1).astype(BF16)
    w_ukv = l1_w_ukv.reshape(KV_LORA, MLA_HEADS, NOPE_DIM + V_DIM)
    w_ukv = jnp.concatenate([w_ukv[:, :, :NOPE_DIM].reshape(KV_LORA, -1),
                             w_ukv[:, :, NOPE_DIM:].reshape(KV_LORA, -1)], axis=1).astype(BF16)
    qn, qp = _l1_q(cq, w_uq, cos1, sin1, tm=512)
    kv1 = _mm(jnp.concatenate([ckv_c, ckv_l], axis=1), w_ukv, tm=256)
    kpe = jnp.concatenate([kpe_c, kpe_l], axis=1)
    a1 = _mla_attention(qn, qp, kv1, kpe, tq=256)
    x3, h2, lg = _outproj(x2, [a1], [l1_w_out.astype(BF16)], ga1, row2(l1_norm2_g), sh2, sc2,
                          _pad_router(l1_router), tm=256)
    moe1 = _ec_moe(h2, lg, l1_w_gate.astype(BF16), l1_w_up.astype(BF16), l1_w_down.astype(BF16))
    return _final(x3, moe1, ga2_1, row2(final_norm_g), tm=512)
```

```python
import functools

import jax
import jax.numpy as jnp
from jax import lax
from jax.experimental import pallas as pl
from jax.experimental.pallas import tpu as pltpu

F32 = jnp.float32
BF16 = jnp.bfloat16

D_MODEL = 2048
GRID_W = 64
EPS = 1e-6
ROPE_THETA = 10000.0
HEAD_DIM = 128
N_Q_HEADS = 8
N_KV_HEADS = 2
ATTN_WIDTH = N_Q_HEADS * HEAD_DIM
KV_WIDTH = N_KV_HEADS * HEAD_DIM
CONV_WIDTH = D_MODEL - ATTN_WIDTH
CONV_K = 31
GQA_SCALE = HEAD_DIM ** -0.5
MLA_HEADS = 16
Q_LORA = 1536
KV_LORA = 512
NOPE_DIM = 128
ROPE_DIM = 64
V_DIM = 128
QK_DIM = NOPE_DIM + ROPE_DIM
MLA_SCALE = QK_DIM ** -0.5
N_EXPERTS = 16
D_EXPERT = 1024
EC_FACTOR = 2

LANES = 128
SUBLANES = 8
VMEM_LIMIT = 56 * 1024 * 1024
CONV_PAD = 16


def _cparams(*sem):
    return pltpu.CompilerParams(dimension_semantics=sem, vmem_limit_bytes=VMEM_LIMIT)


def _dot(a, b):
    return jnp.dot(a, b, preferred_element_type=F32)


def _sigmoid(x):
    return 1.0 / (1.0 + jnp.exp(-x))


def _prenorm(x, g, sh, sc):
    ms = jnp.mean(x * x, axis=-1, keepdims=True)
    y = x * lax.rsqrt(ms + EPS) * g
    return y * (1.0 + sc) + sh


def _mod_spec(arr):
    if arr.shape[0] > 1:
        return pl.BlockSpec((1, 1, arr.shape[2]), lambda b, *_: (b, 0, 0))
    return pl.BlockSpec((1, 1, arr.shape[2]), lambda b, *_: (0, 0, 0))


def _full_spec(arr):
    nd = arr.ndim
    return pl.BlockSpec(arr.shape, lambda *_: (0,) * nd)


def _ada_kernel(c_ref, w_ref, b_ref, o_ref):
    c = c_ref[...]
    s = c * _sigmoid(c)
    s_hi = s.astype(BF16)
    s_lo = (s - s_hi.astype(F32)).astype(BF16)
    w = w_ref[...]
    w_hi = w.astype(BF16)
    w_lo = (w - w_hi.astype(F32)).astype(BF16)
    o_ref[...] = _dot(s_hi, w_hi) + _dot(s_lo, w_hi) + _dot(s_hi, w_lo) + b_ref[...]


def _ada(cond, w, b):
    m, d = cond.shape
    n = w.shape[1]
    tn = 512
    return pl.pallas_call(
        _ada_kernel,
        out_shape=jax.ShapeDtypeStruct((m, n), F32),
        grid=(n // tn,),
        in_specs=[pl.BlockSpec((m, d), lambda j: (0, 0)),
                  pl.BlockSpec((d, tn), lambda j: (0, j)),
                  pl.BlockSpec((1, tn), lambda j: (0, j))],
        out_specs=pl.BlockSpec((m, tn), lambda j: (0, j)),
        compiler_params=_cparams("parallel"),
        name="ada_params",
    )(cond, w, b.reshape(1, n))


def _rope_half(y, cos, sin):
    return y * cos + pltpu.roll(y, HEAD_DIM // 2, 1) * sin


def _l0_inproj_kernel(x_ref, g_ref, sh_ref, sc_ref, w_ref, qg_ref, kg_ref, cos_ref, sin_ref,
                      q_ref, ug_ref, kv_ref, *, rope):
    h = _prenorm(x_ref[0], g_ref[...], sh_ref[0], sc_ref[0]).astype(BF16)
    cos = cos_ref[...]
    sin = sin_ref[...]

    def head(y, gain, scale):
        ms = jnp.mean(y * y, axis=-1, keepdims=True)
        y = y * lax.rsqrt(ms + EPS) * gain
        if rope:
            y = _rope_half(y, cos, sin)
        return y * scale if scale != 1.0 else y

    cw = 4 * HEAD_DIM
    for c in range(ATTN_WIDTH // cw):
        y = _dot(h, w_ref[:, c * cw:(c + 1) * cw])
        parts = [head(y[:, j * HEAD_DIM:(j + 1) * HEAD_DIM], qg_ref[...], GQA_SCALE) for j in range(4)]
        q_ref[0, :, c * cw:(c + 1) * cw] = jnp.concatenate(parts, axis=1).astype(BF16)
    u0 = ATTN_WIDTH
    g0 = ATTN_WIDTH + CONV_WIDTH
    for c in range(CONV_WIDTH // cw):
        u = _dot(h, w_ref[:, u0 + c * cw:u0 + (c + 1) * cw])
        gt = _dot(h, w_ref[:, g0 + c * cw:g0 + (c + 1) * cw])
        ug_ref[0, :, c * cw:(c + 1) * cw] = (u * _sigmoid(gt)).astype(BF16)
    k0 = ATTN_WIDTH + 2 * CONV_WIDTH
    y = _dot(h, w_ref[:, k0:k0 + 2 * KV_WIDTH])
    parts = [head(y[:, j * HEAD_DIM:(j + 1) * HEAD_DIM], kg_ref[...], 1.0) for j in range(N_KV_HEADS)]
    parts.append(y[:, KV_WIDTH:])
    kv_ref[0] = jnp.concatenate(parts, axis=1).astype(BF16)


def _l0_inproj(x, g, sh, sc, w, qg, kg, cos, sin, *, rope, tm):
    b, l, d = x.shape
    n = w.shape[1]
    row = lambda bi, i: (bi, i, 0)
    return pl.pallas_call(
        functools.partial(_l0_inproj_kernel, rope=rope),
        out_shape=(jax.ShapeDtypeStruct((b, l, ATTN_WIDTH), BF16),
                   jax.ShapeDtypeStruct((b, l, CONV_WIDTH), BF16),
                   jax.ShapeDtypeStruct((b, l, 2 * KV_WIDTH), BF16)),
        grid=(b, l // tm),
        in_specs=[pl.BlockSpec((1, tm, d), row), _full_spec(g), _mod_spec(sh), _mod_spec(sc),
                  _full_spec(w), _full_spec(qg), _full_spec(kg),
                  pl.BlockSpec((tm, LANES), lambda bi, i: (i, 0)),
                  pl.BlockSpec((tm, LANES), lambda bi, i: (i, 0))],
        out_specs=(pl.BlockSpec((1, tm, ATTN_WIDTH), row),
                   pl.BlockSpec((1, tm, CONV_WIDTH), row),
                   pl.BlockSpec((1, tm, 2 * KV_WIDTH), row)),
        compiler_params=_cparams("parallel", "parallel"),
        name="l0_inproj",
    )(x, g, sh, sc, w, qg, kg, cos, sin)


ATT_ROWS = 256


def _scores(q, k):
    return lax.dot_general(q, k, (((1,), (1,)), ((), ())), preferred_element_type=F32)


def _softmax_pv(s, v):
    m = jnp.max(s, axis=-1, keepdims=True)
    p = jnp.exp(s - m)
    l = jnp.sum(p, axis=-1, keepdims=True)
    return _dot(p.astype(BF16), v) / l


def _attend_units(n_units, q_of, k_of, v_of, store):
    s = _scores(q_of(0), k_of(0))
    for n in range(n_units):
        s_cur = s
        if n + 1 < n_units:
            s = _scores(q_of(n + 1), k_of(n + 1))
        store(n, _softmax_pv(s_cur, v_of(n)).astype(BF16))


def _gqa_kernel(q_ref, k_ref, v_ref, o_ref):
    k = k_ref[0]
    v = v_ref[0]
    group = N_Q_HEADS // N_KV_HEADS
    n_units = q_ref.shape[1] // ATT_ROWS * group

    def where(n):
        r, j = divmod(n, group)
        return slice(r * ATT_ROWS, (r + 1) * ATT_ROWS), slice(j * HEAD_DIM, (j + 1) * HEAD_DIM)

    def store(n, o):
        rows, cols = where(n)
        o_ref[0, rows, cols] = o

    def q_of(n):
        rows, cols = where(n)
        return q_ref[0, rows, cols]

    _attend_units(n_units, q_of, lambda n: k, lambda n: v, store)


def _gqa_attention(q, kv, *, tq):
    b, l, _ = q.shape
    lk = kv.shape[1]
    gw = ATTN_WIDTH // N_KV_HEADS
    return pl.pallas_call(
        _gqa_kernel,
        out_shape=jax.ShapeDtypeStruct((b, l, ATTN_WIDTH), BF16),
        grid=(b, N_KV_HEADS, l // tq),
        in_specs=[pl.BlockSpec((1, tq, gw), lambda bi, h, i: (bi, i, h)),
                  pl.BlockSpec((1, lk, HEAD_DIM), lambda bi, h, i: (bi, 0, h)),
                  pl.BlockSpec((1, lk, HEAD_DIM), lambda bi, h, i: (bi, 0, N_KV_HEADS + h))],
        out_specs=pl.BlockSpec((1, tq, gw), lambda bi, h, i: (bi, i, h)),
        compiler_params=_cparams("parallel", "parallel", "arbitrary"),
        name="gqa_attention",
    )(q, kv, kv)


MLA_HB = 4


def _mla_kernel(qn_ref, qp_ref, k_ref, v_ref, kpe_ref, o_ref, kcat_ref):
    @pl.when(pl.program_id(2) == 0)
    def _():
        for i in range(MLA_HB):
            kcat_ref[i, :, 0:NOPE_DIM] = k_ref[0, :, i * NOPE_DIM:(i + 1) * NOPE_DIM]
            kcat_ref[i, :, NOPE_DIM:2 * NOPE_DIM] = kpe_ref[0]

    lane = lax.broadcasted_iota(jnp.int32, (qp_ref.shape[1], LANES), 1)
    per_block = LANES // ROPE_DIM

    def q_of(i):
        blk, pos = divmod(i, per_block)
        qp = qp_ref[0, :, blk * LANES:(blk + 1) * LANES]
        mine = (lane >= pos * ROPE_DIM) & (lane < (pos + 1) * ROPE_DIM)
        return jnp.concatenate(
            [qn_ref[0, :, i * NOPE_DIM:(i + 1) * NOPE_DIM], jnp.where(mine, qp, jnp.zeros_like(qp))], axis=1)

    def store(i, o):
        o_ref[0, :, i * V_DIM:(i + 1) * V_DIM] = o

    _attend_units(MLA_HB, q_of, lambda i: kcat_ref[i], lambda i: v_ref[0, :, i * V_DIM:(i + 1) * V_DIM], store)


def _mla_attention(qn, qp, kv, kpe, *, tq):
    b, l, _ = qn.shape
    lk = kv.shape[1]
    wn = MLA_HB * NOPE_DIM
    nv0 = MLA_HEADS * NOPE_DIM // wn
    return pl.pallas_call(
        _mla_kernel,
        out_shape=jax.ShapeDtypeStruct((b, l, MLA_HEADS * V_DIM), BF16),
        grid=(b, MLA_HEADS // MLA_HB, l // tq),
        in_specs=[pl.BlockSpec((1, tq, wn), lambda bi, h, i: (bi, i, h)),
                  pl.BlockSpec((1, tq, MLA_HB * ROPE_DIM), lambda bi, h, i: (bi, i, h)),
                  pl.BlockSpec((1, lk, wn), lambda bi, h, i: (bi, 0, h)),
                  pl.BlockSpec((1, lk, wn), lambda bi, h, i: (bi, 0, nv0 + h)),
                  pl.BlockSpec((1, lk, LANES), lambda bi, h, i: (bi, 0, 0))],
        out_specs=pl.BlockSpec((1, tq, wn), lambda bi, h, i: (bi, i, h)),
        scratch_shapes=[pltpu.VMEM((MLA_HB, lk, 2 * NOPE_DIM), BF16)],
        compiler_params=_cparams("parallel", "parallel", "arbitrary"),
        name="mla_attention",
    )(qn, qp, kv, kv, kpe)


CONV_ROWS = 64
CONV_COLS = 128


def _conv_kernel(ug_ref, w_ref, b_ref, lg_ref, lb_ref, o_ref, pad_ref, y_ref):
    l, c = ug_ref.shape[1], ug_ref.shape[2]
    pad_ref[0:CONV_PAD, :] = jnp.zeros((CONV_PAD, c), F32)
    pad_ref[CONV_PAD + l:2 * CONV_PAD + l, :] = jnp.zeros((CONV_PAD, c), F32)
    pad_ref[CONV_PAD:CONV_PAD + l, :] = ug_ref[0].astype(F32)
    off = CONV_PAD - CONV_K // 2
    nwin = CONV_ROWS + 2 * CONV_PAD

    def body(r, carry):
        r0 = pl.multiple_of(r * CONV_ROWS, CONV_ROWS)
        for cb in range(c // CONV_COLS):
            cs = slice(cb * CONV_COLS, (cb + 1) * CONV_COLS)
            win = pad_ref[pl.ds(r0, nwin), cs]
            acc = jnp.zeros((CONV_ROWS, CONV_COLS), F32) + b_ref[:, cs]
            for s in range(SUBLANES):
                ws = win if s == 0 else pltpu.roll(win, nwin - s, 0)
                for k in range(CONV_K):
                    if (off + k) % SUBLANES == s:
                        j = (off + k) // SUBLANES * SUBLANES
                        acc = acc + w_ref[k:k + 1, cs] * ws[j:j + CONV_ROWS, :]
            y_ref[:, cs] = acc
        y = y_ref[...]
        mu = jnp.mean(y, axis=-1, keepdims=True)
        yc = y - mu
        var = jnp.mean(yc * yc, axis=-1, keepdims=True)
        z = yc * lax.rsqrt(var + EPS) * lg_ref[...] + lb_ref[...]
        o_ref[0, pl.ds(r0, CONV_ROWS), :] = (z * _sigmoid(z)).astype(BF16)
        return carry

    lax.fori_loop(0, l // CONV_ROWS, body, 0)


def _conv_branch(ug, w, b, lg, lb):
    bsz, l, c = ug.shape
    return pl.pallas_call(
        _conv_kernel,
        out_shape=jax.ShapeDtypeStruct((bsz, l, c), BF16),
        grid=(bsz,),
        in_specs=[pl.BlockSpec((1, l, c), lambda bi: (bi, 0, 0)),
                  _full_spec(w), _full_spec(b), _full_spec(lg), _full_spec(lb)],
        out_specs=pl.BlockSpec((1, l, c), lambda bi: (bi, 0, 0)),
        scratch_shapes=[pltpu.VMEM((l + 2 * CONV_PAD, c), F32), pltpu.VMEM((CONV_ROWS, c), F32)],
        compiler_params=_cparams("parallel"),
        name="conv_branch",
    )(ug, w, b, lg, lb)


def _outproj_kernel(*refs, n_in):
    x_ref = refs[0]
    a_refs = refs[1:1 + n_in]
    w_refs = refs[1 + n_in:1 + 2 * n_in]
    ga_ref, g2_ref, sh2_ref, sc2_ref, r_ref, xo_ref, h2_ref, lg_ref = refs[1 + 2 * n_in:]
    acc = _dot(a_refs[0][0], w_refs[0][...])
    for a_ref, w_ref in zip(a_refs[1:], w_refs[1:]):
        acc = acc + _dot(a_ref[0], w_ref[...])
    xn = x_ref[0] + ga_ref[0] * acc
    xo_ref[0] = xn
    h2 = _prenorm(xn, g2_ref[...], sh2_ref[0], sc2_ref[0]).astype(BF16)
    h2_ref[0] = h2
    lg_ref[0] = _dot(h2, r_ref[...])


def _outproj(x, acts, ws, ga, g2, sh2, sc2, router, *, tm):
    b, l, d = x.shape
    n_in = len(acts)
    row = lambda bi, i: (bi, i, 0)
    in_specs = [pl.BlockSpec((1, tm, d), row)]
    in_specs += [pl.BlockSpec((1, tm, a.shape[2]), row) for a in acts]
    in_specs += [_full_spec(w) for w in ws]
    in_specs += [_mod_spec(ga), _full_spec(g2), _mod_spec(sh2), _mod_spec(sc2), _full_spec(router)]
    return pl.pallas_call(
        functools.partial(_outproj_kernel, n_in=n_in),
        out_shape=(jax.ShapeDtypeStruct((b, l, d), F32),
                   jax.ShapeDtypeStruct((b, l, d), BF16),
                   jax.ShapeDtypeStruct((b, l, LANES), F32)),
        grid=(b, l // tm),
        in_specs=in_specs,
        out_specs=(pl.BlockSpec((1, tm, d), row), pl.BlockSpec((1, tm, d), row),
                   pl.BlockSpec((1, tm, LANES), row)),
        compiler_params=_cparams("parallel", "parallel"),
        name="outproj",
    )(x, *acts, *ws, ga, g2, sh2, sc2, router)


MOE_TM = 256


def _moe_kernel(xs_ref, gate_ref, wg_ref, wu_ref, wd_ref, o_ref):
    x = xs_ref[0]
    g = _dot(x, wg_ref[0])
    u = _dot(x, wu_ref[0])
    hid = (g * _sigmoid(g) * u).astype(BF16)
    o_ref[0] = (_dot(hid, wd_ref[0]) * gate_ref[0]).astype(BF16)


def _moe_ffn(xs, gates, wg, wu, wd):
    e, m, d = xs.shape
    f = wg.shape[2]
    return pl.pallas_call(
        _moe_kernel,
        out_shape=jax.ShapeDtypeStruct((e, m, d), BF16),
        grid=(e, m // MOE_TM),
        in_specs=[pl.BlockSpec((1, MOE_TM, d), lambda ei, i: (ei, i, 0)),
                  pl.BlockSpec((1, MOE_TM, 1), lambda ei, i: (ei, i, 0)),
                  pl.BlockSpec((1, d, f), lambda ei, i: (ei, 0, 0)),
                  pl.BlockSpec((1, d, f), lambda ei, i: (ei, 0, 0)),
                  pl.BlockSpec((1, f, d), lambda ei, i: (ei, 0, 0))],
        out_specs=pl.BlockSpec((1, MOE_TM, d), lambda ei, i: (ei, i, 0)),
        compiler_params=_cparams("parallel", "arbitrary"),
        name="moe_ffn",
    )(xs, gates, wg, wu, wd)


COMBINE_TT = 256
COMBINE_W = 256


def _combine_kernel(p0_ref, tok_ref, z_ref, x_ref, ga_ref, g_ref, o_ref, acc_ref, *, final_norm):
    bi = pl.program_id(0)
    ti = pl.program_id(1)
    p0 = p0_ref[bi, ti]
    p1 = p0_ref[bi, ti + 1]
    acc_ref[...] = jnp.zeros_like(acc_ref)
    tok_row = ti * COMBINE_TT + lax.broadcasted_iota(jnp.int32, (COMBINE_TT, COMBINE_W), 0)

    def body(w, carry):
        start = pl.multiple_of(w * COMBINE_W, COMBINE_W)
        toks = tok_ref[0, :, pl.ds(start, COMBINE_W)]
        onehot = jnp.where(toks == tok_row, 1.0, 0.0).astype(BF16)
        acc_ref[...] += _dot(onehot, z_ref[0, pl.ds(start, COMBINE_W), :])
        return carry

    lax.fori_loop(p0 // COMBINE_W, (p1 + COMBINE_W - 1) // COMBINE_W, body, 0)
    x = x_ref[0] + ga_ref[0] * acc_ref[...]
    if final_norm:
        ms = jnp.mean(x * x, axis=-1, keepdims=True)
        x = x * lax.rsqrt(ms + EPS) * g_ref[...]
    o_ref[0] = x


def _combine(x, ga, g, tok_sorted, z, p0, *, final_norm):
    b, n, d = x.shape
    p = z.shape[1]
    row = lambda bi, ti, *_: (bi, ti, 0)
    grid_spec = pltpu.PrefetchScalarGridSpec(
        num_scalar_prefetch=1,
        grid=(b, n // COMBINE_TT),
        in_specs=[pl.BlockSpec((1, 1, p), lambda bi, ti, *_: (bi, 0, 0)),
                  pl.BlockSpec((1, p, d), lambda bi, ti, *_: (bi, 0, 0)),
                  pl.BlockSpec((1, COMBINE_TT, d), row), _mod_spec(ga), _full_spec(g)],
        out_specs=pl.BlockSpec((1, COMBINE_TT, d), row),
        scratch_shapes=[pltpu.VMEM((COMBINE_TT, d), F32)])
    return pl.pallas_call(
        functools.partial(_combine_kernel, final_norm=final_norm),
        out_shape=jax.ShapeDtypeStruct((b, n, d), F32),
        grid_spec=grid_spec,
        compiler_params=_cparams("parallel", "arbitrary"),
        name="moe_combine",
    )(p0, tok_sorted, z, x, ga, g)


def _ec_moe(x, ga, g, h2, logits, wg, wu, wd, *, final_norm):
    b, n, d = h2.shape
    cap = EC_FACTOR * n // N_EXPERTS
    pairs = N_EXPERTS * cap
    aff = jax.nn.softmax(logits[..., :N_EXPERTS], axis=-1)
    gate, idx = lax.top_k(jnp.swapaxes(aff, 1, 2), cap)
    idx_e = jnp.swapaxes(idx, 0, 1)
    bidx = jnp.arange(b, dtype=idx.dtype)[None, :, None]
    xs = h2[bidx, idx_e].reshape(N_EXPERTS, b * cap, d)
    gates = jnp.swapaxes(gate, 0, 1).reshape(N_EXPERTS, b * cap, 1)
    ys = _moe_ffn(xs, gates, wg, wu, wd)
    slots = lax.broadcasted_iota(jnp.int32, (b, pairs), 1)
    tok_sorted, order = lax.sort((idx.reshape(b, pairs), slots), dimension=1, num_keys=1)
    flat = (order // cap) * (b * cap) + jnp.arange(b, dtype=jnp.int32)[:, None] * cap + order % cap
    z = ys.reshape(N_EXPERTS * b * cap, d)[flat]
    bounds = jnp.arange(n // COMBINE_TT + 1, dtype=jnp.int32) * COMBINE_TT
    p0 = jnp.sum(tok_sorted[:, None, :] < bounds[None, :, None], axis=-1).astype(jnp.int32)
    return _combine(x, ga, g, tok_sorted.reshape(b, 1, pairs), z, p0, final_norm=final_norm)


def _rope_quarter(y, cos, sin, first):
    rot = jnp.where(first, pltpu.roll(y, LANES - ROPE_DIM // 2, 1), pltpu.roll(y, ROPE_DIM // 2, 1))
    return y * cos + rot * sin


def _l1_dqkv_kernel(x_ref, g_ref, sh_ref, sc_ref, w_ref, qg_ref, kvg_ref, cos_ref, sin_ref, *out_refs,
                    with_q, rope):
    if with_q:
        cq_ref, ckv_ref, kpe_ref = out_refs
    else:
        ckv_ref, kpe_ref = out_refs
    h = _prenorm(x_ref[0], g_ref[...], sh_ref[0], sc_ref[0]).astype(BF16)
    if with_q:
        cq = _dot(h, w_ref[:, 0:Q_LORA])
        ms = jnp.mean(cq * cq, axis=-1, keepdims=True)
        cq_ref[0] = (cq * lax.rsqrt(ms + EPS) * qg_ref[...]).astype(BF16)
    z = _dot(h, w_ref[:, Q_LORA:Q_LORA + KV_LORA + LANES])
    ckv = z[:, 0:KV_LORA]
    ms = jnp.mean(ckv * ckv, axis=-1, keepdims=True)
    ckv_ref[0] = (ckv * lax.rsqrt(ms + EPS) * kvg_ref[...]).astype(BF16)
    pe = z[:, KV_LORA:KV_LORA + LANES]
    if rope:
        lane = lax.broadcasted_iota(jnp.int32, pe.shape, 1)
        pe = _rope_quarter(pe, cos_ref[...], sin_ref[...], (lane % ROPE_DIM) < ROPE_DIM // 2)
    kpe_ref[0] = (pe + pltpu.roll(pe, ROPE_DIM, 1)).astype(BF16)


def _l1_dqkv(x, g, sh, sc, w, qg, kvg, cos, sin, *, rope, with_q, tm):
    b, l, d = x.shape
    row = lambda bi, i: (bi, i, 0)
    args = [x, g, sh, sc, w, qg, kvg, cos, sin]
    in_specs = [pl.BlockSpec((1, tm, d), row), _full_spec(g), _mod_spec(sh), _mod_spec(sc), _full_spec(w),
                _full_spec(qg), _full_spec(kvg),
                pl.BlockSpec((tm, LANES), lambda bi, i: (i, 0)), pl.BlockSpec((tm, LANES), lambda bi, i: (i, 0))]
    out_shape, out_specs = [], []
    if with_q:
        out_shape.append(jax.ShapeDtypeStruct((b, l, Q_LORA), BF16))
        out_specs.append(pl.BlockSpec((1, tm, Q_LORA), row))
    out_shape += [jax.ShapeDtypeStruct((b, l, KV_LORA), BF16), jax.ShapeDtypeStruct((b, l, LANES), BF16)]
    out_specs += [pl.BlockSpec((1, tm, KV_LORA), row), pl.BlockSpec((1, tm, LANES), row)]

    return pl.pallas_call(
        functools.partial(_l1_dqkv_kernel, with_q=with_q, rope=rope),
        out_shape=tuple(out_shape),
        grid=(b, l // tm),
        in_specs=in_specs,
        out_specs=tuple(out_specs),
        compiler_params=_cparams("parallel", "parallel"),
        name="l1_dqkv",
    )(*args)


def _l1_q_kernel(cq_ref, w_ref, cos_ref, sin_ref, qn_ref, qp_ref):
    a = cq_ref[0]
    cw = 4 * LANES
    n_nope = MLA_HEADS * NOPE_DIM
    for c in range(n_nope // cw):
        qn_ref[0, :, c * cw:(c + 1) * cw] = (_dot(a, w_ref[:, c * cw:(c + 1) * cw]) * MLA_SCALE).astype(BF16)
    cos = cos_ref[...]
    sin = sin_ref[...]
    lane = lax.broadcasted_iota(jnp.int32, cos.shape, 1)
    first = (lane % ROPE_DIM) < ROPE_DIM // 2
    for c in range(MLA_HEADS * ROPE_DIM // cw):
        y = _dot(a, w_ref[:, n_nope + c * cw:n_nope + (c + 1) * cw])
        parts = [_rope_quarter(y[:, j * LANES:(j + 1) * LANES], cos, sin, first) * MLA_SCALE for j in range(4)]
        qp_ref[0, :, c * cw:(c + 1) * cw] = jnp.concatenate(parts, axis=1).astype(BF16)


def _l1_q(cq, w, cos, sin, *, tm):
    b, l, k = cq.shape
    row = lambda bi, i: (bi, i, 0)
    return pl.pallas_call(
        _l1_q_kernel,
        out_shape=(jax.ShapeDtypeStruct((b, l, MLA_HEADS * NOPE_DIM), BF16),
                   jax.ShapeDtypeStruct((b, l, MLA_HEADS * ROPE_DIM), BF16)),
        grid=(b, l // tm),
        in_specs=[pl.BlockSpec((1, tm, k), row), _full_spec(w),
                  pl.BlockSpec((tm, LANES), lambda bi, i: (i, 0)), pl.BlockSpec((tm, LANES), lambda bi, i: (i, 0))],
        out_specs=(pl.BlockSpec((1, tm, MLA_HEADS * NOPE_DIM), row),
                   pl.BlockSpec((1, tm, MLA_HEADS * ROPE_DIM), row)),
        compiler_params=_cparams("parallel", "parallel"),
        name="l1_q",
    )(cq, w, cos, sin)


def _mm_kernel(a_ref, w_ref, o_ref):
    a = a_ref[0]
    cw = 4 * LANES
    for c in range(w_ref.shape[1] // cw):
        o_ref[0, :, c * cw:(c + 1) * cw] = _dot(a, w_ref[:, c * cw:(c + 1) * cw]).astype(o_ref.dtype)


def _mm(a, w, *, tm):
    b, l, k = a.shape
    n = w.shape[1]
    row = lambda bi, i: (bi, i, 0)
    return pl.pallas_call(
        _mm_kernel,
        out_shape=jax.ShapeDtypeStruct((b, l, n), BF16),
        grid=(b, l // tm),
        in_specs=[pl.BlockSpec((1, tm, k), row), _full_spec(w)],
        out_specs=pl.BlockSpec((1, tm, n), row),
        compiler_params=_cparams("parallel", "parallel"),
        name="l1_kv_up",
    )(a, w)


def _rope_tables(n_tokens, d_rot):
    rows = n_tokens // GRID_W
    row = jnp.repeat(jnp.arange(rows, dtype=F32), GRID_W)
    col = jnp.tile(jnp.arange(GRID_W, dtype=F32), rows)
    n_axis = d_rot // 4
    inv_freq = ROPE_THETA ** (-jnp.arange(n_axis, dtype=F32) / n_axis)
    ang = jnp.concatenate([row[:, None] * inv_freq, col[:, None] * inv_freq], axis=-1)
    cos, sin = jnp.cos(ang), jnp.sin(ang)
    cos_t = jnp.concatenate([cos, cos], axis=-1)
    sin_t = jnp.concatenate([-sin, sin], axis=-1)
    reps = LANES // d_rot
    return jnp.tile(cos_t, (1, reps)), jnp.tile(sin_t, (1, reps))


def _mods(mod, lo, hi):
    return [mod[lo:hi, None, i * D_MODEL:(i + 1) * D_MODEL] for i in range(6)]


def _pad_router(router):
    return jnp.pad(router, ((0, 0), (0, LANES - router.shape[1]))).astype(BF16)


def kernel(x, c, ctx, c_ctx, l0_mod_w, l0_mod_b, l0_norm1_g, l0_w_in, l0_q_norm_g, l0_k_norm_g, l0_dw_w, l0_dw_b, l0_conv_ln_g, l0_conv_ln_b, l0_w_out, l0_norm2_g, l0_router, l0_w_gate, l0_w_up, l0_w_down, l1_mod_w, l1_mod_b, l1_norm1_g, l1_w_dqkv, l1_q_lora_norm_g, l1_w_uq, l1_kv_lora_norm_g, l1_w_ukv, l1_w_out, l1_norm2_g, l1_router, l1_w_gate, l1_w_up, l1_w_down, final_norm_g):
    bsz, seq, d = x.shape
    n_ctx = ctx.shape[1]
    row2 = lambda v: v.reshape(1, -1)

    cond = jnp.zeros((16, d), F32).at[:bsz].set(c).at[bsz].set(c_ctx)
    mod0 = _ada(cond, l0_mod_w, l0_mod_b)
    mod1 = _ada(cond, l1_mod_w, l1_mod_b)

    sh1, sc1, ga1, sh2, sc2, ga2 = _mods(mod0, 0, bsz)
    csh1, csc1, cga1, csh2, csc2, cga2 = _mods(mod0, bsz, bsz + 1)
    s_q, s_k, s_v, s_u = ATTN_WIDTH, ATTN_WIDTH + KV_WIDTH, ATTN_WIDTH + 2 * KV_WIDTH, ATTN_WIDTH + 2 * KV_WIDTH + CONV_WIDTH
    w_in = jnp.concatenate([l0_w_in[:, :s_q], l0_w_in[:, s_v:s_u], l0_w_in[:, s_u:],
                            l0_w_in[:, s_q:s_k], l0_w_in[:, s_k:s_v]], axis=1).astype(BF16)
    cos0, sin0 = _rope_tables(seq, HEAD_DIM)
    g1 = row2(l0_norm1_g)
    qg, kg = row2(l0_q_norm_g), row2(l0_k_norm_g)
    q_l, ug_l, kv_l = _l0_inproj(x, g1, sh1, sc1, w_in, qg, kg, cos0, sin0, rope=True, tm=512)
    q_c, ug_c, kv_c = _l0_inproj(ctx, g1, csh1, csc1, w_in, qg, kg, cos0, sin0, rope=False, tm=n_ctx)
    kv_all = jnp.concatenate([kv_c, kv_l], axis=1)
    a_l = _gqa_attention(q_l, kv_all, tq=512)
    a_c = _gqa_attention(q_c, kv_c, tq=n_ctx)
    dw_w = l0_dw_w.reshape(CONV_K, CONV_WIDTH)
    dw_b, ln_g, ln_b = row2(l0_dw_b), row2(l0_conv_ln_g), row2(l0_conv_ln_b)
    cb_l = _conv_branch(ug_l, dw_w, dw_b, ln_g, ln_b)
    cb_c = _conv_branch(ug_c, dw_w, dw_b, ln_g, ln_b)
    w_out0 = l0_w_out.astype(BF16)
    wo_a, wo_c = w_out0[:ATTN_WIDTH], w_out0[ATTN_WIDTH:]
    g2 = row2(l0_norm2_g)
    router0 = _pad_router(l0_router)
    x1, h2_l, lg_l = _outproj(x, [a_l, cb_l], [wo_a, wo_c], ga1, g2, sh2, sc2, router0, tm=256)
    xc1, h2_c, lg_c = _outproj(ctx, [a_c, cb_c], [wo_a, wo_c], cga1, g2, csh2, csc2, router0, tm=n_ctx)
    wg0, wu0, wd0 = l0_w_gate.astype(BF16), l0_w_up.astype(BF16), l0_w_down.astype(BF16)
    x2 = _ec_moe(x1, ga2, g2, h2_l, lg_l, wg0, wu0, wd0, final_norm=False)
    xc2 = _ec_moe(xc1, cga2, g2, h2_c, lg_c, wg0, wu0, wd0, final_norm=False)

    sh1, sc1, ga1, sh2, sc2, ga2_1 = _mods(mod1, 0, bsz)
    csh1, csc1 = _mods(mod1, bsz, bsz + 1)[:2]
    w_dqkv = jnp.pad(l1_w_dqkv, ((0, 0), (0, LANES - ROPE_DIM))).astype(BF16)
    cos1, sin1 = _rope_tables(seq, ROPE_DIM)
    g1 = row2(l1_norm1_g)
    qlg, kvlg = row2(l1_q_lora_norm_g), row2(l1_kv_lora_norm_g)
    cq, ckv_l, kpe_l = _l1_dqkv(x2, g1, sh1, sc1, w_dqkv, qlg, kvlg, cos1, sin1, rope=True, with_q=True, tm=512)
    ckv_c, kpe_c = _l1_dqkv(xc2, g1, csh1, csc1, w_dqkv, qlg, kvlg, cos1, sin1, rope=False, with_q=False, tm=n_ctx)
    w_uq = l1_w_uq.reshape(Q_LORA, MLA_HEADS, QK_DIM)
    w_uq = jnp.concatenate([w_uq[:, :, :NOPE_DIM].reshape(Q_LORA, -1),
                            w_uq[:, :, NOPE_DIM:].reshape(Q_LORA, -1)], axis=1).astype(BF16)
    w_ukv = l1_w_ukv.reshape(KV_LORA, MLA_HEADS, NOPE_DIM + V_DIM)
    w_ukv = jnp.concatenate([w_ukv[:, :, :NOPE_DIM].reshape(KV_LORA, -1),
                             w_ukv[:, :, NOPE_DIM:].reshape(KV_LORA, -1)], axis=1).astype(BF16)
    qn, qp = _l1_q(cq, w_uq, cos1, sin1, tm=512)
    kv1 = _mm(jnp.concatenate([ckv_c, ckv_l], axis=1), w_ukv, tm=256)
    kpe = jnp.concatenate([kpe_c, kpe_l], axis=1)
    a1 = _mla_attention(qn, qp, kv1, kpe, tq=256)
    x3, h2, lg = _outproj(x2, [a1], [l1_w_out.astype(BF16)], ga1, row2(l1_norm2_g), sh2, sc2,
                          _pad_router(l1_router), tm=256)
    return _ec_moe(x3, ga2_1, row2(final_norm_g), h2, lg,
                   l1_w_gate.astype(BF16), l1_w_up.astype(BF16), l1_w_down.astype(BF16), final_norm=True)
```

```python
import functools

import jax
import jax.numpy as jnp
from jax import lax
from jax.experimental import pallas as pl
from jax.experimental.pallas import tpu as pltpu

F32 = jnp.float32
BF16 = jnp.bfloat16

D_MODEL = 2048
GRID_W = 64
EPS = 1e-6
ROPE_THETA = 10000.0
HEAD_DIM = 128
N_Q_HEADS = 8
N_KV_HEADS = 2
ATTN_WIDTH = N_Q_HEADS * HEAD_DIM
KV_WIDTH = N_KV_HEADS * HEAD_DIM
CONV_WIDTH = D_MODEL - ATTN_WIDTH
CONV_K = 31
LOG2E = 1.4426950408889634
GQA_SCALE = HEAD_DIM ** -0.5
MLA_HEADS = 16
Q_LORA = 1536
KV_LORA = 512
NOPE_DIM = 128
ROPE_DIM = 64
V_DIM = 128
QK_DIM = NOPE_DIM + ROPE_DIM
MLA_SCALE = QK_DIM ** -0.5
N_EXPERTS = 16
D_EXPERT = 1024
EC_FACTOR = 2

LANES = 128
SUBLANES = 8
VMEM_LIMIT = 56 * 1024 * 1024
CONV_PAD = 16


def _cparams(*sem):
    return pltpu.CompilerParams(dimension_semantics=sem, vmem_limit_bytes=VMEM_LIMIT)


def _dot(a, b):
    return jnp.dot(a, b, preferred_element_type=F32)


def _sigmoid(x):
    return 1.0 / (1.0 + jnp.exp(-x))


def _prenorm(x, g, sh, sc):
    ms = jnp.mean(x * x, axis=-1, keepdims=True)
    y = x * lax.rsqrt(ms + EPS) * g
    return y * (1.0 + sc) + sh


def _mod_spec(arr):
    if arr.shape[0] > 1:
        return pl.BlockSpec((1, 1, arr.shape[2]), lambda b, *_: (b, 0, 0))
    return pl.BlockSpec((1, 1, arr.shape[2]), lambda b, *_: (0, 0, 0))


def _full_spec(arr):
    nd = arr.ndim
    return pl.BlockSpec(arr.shape, lambda *_: (0,) * nd)


def _ada_kernel(c_ref, w_ref, b_ref, o_ref):
    c = c_ref[...]
    s = c * _sigmoid(c)
    s_hi = s.astype(BF16)
    s_lo = (s - s_hi.astype(F32)).astype(BF16)
    w = w_ref[...]
    w_hi = w.astype(BF16)
    w_lo = (w - w_hi.astype(F32)).astype(BF16)
    o_ref[...] = _dot(s_hi, w_hi) + _dot(s_lo, w_hi) + _dot(s_hi, w_lo) + b_ref[...]


def _ada(cond, w, b):
    m, d = cond.shape
    n = w.shape[1]
    tn = 512
    return pl.pallas_call(
        _ada_kernel,
        out_shape=jax.ShapeDtypeStruct((m, n), F32),
        grid=(n // tn,),
        in_specs=[pl.BlockSpec((m, d), lambda j: (0, 0)),
                  pl.BlockSpec((d, tn), lambda j: (0, j)),
                  pl.BlockSpec((1, tn), lambda j: (0, j))],
        out_specs=pl.BlockSpec((m, tn), lambda j: (0, j)),
        compiler_params=_cparams("parallel"),
        name="ada_params",
    )(cond, w, b.reshape(1, n))


def _rope_half(y, cos, sin):
    return y * cos + pltpu.roll(y, HEAD_DIM // 2, 1) * sin


def _l0_inproj_kernel(x_ref, g_ref, sh_ref, sc_ref, w_ref, qg_ref, kg_ref, cos_ref, sin_ref,
                      q_ref, ug_ref, kv_ref, *, rope):
    h = _prenorm(x_ref[0], g_ref[...], sh_ref[0], sc_ref[0]).astype(BF16)
    cos = cos_ref[...]
    sin = sin_ref[...]

    def head(y, gain, scale):
        ms = jnp.mean(y * y, axis=-1, keepdims=True)
        y = y * lax.rsqrt(ms + EPS) * gain
        if rope:
            y = _rope_half(y, cos, sin)
        return y * scale if scale != 1.0 else y

    cw = 4 * HEAD_DIM
    for c in range(ATTN_WIDTH // cw):
        y = _dot(h, w_ref[:, c * cw:(c + 1) * cw])
        parts = [head(y[:, j * HEAD_DIM:(j + 1) * HEAD_DIM], qg_ref[...], GQA_SCALE * LOG2E) for j in range(4)]
        q_ref[0, :, c * cw:(c + 1) * cw] = jnp.concatenate(parts, axis=1).astype(BF16)
    u0 = ATTN_WIDTH
    g0 = ATTN_WIDTH + CONV_WIDTH
    for c in range(CONV_WIDTH // cw):
        u = _dot(h, w_ref[:, u0 + c * cw:u0 + (c + 1) * cw])
        gt = _dot(h, w_ref[:, g0 + c * cw:g0 + (c + 1) * cw])
        ug_ref[0, :, c * cw:(c + 1) * cw] = (u * _sigmoid(gt)).astype(BF16)
    k0 = ATTN_WIDTH + 2 * CONV_WIDTH
    y = _dot(h, w_ref[:, k0:k0 + 2 * KV_WIDTH])
    parts = [head(y[:, j * HEAD_DIM:(j + 1) * HEAD_DIM], kg_ref[...], 1.0) for j in range(N_KV_HEADS)]
    parts.append(y[:, KV_WIDTH:])
    kv_ref[0] = jnp.concatenate(parts, axis=1).astype(BF16)


def _l0_inproj(x, g, sh, sc, w, qg, kg, cos, sin, *, rope, tm):
    b, l, d = x.shape
    n = w.shape[1]
    row = lambda bi, i: (bi, i, 0)
    return pl.pallas_call(
        functools.partial(_l0_inproj_kernel, rope=rope),
        out_shape=(jax.ShapeDtypeStruct((b, l, ATTN_WIDTH), BF16),
                   jax.ShapeDtypeStruct((b, l, CONV_WIDTH), BF16),
                   jax.ShapeDtypeStruct((b, l, 2 * KV_WIDTH), BF16)),
        grid=(b, l // tm),
        in_specs=[pl.BlockSpec((1, tm, d), row), _full_spec(g), _mod_spec(sh), _mod_spec(sc),
                  _full_spec(w), _full_spec(qg), _full_spec(kg),
                  pl.BlockSpec((tm, LANES), lambda bi, i: (i, 0)),
                  pl.BlockSpec((tm, LANES), lambda bi, i: (i, 0))],
        out_specs=(pl.BlockSpec((1, tm, ATTN_WIDTH), row),
                   pl.BlockSpec((1, tm, CONV_WIDTH), row),
                   pl.BlockSpec((1, tm, 2 * KV_WIDTH), row)),
        compiler_params=_cparams("parallel", "parallel"),
        name="l0_inproj",
    )(x, g, sh, sc, w, qg, kg, cos, sin)


ATT_ROWS = 256


def _scores(q, k):
    return lax.dot_general(q, k, (((1,), (1,)), ((), ())), preferred_element_type=F32)


def _softmax_pv(s, v):
    m = jnp.max(s, axis=-1, keepdims=True)
    p = jnp.exp2(s - m)
    l = jnp.sum(p, axis=-1, keepdims=True)
    return _dot(p.astype(BF16), v) / l


def _attend_units(n_units, q_of, k_of, v_of, store):
    s = _scores(q_of(0), k_of(0))
    for n in range(n_units):
        s_cur = s
        if n + 1 < n_units:
            s = _scores(q_of(n + 1), k_of(n + 1))
        store(n, _softmax_pv(s_cur, v_of(n)).astype(BF16))


def _gqa_kernel(q_ref, k_ref, v_ref, o_ref):
    k = k_ref[0]
    v = v_ref[0]
    group = N_Q_HEADS // N_KV_HEADS
    n_units = q_ref.shape[1] // ATT_ROWS * group

    def where(n):
        r, j = divmod(n, group)
        return slice(r * ATT_ROWS, (r + 1) * ATT_ROWS), slice(j * HEAD_DIM, (j + 1) * HEAD_DIM)

    def store(n, o):
        rows, cols = where(n)
        o_ref[0, rows, cols] = o

    def q_of(n):
        rows, cols = where(n)
        return q_ref[0, rows, cols]

    _attend_units(n_units, q_of, lambda n: k, lambda n: v, store)


def _gqa_attention(q, kv, *, tq):
    b, l, _ = q.shape
    lk = kv.shape[1]
    gw = ATTN_WIDTH // N_KV_HEADS
    return pl.pallas_call(
        _gqa_kernel,
        out_shape=jax.ShapeDtypeStruct((b, l, ATTN_WIDTH), BF16),
        grid=(b, N_KV_HEADS, l // tq),
        in_specs=[pl.BlockSpec((1, tq, gw), lambda bi, h, i: (bi, i, h)),
                  pl.BlockSpec((1, lk, HEAD_DIM), lambda bi, h, i: (bi, 0, h)),
                  pl.BlockSpec((1, lk, HEAD_DIM), lambda bi, h, i: (bi, 0, N_KV_HEADS + h))],
        out_specs=pl.BlockSpec((1, tq, gw), lambda bi, h, i: (bi, i, h)),
        compiler_params=_cparams("parallel", "parallel", "arbitrary"),
        name="gqa_attention",
    )(q, kv, kv)


MLA_HB = 4


def _mla_kernel(qn_ref, qp_ref, k_ref, v_ref, kpe_ref, o_ref, kcat_ref):
    @pl.when(pl.program_id(2) == 0)
    def _():
        for i in range(MLA_HB):
            kcat_ref[i, :, 0:NOPE_DIM] = k_ref[0, :, i * NOPE_DIM:(i + 1) * NOPE_DIM]
            kcat_ref[i, :, NOPE_DIM:2 * NOPE_DIM] = kpe_ref[0]

    lane = lax.broadcasted_iota(jnp.int32, (qp_ref.shape[1], LANES), 1)
    per_block = LANES // ROPE_DIM

    def q_of(i):
        blk, pos = divmod(i, per_block)
        qp = qp_ref[0, :, blk * LANES:(blk + 1) * LANES]
        mine = (lane >= pos * ROPE_DIM) & (lane < (pos + 1) * ROPE_DIM)
        return jnp.concatenate(
            [qn_ref[0, :, i * NOPE_DIM:(i + 1) * NOPE_DIM], jnp.where(mine, qp, jnp.zeros_like(qp))], axis=1)

    def store(i, o):
        o_ref[0, :, i * V_DIM:(i + 1) * V_DIM] = o

    _attend_units(MLA_HB, q_of, lambda i: kcat_ref[i], lambda i: v_ref[0, :, i * V_DIM:(i + 1) * V_DIM], store)


def _mla_attention(qn, qp, kv, kpe, *, tq):
    b, l, _ = qn.shape
    lk = kv.shape[1]
    wn = MLA_HB * NOPE_DIM
    nv0 = MLA_HEADS * NOPE_DIM // wn
    return pl.pallas_call(
        _mla_kernel,
        out_shape=jax.ShapeDtypeStruct((b, l, MLA_HEADS * V_DIM), BF16),
        grid=(b, MLA_HEADS // MLA_HB, l // tq),
        in_specs=[pl.BlockSpec((1, tq, wn), lambda bi, h, i: (bi, i, h)),
                  pl.BlockSpec((1, tq, MLA_HB * ROPE_DIM), lambda bi, h, i: (bi, i, h)),
                  pl.BlockSpec((1, lk, wn), lambda bi, h, i: (bi, 0, h)),
                  pl.BlockSpec((1, lk, wn), lambda bi, h, i: (bi, 0, nv0 + h)),
                  pl.BlockSpec((1, lk, LANES), lambda bi, h, i: (bi, 0, 0))],
        out_specs=pl.BlockSpec((1, tq, wn), lambda bi, h, i: (bi, i, h)),
        scratch_shapes=[pltpu.VMEM((MLA_HB, lk, 2 * NOPE_DIM), BF16)],
        compiler_params=_cparams("parallel", "parallel", "arbitrary"),
        name="mla_attention",
    )(qn, qp, kv, kv, kpe)


CONV_ROWS = 64
CONV_COLS = 128


def _conv_kernel(ug_ref, w_ref, b_ref, lg_ref, lb_ref, o_ref, pad_ref, y_ref):
    l, c = ug_ref.shape[1], ug_ref.shape[2]
    pad_ref[0:CONV_PAD, :] = jnp.zeros((CONV_PAD, c), F32)
    pad_ref[CONV_PAD + l:2 * CONV_PAD + l, :] = jnp.zeros((CONV_PAD, c), F32)
    pad_ref[CONV_PAD:CONV_PAD + l, :] = ug_ref[0].astype(F32)
    off = CONV_PAD - CONV_K // 2
    nwin = CONV_ROWS + 2 * CONV_PAD

    def body(r, carry):
        r0 = pl.multiple_of(r * CONV_ROWS, CONV_ROWS)
        for cb in range(c // CONV_COLS):
            cs = slice(cb * CONV_COLS, (cb + 1) * CONV_COLS)
            win = pad_ref[pl.ds(r0, nwin), cs]
            acc = jnp.zeros((CONV_ROWS, CONV_COLS), F32) + b_ref[:, cs]
            for s in range(SUBLANES):
                ws = win if s == 0 else pltpu.roll(win, nwin - s, 0)
                for k in range(CONV_K):
                    if (off + k) % SUBLANES == s:
                        j = (off + k) // SUBLANES * SUBLANES
                        acc = acc + w_ref[k:k + 1, cs] * ws[j:j + CONV_ROWS, :]
            y_ref[:, cs] = acc
        y = y_ref[...]
        mu = jnp.mean(y, axis=-1, keepdims=True)
        yc = y - mu
        var = jnp.mean(yc * yc, axis=-1, keepdims=True)
        z = yc * lax.rsqrt(var + EPS) * lg_ref[...] + lb_ref[...]
        o_ref[0, pl.ds(r0, CONV_ROWS), :] = (z * _sigmoid(z)).astype(BF16)
        return carry

    lax.fori_loop(0, l // CONV_ROWS, body, 0)


def _conv_branch(ug, w, b, lg, lb):
    bsz, l, c = ug.shape
    return pl.pallas_call(
        _conv_kernel,
        out_shape=jax.ShapeDtypeStruct((bsz, l, c), BF16),
        grid=(bsz,),
        in_specs=[pl.BlockSpec((1, l, c), lambda bi: (bi, 0, 0)),
                  _full_spec(w), _full_spec(b), _full_spec(lg), _full_spec(lb)],
        out_specs=pl.BlockSpec((1, l, c), lambda bi: (bi, 0, 0)),
        scratch_shapes=[pltpu.VMEM((l + 2 * CONV_PAD, c), F32), pltpu.VMEM((CONV_ROWS, c), F32)],
        compiler_params=_cparams("parallel"),
        name="conv_branch",
    )(ug, w, b, lg, lb)


def _outproj_kernel(*refs, n_in):
    x_ref = refs[0]
    a_refs = refs[1:1 + n_in]
    w_refs = refs[1 + n_in:1 + 2 * n_in]
    ga_ref, g2_ref, sh2_ref, sc2_ref, r_ref, xo_ref, h2_ref, lg_ref = refs[1 + 2 * n_in:]
    acc = _dot(a_refs[0][0], w_refs[0][...])
    for a_ref, w_ref in zip(a_refs[1:], w_refs[1:]):
        acc = acc + _dot(a_ref[0], w_ref[...])
    xn = x_ref[0] + ga_ref[0] * acc
    xo_ref[0] = xn
    h2 = _prenorm(xn, g2_ref[...], sh2_ref[0], sc2_ref[0]).astype(BF16)
    h2_ref[0] = h2
    lg_ref[0] = _dot(h2, r_ref[...])


def _outproj(x, acts, ws, ga, g2, sh2, sc2, router, *, tm):
    b, l, d = x.shape
    n_in = len(acts)
    row = lambda bi, i: (bi, i, 0)
    in_specs = [pl.BlockSpec((1, tm, d), row)]
    in_specs += [pl.BlockSpec((1, tm, a.shape[2]), row) for a in acts]
    in_specs += [_full_spec(w) for w in ws]
    in_specs += [_mod_spec(ga), _full_spec(g2), _mod_spec(sh2), _mod_spec(sc2), _full_spec(router)]
    return pl.pallas_call(
        functools.partial(_outproj_kernel, n_in=n_in),
        out_shape=(jax.ShapeDtypeStruct((b, l, d), F32),
                   jax.ShapeDtypeStruct((b, l, d), BF16),
                   jax.ShapeDtypeStruct((b, l, LANES), F32)),
        grid=(b, l // tm),
        in_specs=in_specs,
        out_specs=(pl.BlockSpec((1, tm, d), row), pl.BlockSpec((1, tm, d), row),
                   pl.BlockSpec((1, tm, LANES), row)),
        compiler_params=_cparams("parallel", "parallel"),
        name="outproj",
    )(x, *acts, *ws, ga, g2, sh2, sc2, router)


MOE_TM = 256
MOE_CHUNKS = 8


def _moe_kernel(xs_ref, gate_ref, wg_ref, wu_ref, wd_ref, o_ref, wgb_ref, wub_ref, wdb_ref):
    p = pl.program_id(0)
    i = pl.program_id(1)
    n_experts = pl.num_programs(0) - 1
    cg = wg_ref.shape[1]
    cd = wd_ref.shape[1]

    @pl.when((p < n_experts) & (i < MOE_CHUNKS))
    def _():
        slot = p % 2
        rg = pl.multiple_of(i * cg, cg)
        rd = pl.multiple_of(i * cd, cd)
        wgb_ref[slot, pl.ds(rg, cg), :] = wg_ref[0].astype(BF16)
        wub_ref[slot, pl.ds(rg, cg), :] = wu_ref[0].astype(BF16)
        wdb_ref[slot, pl.ds(rd, cd), :] = wd_ref[0].astype(BF16)

    @pl.when(p == 0)
    def _():
        o_ref[...] = jnp.zeros_like(o_ref)

    @pl.when(p > 0)
    def _():
        slot = (p + 1) % 2
        x = xs_ref[0]
        g = _dot(x, wgb_ref[slot])
        u = _dot(x, wub_ref[slot])
        hid = (g * _sigmoid(g) * u).astype(BF16)
        o_ref[0] = (_dot(hid, wdb_ref[slot]) * gate_ref[0]).astype(BF16)


def _moe_ffn(xs, gates, wg, wu, wd):
    e, m, d = xs.shape
    f = wg.shape[2]
    last_e, last_c = e - 1, MOE_CHUNKS - 1
    rows = lambda p, i: (jnp.maximum(p - 1, 0), i, 0)
    chunk = lambda p, i: (jnp.minimum(p, last_e), jnp.minimum(i, last_c), 0)
    return pl.pallas_call(
        _moe_kernel,
        out_shape=jax.ShapeDtypeStruct((e + 1, m, d), BF16),
        grid=(e + 1, m // MOE_TM),
        in_specs=[pl.BlockSpec((1, MOE_TM, d), rows),
                  pl.BlockSpec((1, MOE_TM, 1), rows),
                  pl.BlockSpec((1, d // MOE_CHUNKS, f), chunk),
                  pl.BlockSpec((1, d // MOE_CHUNKS, f), chunk),
                  pl.BlockSpec((1, f // MOE_CHUNKS, d), chunk)],
        out_specs=pl.BlockSpec((1, MOE_TM, d), lambda p, i: (p, i, 0)),
        scratch_shapes=[pltpu.VMEM((2, d, f), BF16), pltpu.VMEM((2, d, f), BF16), pltpu.VMEM((2, f, d), BF16)],
        compiler_params=_cparams("arbitrary", "arbitrary"),
        name="moe_ffn",
    )(xs, gates, wg, wu, wd)


COMBINE_TT = 256
COMBINE_W = 256
COMBINE_KW = 1024


def _combine_kernel(p0_ref, tok_ref, z_ref, x_ref, ga_ref, g_ref, o_ref, acc_ref, *, final_norm):
    bi = pl.program_id(0)
    ti = pl.program_id(1)
    p0 = p0_ref[bi, ti]
    p1 = p0_ref[bi, ti + 1]
    n_pairs = z_ref.shape[1]
    kw = min(COMBINE_KW, n_pairs)
    tok0 = ti * COMBINE_TT

    def window(start, width):
        tok_row = tok0 + lax.broadcasted_iota(jnp.int32, (COMBINE_TT, width), 0)
        toks = tok_ref[0, :, pl.ds(start, width)]
        onehot = jnp.where(toks == tok_row, 1.0, 0.0).astype(BF16)
        return _dot(onehot, z_ref[0, pl.ds(start, width), :])

    first = jnp.minimum(p0 // COMBINE_W * COMBINE_W, n_pairs - kw)
    fits = p1 <= first + kw

    @pl.when(fits)
    def _():
        acc_ref[...] = window(pl.multiple_of(first, COMBINE_W), kw)

    @pl.when(jnp.logical_not(fits))
    def _():
        acc_ref[...] = jnp.zeros_like(acc_ref)

        def body(w, carry):
            acc_ref[...] += window(pl.multiple_of(w * COMBINE_W, COMBINE_W), COMBINE_W)
            return carry

        lax.fori_loop(p0 // COMBINE_W, (p1 + COMBINE_W - 1) // COMBINE_W, body, 0)

    x = x_ref[0] + ga_ref[0] * acc_ref[...]
    if final_norm:
        ms = jnp.mean(x * x, axis=-1, keepdims=True)
        x = x * lax.rsqrt(ms + EPS) * g_ref[...]
    o_ref[0] = x


def _combine(x, ga, g, tok_sorted, z, p0, *, final_norm):
    b, n, d = x.shape
    p = z.shape[1]
    row = lambda bi, ti, *_: (bi, ti, 0)
    grid_spec = pltpu.PrefetchScalarGridSpec(
        num_scalar_prefetch=1,
        grid=(b, n // COMBINE_TT),
        in_specs=[pl.BlockSpec((1, 1, p), lambda bi, ti, *_: (bi, 0, 0)),
                  pl.BlockSpec((1, p, d), lambda bi, ti, *_: (bi, 0, 0)),
                  pl.BlockSpec((1, COMBINE_TT, d), row), _mod_spec(ga), _full_spec(g)],
        out_specs=pl.BlockSpec((1, COMBINE_TT, d), row),
        scratch_shapes=[pltpu.VMEM((COMBINE_TT, d), F32)])
    return pl.pallas_call(
        functools.partial(_combine_kernel, final_norm=final_norm),
        out_shape=jax.ShapeDtypeStruct((b, n, d), F32),
        grid_spec=grid_spec,
        compiler_params=_cparams("parallel", "arbitrary"),
        name="moe_combine",
    )(p0, tok_sorted, z, x, ga, g)


def _route(h2, logits):
    b, n, d = h2.shape
    cap = EC_FACTOR * n // N_EXPERTS
    aff = jax.nn.softmax(logits[..., :N_EXPERTS], axis=-1)
    gate, idx = lax.top_k(jnp.swapaxes(aff, 1, 2), cap)
    idx_e = jnp.swapaxes(idx, 0, 1)
    bidx = jnp.arange(b, dtype=idx.dtype)[None, :, None]
    xs = h2[bidx, idx_e].reshape(N_EXPERTS, b * cap, d)
    gates = jnp.swapaxes(gate, 0, 1).reshape(N_EXPERTS, b * cap, 1)
    return xs, gates, idx


def _combine_routed(x, ga, g, idx, ys, row0, *, final_norm):
    b, n, d = x.shape
    cap = idx.shape[2]
    pairs = N_EXPERTS * cap
    m = ys.shape[1]
    slots = lax.broadcasted_iota(jnp.int32, (b, pairs), 1)
    tok_sorted, order = lax.sort((idx.reshape(b, pairs), slots), dimension=1, num_keys=1)
    flat = (order // cap + 1) * m + row0 + jnp.arange(b, dtype=jnp.int32)[:, None] * cap + order % cap
    z = ys.reshape((N_EXPERTS + 1) * m, d)[flat]
    bounds = jnp.arange(n // COMBINE_TT + 1, dtype=jnp.int32) * COMBINE_TT
    p0 = jnp.sum(tok_sorted[:, None, :] < bounds[None, :, None], axis=-1).astype(jnp.int32)
    return _combine(x, ga, g, tok_sorted.reshape(b, 1, pairs), z, p0, final_norm=final_norm)


def _rope_quarter(y, cos, sin, first):
    rot = jnp.where(first, pltpu.roll(y, LANES - ROPE_DIM // 2, 1), pltpu.roll(y, ROPE_DIM // 2, 1))
    return y * cos + rot * sin


def _l1_dqkv_kernel(x_ref, g_ref, sh_ref, sc_ref, w_ref, qg_ref, kvg_ref, cos_ref, sin_ref, *out_refs,
                    with_q, rope):
    if with_q:
        cq_ref, ckv_ref, kpe_ref = out_refs
    else:
        ckv_ref, kpe_ref = out_refs
    h = _prenorm(x_ref[0], g_ref[...], sh_ref[0], sc_ref[0]).astype(BF16)
    if with_q:
        cq = _dot(h, w_ref[:, 0:Q_LORA])
        ms = jnp.mean(cq * cq, axis=-1, keepdims=True)
        cq_ref[0] = (cq * lax.rsqrt(ms + EPS) * qg_ref[...]).astype(BF16)
    z = _dot(h, w_ref[:, Q_LORA:Q_LORA + KV_LORA + LANES])
    ckv = z[:, 0:KV_LORA]
    ms = jnp.mean(ckv * ckv, axis=-1, keepdims=True)
    ckv_ref[0] = (ckv * lax.rsqrt(ms + EPS) * kvg_ref[...]).astype(BF16)
    pe = z[:, KV_LORA:KV_LORA + LANES]
    if rope:
        lane = lax.broadcasted_iota(jnp.int32, pe.shape, 1)
        pe = _rope_quarter(pe, cos_ref[...], sin_ref[...], (lane % ROPE_DIM) < ROPE_DIM // 2)
    kpe_ref[0] = (pe + pltpu.roll(pe, ROPE_DIM, 1)).astype(BF16)


def _l1_dqkv(x, g, sh, sc, w, qg, kvg, cos, sin, *, rope, with_q, tm):
    b, l, d = x.shape
    row = lambda bi, i: (bi, i, 0)
    args = [x, g, sh, sc, w, qg, kvg, cos, sin]
    in_specs = [pl.BlockSpec((1, tm, d), row), _full_spec(g), _mod_spec(sh), _mod_spec(sc), _full_spec(w),
                _full_spec(qg), _full_spec(kvg),
                pl.BlockSpec((tm, LANES), lambda bi, i: (i, 0)), pl.BlockSpec((tm, LANES), lambda bi, i: (i, 0))]
    out_shape, out_specs = [], []
    if with_q:
        out_shape.append(jax.ShapeDtypeStruct((b, l, Q_LORA), BF16))
        out_specs.append(pl.BlockSpec((1, tm, Q_LORA), row))
    out_shape += [jax.ShapeDtypeStruct((b, l, KV_LORA), BF16), jax.ShapeDtypeStruct((b, l, LANES), BF16)]
    out_specs += [pl.BlockSpec((1, tm, KV_LORA), row), pl.BlockSpec((1, tm, LANES), row)]

    return pl.pallas_call(
        functools.partial(_l1_dqkv_kernel, with_q=with_q, rope=rope),
        out_shape=tuple(out_shape),
        grid=(b, l // tm),
        in_specs=in_specs,
        out_specs=tuple(out_specs),
        compiler_params=_cparams("parallel", "parallel"),
        name="l1_dqkv",
    )(*args)


def _l1_q_kernel(cq_ref, w_ref, cos_ref, sin_ref, qn_ref, qp_ref):
    a = cq_ref[0]
    cw = 4 * LANES
    n_nope = MLA_HEADS * NOPE_DIM
    qscale = MLA_SCALE * LOG2E
    for c in range(n_nope // cw):
        qn_ref[0, :, c * cw:(c + 1) * cw] = (_dot(a, w_ref[:, c * cw:(c + 1) * cw]) * qscale).astype(BF16)
    cos = cos_ref[...]
    sin = sin_ref[...]
    lane = lax.broadcasted_iota(jnp.int32, cos.shape, 1)
    first = (lane % ROPE_DIM) < ROPE_DIM // 2
    for c in range(MLA_HEADS * ROPE_DIM // cw):
        y = _dot(a, w_ref[:, n_nope + c * cw:n_nope + (c + 1) * cw])
        parts = [_rope_quarter(y[:, j * LANES:(j + 1) * LANES], cos, sin, first) * qscale for j in range(4)]
        qp_ref[0, :, c * cw:(c + 1) * cw] = jnp.concatenate(parts, axis=1).astype(BF16)


def _l1_q(cq, w, cos, sin, *, tm):
    b, l, k = cq.shape
    row = lambda bi, i: (bi, i, 0)
    return pl.pallas_call(
        _l1_q_kernel,
        out_shape=(jax.ShapeDtypeStruct((b, l, MLA_HEADS * NOPE_DIM), BF16),
                   jax.ShapeDtypeStruct((b, l, MLA_HEADS * ROPE_DIM), BF16)),
        grid=(b, l // tm),
        in_specs=[pl.BlockSpec((1, tm, k), row), _full_spec(w),
                  pl.BlockSpec((tm, LANES), lambda bi, i: (i, 0)), pl.BlockSpec((tm, LANES), lambda bi, i: (i, 0))],
        out_specs=(pl.BlockSpec((1, tm, MLA_HEADS * NOPE_DIM), row),
                   pl.BlockSpec((1, tm, MLA_HEADS * ROPE_DIM), row)),
        compiler_params=_cparams("parallel", "parallel"),
        name="l1_q",
    )(cq, w, cos, sin)


def _mm_kernel(a_ref, w_ref, o_ref):
    a = a_ref[0]
    cw = 4 * LANES
    for c in range(w_ref.shape[1] // cw):
        o_ref[0, :, c * cw:(c + 1) * cw] = _dot(a, w_ref[:, c * cw:(c + 1) * cw]).astype(o_ref.dtype)


def _mm(a, w, *, tm):
    b, l, k = a.shape
    n = w.shape[1]
    row = lambda bi, i: (bi, i, 0)
    return pl.pallas_call(
        _mm_kernel,
        out_shape=jax.ShapeDtypeStruct((b, l, n), BF16),
        grid=(b, l // tm),
        in_specs=[pl.BlockSpec((1, tm, k), row), _full_spec(w)],
        out_specs=pl.BlockSpec((1, tm, n), row),
        compiler_params=_cparams("parallel", "parallel"),
        name="l1_kv_up",
    )(a, w)


def _rope_tables(n_tokens, d_rot):
    rows = n_tokens // GRID_W
    row = jnp.repeat(jnp.arange(rows, dtype=F32), GRID_W)
    col = jnp.tile(jnp.arange(GRID_W, dtype=F32), rows)
    n_axis = d_rot // 4
    inv_freq = ROPE_THETA ** (-jnp.arange(n_axis, dtype=F32) / n_axis)
    ang = jnp.concatenate([row[:, None] * inv_freq, col[:, None] * inv_freq], axis=-1)
    cos, sin = jnp.cos(ang), jnp.sin(ang)
    cos_t = jnp.concatenate([cos, cos], axis=-1)
    sin_t = jnp.concatenate([-sin, sin], axis=-1)
    reps = LANES // d_rot
    return jnp.tile(cos_t, (1, reps)), jnp.tile(sin_t, (1, reps))


def _mods(mod, lo, hi):
    return [mod[lo:hi, None, i * D_MODEL:(i + 1) * D_MODEL] for i in range(6)]


def _pad_router(router):
    return jnp.pad(router, ((0, 0), (0, LANES - router.shape[1]))).astype(BF16)


def kernel(x, c, ctx, c_ctx, l0_mod_w, l0_mod_b, l0_norm1_g, l0_w_in, l0_q_norm_g, l0_k_norm_g, l0_dw_w, l0_dw_b, l0_conv_ln_g, l0_conv_ln_b, l0_w_out, l0_norm2_g, l0_router, l0_w_gate, l0_w_up, l0_w_down, l1_mod_w, l1_mod_b, l1_norm1_g, l1_w_dqkv, l1_q_lora_norm_g, l1_w_uq, l1_kv_lora_norm_g, l1_w_ukv, l1_w_out, l1_norm2_g, l1_router, l1_w_gate, l1_w_up, l1_w_down, final_norm_g):
    bsz, seq, d = x.shape
    n_ctx = ctx.shape[1]
    row2 = lambda v: v.reshape(1, -1)

    cond = jnp.zeros((16, d), F32).at[:bsz].set(c).at[bsz].set(c_ctx)
    mod0 = _ada(cond, l0_mod_w, l0_mod_b)
    mod1 = _ada(cond, l1_mod_w, l1_mod_b)

    sh1, sc1, ga1, sh2, sc2, ga2 = _mods(mod0, 0, bsz)
    csh1, csc1, cga1, csh2, csc2, cga2 = _mods(mod0, bsz, bsz + 1)
    s_q, s_k, s_v, s_u = ATTN_WIDTH, ATTN_WIDTH + KV_WIDTH, ATTN_WIDTH + 2 * KV_WIDTH, ATTN_WIDTH + 2 * KV_WIDTH + CONV_WIDTH
    w_in = jnp.concatenate([l0_w_in[:, :s_q], l0_w_in[:, s_v:s_u], l0_w_in[:, s_u:],
                            l0_w_in[:, s_q:s_k], l0_w_in[:, s_k:s_v]], axis=1).astype(BF16)
    cos0, sin0 = _rope_tables(seq, HEAD_DIM)
    g1 = row2(l0_norm1_g)
    qg, kg = row2(l0_q_norm_g), row2(l0_k_norm_g)
    q_l, ug_l, kv_l = _l0_inproj(x, g1, sh1, sc1, w_in, qg, kg, cos0, sin0, rope=True, tm=512)
    q_c, ug_c, kv_c = _l0_inproj(ctx, g1, csh1, csc1, w_in, qg, kg, cos0, sin0, rope=False, tm=n_ctx)
    kv_all = jnp.concatenate([kv_c, kv_l], axis=1)
    a_l = _gqa_attention(q_l, kv_all, tq=512)
    a_c = _gqa_attention(q_c, kv_c, tq=n_ctx)
    dw_w = l0_dw_w.reshape(CONV_K, CONV_WIDTH)
    dw_b, ln_g, ln_b = row2(l0_dw_b), row2(l0_conv_ln_g), row2(l0_conv_ln_b)
    cb_l = _conv_branch(ug_l, dw_w, dw_b, ln_g, ln_b)
    cb_c = _conv_branch(ug_c, dw_w, dw_b, ln_g, ln_b)
    w_out0 = l0_w_out.astype(BF16)
    wo_a, wo_c = w_out0[:ATTN_WIDTH], w_out0[ATTN_WIDTH:]
    g2 = row2(l0_norm2_g)
    router0 = _pad_router(l0_router)
    x1, h2_l, lg_l = _outproj(x, [a_l, cb_l], [wo_a, wo_c], ga1, g2, sh2, sc2, router0, tm=256)
    xc1, h2_c, lg_c = _outproj(ctx, [a_c, cb_c], [wo_a, wo_c], cga1, g2, csh2, csc2, router0, tm=n_ctx)
    xs_l, gates_l, idx_l = _route(h2_l, lg_l)
    xs_c, gates_c, idx_c = _route(h2_c, lg_c)
    ys0 = _moe_ffn(jnp.concatenate([xs_l, xs_c], axis=1), jnp.concatenate([gates_l, gates_c], axis=1),
                   l0_w_gate, l0_w_up, l0_w_down)
    x2 = _combine_routed(x1, ga2, g2, idx_l, ys0, 0, final_norm=False)
    xc2 = _combine_routed(xc1, cga2, g2, idx_c, ys0, xs_l.shape[1], final_norm=False)

    sh1, sc1, ga1, sh2, sc2, ga2_1 = _mods(mod1, 0, bsz)
    csh1, csc1 = _mods(mod1, bsz, bsz + 1)[:2]
    w_dqkv = jnp.pad(l1_w_dqkv, ((0, 0), (0, LANES - ROPE_DIM))).astype(BF16)
    cos1, sin1 = _rope_tables(seq, ROPE_DIM)
    g1 = row2(l1_norm1_g)
    qlg, kvlg = row2(l1_q_lora_norm_g), row2(l1_kv_lora_norm_g)
    cq, ckv_l, kpe_l = _l1_dqkv(x2, g1, sh1, sc1, w_dqkv, qlg, kvlg, cos1, sin1, rope=True, with_q=True, tm=512)
    ckv_c, kpe_c = _l1_dqkv(xc2, g1, csh1, csc1, w_dqkv, qlg, kvlg, cos1, sin1, rope=False, with_q=False, tm=n_ctx)
    w_uq = l1_w_uq.reshape(Q_LORA, MLA_HEADS, QK_DIM)
    w_uq = jnp.concatenate([w_uq[:, :, :NOPE_DIM].reshape(Q_LORA, -1),
                            w_uq[:, :, NOPE_DIM:].reshape(Q_LORA, -1)], axis=1).astype(BF16)
    w_ukv = l1_w_ukv.reshape(KV_LORA, MLA_HEADS, NOPE_DIM + V_DIM)
    w_ukv = jnp.concatenate([w_ukv[:, :, :NOPE_DIM].reshape(KV_LORA, -1),
                             w_ukv[:, :, NOPE_DIM:].reshape(KV_LORA, -1)], axis=1).astype(BF16)
    qn, qp = _l1_q(cq, w_uq, cos1, sin1, tm=512)
    kv1 = _mm(jnp.concatenate([ckv_c, ckv_l], axis=1), w_ukv, tm=256)
    kpe = jnp.concatenate([kpe_c, kpe_l], axis=1)
    a1 = _mla_attention(qn, qp, kv1, kpe, tq=256)
    x3, h2, lg = _outproj(x2, [a1], [l1_w_out.astype(BF16)], ga1, row2(l1_norm2_g), sh2, sc2,
                          _pad_router(l1_router), tm=256)
    xs, gates, idx = _route(h2, lg)
    ys1 = _moe_ffn(xs, gates, l1_w_gate, l1_w_up, l1_w_down)
    return _combine_routed(x3, ga2_1, row2(final_norm_g), idx, ys1, 0, final_norm=True)
```

```python
import functools

import jax
import jax.numpy as jnp
from jax import lax
from jax.experimental import pallas as pl
from jax.experimental.pallas import tpu as pltpu

F32 = jnp.float32
BF16 = jnp.bfloat16

D_MODEL = 2048
GRID_W = 64
EPS = 1e-6
ROPE_THETA = 10000.0
HEAD_DIM = 128
N_Q_HEADS = 8
N_KV_HEADS = 2
ATTN_WIDTH = N_Q_HEADS * HEAD_DIM
KV_WIDTH = N_KV_HEADS * HEAD_DIM
CONV_WIDTH = D_MODEL - ATTN_WIDTH
CONV_K = 31
LOG2E = 1.4426950408889634
GQA_SCALE = HEAD_DIM ** -0.5
MLA_HEADS = 16
Q_LORA = 1536
KV_LORA = 512
NOPE_DIM = 128
ROPE_DIM = 64
V_DIM = 128
QK_DIM = NOPE_DIM + ROPE_DIM
MLA_SCALE = QK_DIM ** -0.5
N_EXPERTS = 16
D_EXPERT = 1024
EC_FACTOR = 2

LANES = 128
SUBLANES = 8
VMEM_LIMIT = 56 * 1024 * 1024
CONV_PAD = 16


def _cparams(*sem):
    return pltpu.CompilerParams(dimension_semantics=sem, vmem_limit_bytes=VMEM_LIMIT)


def _dot(a, b):
    return jnp.dot(a, b, preferred_element_type=F32)


def _sigmoid(x):
    return 1.0 / (1.0 + jnp.exp(-x))


def _prenorm(x, g, sh, sc):
    ms = jnp.mean(x * x, axis=-1, keepdims=True)
    y = x * lax.rsqrt(ms + EPS) * g
    return y * (1.0 + sc) + sh


def _mod_spec(arr):
    if arr.shape[0] > 1:
        return pl.BlockSpec((1, 1, arr.shape[2]), lambda b, *_: (b, 0, 0))
    return pl.BlockSpec((1, 1, arr.shape[2]), lambda b, *_: (0, 0, 0))


def _full_spec(arr):
    nd = arr.ndim
    return pl.BlockSpec(arr.shape, lambda *_: (0,) * nd)


def _ada_kernel(c_ref, w_ref, b_ref, o_ref):
    c = c_ref[...]
    s = c * _sigmoid(c)
    s_hi = s.astype(BF16)
    s_lo = (s - s_hi.astype(F32)).astype(BF16)
    w = w_ref[...]
    w_hi = w.astype(BF16)
    w_lo = (w - w_hi.astype(F32)).astype(BF16)
    o_ref[...] = _dot(s_hi, w_hi) + _dot(s_lo, w_hi) + _dot(s_hi, w_lo) + b_ref[...]


def _ada(cond, w, b):
    m, d = cond.shape
    n = w.shape[1]
    tn = 512
    return pl.pallas_call(
        _ada_kernel,
        out_shape=jax.ShapeDtypeStruct((m, n), F32),
        grid=(n // tn,),
        in_specs=[pl.BlockSpec((m, d), lambda j: (0, 0)),
                  pl.BlockSpec((d, tn), lambda j: (0, j)),
                  pl.BlockSpec((1, tn), lambda j: (0, j))],
        out_specs=pl.BlockSpec((m, tn), lambda j: (0, j)),
        compiler_params=_cparams("parallel"),
        name="ada_params",
    )(cond, w, b.reshape(1, n))


def _rope_half(y, cos, sin):
    return y * cos + pltpu.roll(y, HEAD_DIM // 2, 1) * sin


def _l0_inproj_kernel(x_ref, g_ref, sh_ref, sc_ref, w_ref, qg_ref, kg_ref, cos_ref, sin_ref,
                      q_ref, ug_ref, kv_ref, *, rope):
    h = _prenorm(x_ref[0], g_ref[...], sh_ref[0], sc_ref[0]).astype(BF16)
    cos = cos_ref[...]
    sin = sin_ref[...]

    def head(y, gain, scale):
        ms = jnp.mean(y * y, axis=-1, keepdims=True)
        y = y * lax.rsqrt(ms + EPS) * gain
        if rope:
            y = _rope_half(y, cos, sin)
        return y * scale if scale != 1.0 else y

    cw = 4 * HEAD_DIM
    for c in range(ATTN_WIDTH // cw):
        y = _dot(h, w_ref[:, c * cw:(c + 1) * cw])
        parts = [head(y[:, j * HEAD_DIM:(j + 1) * HEAD_DIM], qg_ref[...], GQA_SCALE * LOG2E) for j in range(4)]
        q_ref[0, :, c * cw:(c + 1) * cw] = jnp.concatenate(parts, axis=1).astype(BF16)
    u0 = ATTN_WIDTH
    g0 = ATTN_WIDTH + CONV_WIDTH
    for c in range(CONV_WIDTH // cw):
        u = _dot(h, w_ref[:, u0 + c * cw:u0 + (c + 1) * cw])
        gt = _dot(h, w_ref[:, g0 + c * cw:g0 + (c + 1) * cw])
        ug_ref[0, :, c * cw:(c + 1) * cw] = (u * _sigmoid(gt)).astype(BF16)
    k0 = ATTN_WIDTH + 2 * CONV_WIDTH
    y = _dot(h, w_ref[:, k0:k0 + 2 * KV_WIDTH])
    parts = [head(y[:, j * HEAD_DIM:(j + 1) * HEAD_DIM], kg_ref[...], 1.0) for j in range(N_KV_HEADS)]
    parts.append(y[:, KV_WIDTH:])
    kv_ref[0] = jnp.concatenate(parts, axis=1).astype(BF16)


def _l0_inproj(x, g, sh, sc, w, qg, kg, cos, sin, *, rope, tm):
    b, l, d = x.shape
    n = w.shape[1]
    row = lambda bi, i: (bi, i, 0)
    return pl.pallas_call(
        functools.partial(_l0_inproj_kernel, rope=rope),
        out_shape=(jax.ShapeDtypeStruct((b, l, ATTN_WIDTH), BF16),
                   jax.ShapeDtypeStruct((b, l, CONV_WIDTH), BF16),
                   jax.ShapeDtypeStruct((b, l, 2 * KV_WIDTH), BF16)),
        grid=(b, l // tm),
        in_specs=[pl.BlockSpec((1, tm, d), row), _full_spec(g), _mod_spec(sh), _mod_spec(sc),
                  _full_spec(w), _full_spec(qg), _full_spec(kg),
                  pl.BlockSpec((tm, LANES), lambda bi, i: (i, 0)),
                  pl.BlockSpec((tm, LANES), lambda bi, i: (i, 0))],
        out_specs=(pl.BlockSpec((1, tm, ATTN_WIDTH), row),
                   pl.BlockSpec((1, tm, CONV_WIDTH), row),
                   pl.BlockSpec((1, tm, 2 * KV_WIDTH), row)),
        compiler_params=_cparams("parallel", "parallel"),
        name="l0_inproj",
    )(x, g, sh, sc, w, qg, kg, cos, sin)


ATT_ROWS = 256


def _scores(q, k):
    return lax.dot_general(q, k, (((1,), (1,)), ((), ())), preferred_element_type=F32)


def _softmax_pv(s, v_ones):
    dv = v_ones.shape[1] // 2
    m = jnp.max(s, axis=-1, keepdims=True)
    o = _dot(jnp.exp2(s - m).astype(BF16), v_ones)
    return o[:, :dv] / o[:, dv:]


def _attend_units(n_units, q_of, k_of, v_of, store):
    s = _scores(q_of(0), k_of(0))
    for n in range(n_units):
        s_cur = s
        if n + 1 < n_units:
            s = _scores(q_of(n + 1), k_of(n + 1))
        store(n, _softmax_pv(s_cur, v_of(n)).astype(BF16))


def _gqa_kernel(q_ref, k_ref, v_ref, o_ref, vones_ref):
    @pl.when(pl.program_id(2) == 0)
    def _():
        vones_ref[:, 0:HEAD_DIM] = v_ref[0]
        vones_ref[:, HEAD_DIM:2 * HEAD_DIM] = jnp.ones(v_ref.shape[1:], BF16)

    k = k_ref[0]
    group = N_Q_HEADS // N_KV_HEADS
    n_units = q_ref.shape[1] // ATT_ROWS * group

    def where(n):
        r, j = divmod(n, group)
        return slice(r * ATT_ROWS, (r + 1) * ATT_ROWS), slice(j * HEAD_DIM, (j + 1) * HEAD_DIM)

    def store(n, o):
        rows, cols = where(n)
        o_ref[0, rows, cols] = o

    def q_of(n):
        rows, cols = where(n)
        return q_ref[0, rows, cols]

    _attend_units(n_units, q_of, lambda n: k, lambda n: vones_ref[...], store)


def _gqa_attention(q, kv, *, tq):
    b, l, _ = q.shape
    lk = kv.shape[1]
    gw = ATTN_WIDTH // N_KV_HEADS
    return pl.pallas_call(
        _gqa_kernel,
        out_shape=jax.ShapeDtypeStruct((b, l, ATTN_WIDTH), BF16),
        grid=(b, N_KV_HEADS, l // tq),
        in_specs=[pl.BlockSpec((1, tq, gw), lambda bi, h, i: (bi, i, h)),
                  pl.BlockSpec((1, lk, HEAD_DIM), lambda bi, h, i: (bi, 0, h)),
                  pl.BlockSpec((1, lk, HEAD_DIM), lambda bi, h, i: (bi, 0, N_KV_HEADS + h))],
        out_specs=pl.BlockSpec((1, tq, gw), lambda bi, h, i: (bi, i, h)),
        scratch_shapes=[pltpu.VMEM((lk, 2 * HEAD_DIM), BF16)],
        compiler_params=_cparams("parallel", "parallel", "arbitrary"),
        name="gqa_attention",
    )(q, kv, kv)


MLA_HB = 4


def _mla_kernel(qn_ref, qp_ref, k_ref, v_ref, kpe_ref, o_ref, kcat_ref, vones_ref):
    @pl.when(pl.program_id(2) == 0)
    def _():
        for i in range(MLA_HB):
            kcat_ref[i, :, 0:NOPE_DIM] = k_ref[0, :, i * NOPE_DIM:(i + 1) * NOPE_DIM]
            kcat_ref[i, :, NOPE_DIM:2 * NOPE_DIM] = kpe_ref[0]
            vones_ref[i, :, 0:V_DIM] = v_ref[0, :, i * V_DIM:(i + 1) * V_DIM]
            vones_ref[i, :, V_DIM:2 * V_DIM] = jnp.ones((v_ref.shape[1], V_DIM), BF16)

    lane = lax.broadcasted_iota(jnp.int32, (qp_ref.shape[1], LANES), 1)
    per_block = LANES // ROPE_DIM

    def q_of(i):
        blk, pos = divmod(i, per_block)
        qp = qp_ref[0, :, blk * LANES:(blk + 1) * LANES]
        mine = (lane >= pos * ROPE_DIM) & (lane < (pos + 1) * ROPE_DIM)
        return jnp.concatenate(
            [qn_ref[0, :, i * NOPE_DIM:(i + 1) * NOPE_DIM], jnp.where(mine, qp, jnp.zeros_like(qp))], axis=1)

    def store(i, o):
        o_ref[0, :, i * V_DIM:(i + 1) * V_DIM] = o

    _attend_units(MLA_HB, q_of, lambda i: kcat_ref[i], lambda i: vones_ref[i], store)


def _mla_attention(qn, qp, kv, kpe, *, tq):
    b, l, _ = qn.shape
    lk = kv.shape[1]
    wn = MLA_HB * NOPE_DIM
    nv0 = MLA_HEADS * NOPE_DIM // wn
    return pl.pallas_call(
        _mla_kernel,
        out_shape=jax.ShapeDtypeStruct((b, l, MLA_HEADS * V_DIM), BF16),
        grid=(b, MLA_HEADS // MLA_HB, l // tq),
        in_specs=[pl.BlockSpec((1, tq, wn), lambda bi, h, i: (bi, i, h)),
                  pl.BlockSpec((1, tq, MLA_HB * ROPE_DIM), lambda bi, h, i: (bi, i, h)),
                  pl.BlockSpec((1, lk, wn), lambda bi, h, i: (bi, 0, h)),
                  pl.BlockSpec((1, lk, wn), lambda bi, h, i: (bi, 0, nv0 + h)),
                  pl.BlockSpec((1, lk, LANES), lambda bi, h, i: (bi, 0, 0))],
        out_specs=pl.BlockSpec((1, tq, wn), lambda bi, h, i: (bi, i, h)),
        scratch_shapes=[pltpu.VMEM((MLA_HB, lk, 2 * NOPE_DIM), BF16), pltpu.VMEM((MLA_HB, lk, 2 * V_DIM), BF16)],
        compiler_params=_cparams("parallel", "parallel", "arbitrary"),
        name="mla_attention",
    )(qn, qp, kv, kv, kpe)


CONV_ROWS = 64
CONV_COLS = 128


def _conv_kernel(ug_ref, w_ref, b_ref, lg_ref, lb_ref, o_ref, pad_ref, y_ref):
    l, c = ug_ref.shape[1], ug_ref.shape[2]
    pad_ref[0:CONV_PAD, :] = jnp.zeros((CONV_PAD, c), F32)
    pad_ref[CONV_PAD + l:2 * CONV_PAD + l, :] = jnp.zeros((CONV_PAD, c), F32)
    pad_ref[CONV_PAD:CONV_PAD + l, :] = ug_ref[0].astype(F32)
    off = CONV_PAD - CONV_K // 2
    nwin = CONV_ROWS + 2 * CONV_PAD

    def body(r, carry):
        r0 = pl.multiple_of(r * CONV_ROWS, CONV_ROWS)
        for cb in range(c // CONV_COLS):
            cs = slice(cb * CONV_COLS, (cb + 1) * CONV_COLS)
            win = pad_ref[pl.ds(r0, nwin), cs]
            acc = jnp.zeros((CONV_ROWS, CONV_COLS), F32) + b_ref[:, cs]
            for s in range(SUBLANES):
                ws = win if s == 0 else pltpu.roll(win, nwin - s, 0)
                for k in range(CONV_K):
                    if (off + k) % SUBLANES == s:
                        j = (off + k) // SUBLANES * SUBLANES
                        acc = acc + w_ref[k:k + 1, cs] * ws[j:j + CONV_ROWS, :]
            y_ref[:, cs] = acc
        y = y_ref[...]
        mu = jnp.mean(y, axis=-1, keepdims=True)
        yc = y - mu
        var = jnp.mean(yc * yc, axis=-1, keepdims=True)
        z = yc * lax.rsqrt(var + EPS) * lg_ref[...] + lb_ref[...]
        o_ref[0, pl.ds(r0, CONV_ROWS), :] = (z * _sigmoid(z)).astype(BF16)
        return carry

    lax.fori_loop(0, l // CONV_ROWS, body, 0)


def _conv_branch(ug, w, b, lg, lb):
    bsz, l, c = ug.shape
    return pl.pallas_call(
        _conv_kernel,
        out_shape=jax.ShapeDtypeStruct((bsz, l, c), BF16),
        grid=(bsz,),
        in_specs=[pl.BlockSpec((1, l, c), lambda bi: (bi, 0, 0)),
                  _full_spec(w), _full_spec(b), _full_spec(lg), _full_spec(lb)],
        out_specs=pl.BlockSpec((1, l, c), lambda bi: (bi, 0, 0)),
        scratch_shapes=[pltpu.VMEM((l + 2 * CONV_PAD, c), F32), pltpu.VMEM((CONV_ROWS, c), F32)],
        compiler_params=_cparams("parallel"),
        name="conv_branch",
    )(ug, w, b, lg, lb)


def _outproj_kernel(*refs, n_in):
    x_ref = refs[0]
    a_refs = refs[1:1 + n_in]
    w_refs = refs[1 + n_in:1 + 2 * n_in]
    ga_ref, g2_ref, sh2_ref, sc2_ref, r_ref, xo_ref, h2_ref, lg_ref = refs[1 + 2 * n_in:]
    acc = _dot(a_refs[0][0], w_refs[0][...])
    for a_ref, w_ref in zip(a_refs[1:], w_refs[1:]):
        acc = acc + _dot(a_ref[0], w_ref[...])
    xn = x_ref[0] + ga_ref[0] * acc
    xo_ref[0] = xn
    h2 = _prenorm(xn, g2_ref[...], sh2_ref[0], sc2_ref[0]).astype(BF16)
    h2_ref[0] = h2
    lg_ref[0] = _dot(h2, r_ref[...])


def _outproj(x, acts, ws, ga, g2, sh2, sc2, router, *, tm):
    b, l, d = x.shape
    n_in = len(acts)
    row = lambda bi, i: (bi, i, 0)
    in_specs = [pl.BlockSpec((1, tm, d), row)]
    in_specs += [pl.BlockSpec((1, tm, a.shape[2]), row) for a in acts]
    in_specs += [_full_spec(w) for w in ws]
    in_specs += [_mod_spec(ga), _full_spec(g2), _mod_spec(sh2), _mod_spec(sc2), _full_spec(router)]
    return pl.pallas_call(
        functools.partial(_outproj_kernel, n_in=n_in),
        out_shape=(jax.ShapeDtypeStruct((b, l, d), F32),
                   jax.ShapeDtypeStruct((b, l, d), BF16),
                   jax.ShapeDtypeStruct((b, l, LANES), F32)),
        grid=(b, l // tm),
        in_specs=in_specs,
        out_specs=(pl.BlockSpec((1, tm, d), row), pl.BlockSpec((1, tm, d), row),
                   pl.BlockSpec((1, tm, LANES), row)),
        compiler_params=_cparams("parallel", "parallel"),
        name="outproj",
    )(x, *acts, *ws, ga, g2, sh2, sc2, router)


MOE_TM = 256
MOE_CHUNKS = 8


def _moe_kernel(*refs, tiles_a):
    if tiles_a is None:
        xs_ref, gate_ref = refs[:2]
        xs2_ref = gate2_ref = None
        refs = refs[2:]
    else:
        xs_ref, gate_ref, xs2_ref, gate2_ref = refs[:4]
        refs = refs[4:]
    wg_ref, wu_ref, wd_ref, o_ref, wgb_ref, wub_ref, wdb_ref = refs
    p = pl.program_id(0)
    i = pl.program_id(1)
    n_experts = pl.num_programs(0) - 1
    cg = wg_ref.shape[1]
    cd = wd_ref.shape[1]

    @pl.when((p < n_experts) & (i < MOE_CHUNKS))
    def _():
        slot = p % 2
        rg = pl.multiple_of(i * cg, cg)
        rd = pl.multiple_of(i * cd, cd)
        wgb_ref[slot, pl.ds(rg, cg), :] = wg_ref[0].astype(BF16)
        wub_ref[slot, pl.ds(rg, cg), :] = wu_ref[0].astype(BF16)
        wdb_ref[slot, pl.ds(rd, cd), :] = wd_ref[0].astype(BF16)

    @pl.when(p == 0)
    def _():
        o_ref[...] = jnp.zeros_like(o_ref)

    @pl.when(p > 0)
    def _():
        slot = (p + 1) % 2
        x = xs_ref[0]
        gate = gate_ref[0]
        if tiles_a is not None:
            x = jnp.where(i < tiles_a, x, xs2_ref[0])
            gate = jnp.where(i < tiles_a, gate, gate2_ref[0])
        g = _dot(x, wgb_ref[slot])
        u = _dot(x, wub_ref[slot])
        hid = (g * _sigmoid(g) * u).astype(BF16)
        o_ref[0] = (_dot(hid, wdb_ref[slot]) * gate).astype(BF16)


def _moe_ffn(xs, gates, wg, wu, wd, xs2=None, gates2=None):
    e, m, d = xs.shape
    f = wg.shape[2]
    tiles_a = m // MOE_TM
    assert m % MOE_TM == 0 and tiles_a >= MOE_CHUNKS, "one weight chunk is streamed per row-tile step"
    last_e, last_c, last_a = e - 1, MOE_CHUNKS - 1, tiles_a - 1
    rows = lambda p, i: (jnp.maximum(p - 1, 0), jnp.minimum(i, last_a), 0)
    chunk = lambda p, i: (jnp.minimum(p, last_e), jnp.minimum(i, last_c), 0)
    args = [xs, gates]
    in_specs = [pl.BlockSpec((1, MOE_TM, d), rows), pl.BlockSpec((1, MOE_TM, 1), rows)]
    if xs2 is not None:
        assert xs2.shape == (e, MOE_TM, d)
        rows2 = lambda p, i: (jnp.maximum(p - 1, 0), 0, 0)
        args += [xs2, gates2]
        in_specs += [pl.BlockSpec((1, MOE_TM, d), rows2), pl.BlockSpec((1, MOE_TM, 1), rows2)]
    tiles = tiles_a + (xs2 is not None)
    in_specs += [pl.BlockSpec((1, d // MOE_CHUNKS, f), chunk),
                 pl.BlockSpec((1, d // MOE_CHUNKS, f), chunk),
                 pl.BlockSpec((1, f // MOE_CHUNKS, d), chunk)]
    return pl.pallas_call(
        functools.partial(_moe_kernel, tiles_a=tiles_a if xs2 is not None else None),
        out_shape=jax.ShapeDtypeStruct((e + 1, tiles * MOE_TM, d), BF16),
        grid=(e + 1, tiles),
        in_specs=in_specs,
        out_specs=pl.BlockSpec((1, MOE_TM, d), lambda p, i: (p, i, 0)),
        scratch_shapes=[pltpu.VMEM((2, d, f), BF16), pltpu.VMEM((2, d, f), BF16), pltpu.VMEM((2, f, d), BF16)],
        compiler_params=_cparams("arbitrary", "arbitrary"),
        name="moe_ffn",
    )(*args, wg, wu, wd)


COMBINE_TT = 256
COMBINE_W = 256
COMBINE_KW = 1024


def _combine_kernel(p0_ref, tok_ref, z_ref, x_ref, ga_ref, g_ref, o_ref, acc_ref, *, final_norm):
    bi = pl.program_id(0)
    ti = pl.program_id(1)
    p0 = p0_ref[bi, ti]
    p1 = p0_ref[bi, ti + 1]
    n_pairs = z_ref.shape[1]
    kw = min(COMBINE_KW, n_pairs)
    tok0 = ti * COMBINE_TT

    def window(start, width):
        tok_row = tok0 + lax.broadcasted_iota(jnp.int32, (COMBINE_TT, width), 0)
        toks = tok_ref[0, :, pl.ds(start, width)]
        onehot = jnp.where(toks == tok_row, 1.0, 0.0).astype(BF16)
        return _dot(onehot, z_ref[0, pl.ds(start, width), :])

    first = jnp.minimum(p0 // COMBINE_W * COMBINE_W, n_pairs - kw)
    fits = p1 <= first + kw

    @pl.when(fits)
    def _():
        acc_ref[...] = window(pl.multiple_of(first, COMBINE_W), kw)

    @pl.when(jnp.logical_not(fits))
    def _():
        acc_ref[...] = jnp.zeros_like(acc_ref)

        def body(w, carry):
            acc_ref[...] += window(pl.multiple_of(w * COMBINE_W, COMBINE_W), COMBINE_W)
            return carry

        lax.fori_loop(p0 // COMBINE_W, (p1 + COMBINE_W - 1) // COMBINE_W, body, 0)

    x = x_ref[0] + ga_ref[0] * acc_ref[...]
    if final_norm:
        ms = jnp.mean(x * x, axis=-1, keepdims=True)
        x = x * lax.rsqrt(ms + EPS) * g_ref[...]
    o_ref[0] = x


def _combine(x, ga, g, tok_sorted, z, p0, *, final_norm):
    b, n, d = x.shape
    p = z.shape[1]
    row = lambda bi, ti, *_: (bi, ti, 0)
    grid_spec = pltpu.PrefetchScalarGridSpec(
        num_scalar_prefetch=1,
        grid=(b, n // COMBINE_TT),
        in_specs=[pl.BlockSpec((1, 1, p), lambda bi, ti, *_: (bi, 0, 0)),
                  pl.BlockSpec((1, p, d), lambda bi, ti, *_: (bi, 0, 0)),
                  pl.BlockSpec((1, COMBINE_TT, d), row), _mod_spec(ga), _full_spec(g)],
        out_specs=pl.BlockSpec((1, COMBINE_TT, d), row),
        scratch_shapes=[pltpu.VMEM((COMBINE_TT, d), F32)])
    return pl.pallas_call(
        functools.partial(_combine_kernel, final_norm=final_norm),
        out_shape=jax.ShapeDtypeStruct((b, n, d), F32),
        grid_spec=grid_spec,
        compiler_params=_cparams("parallel", "arbitrary"),
        name="moe_combine",
    )(p0, tok_sorted, z, x, ga, g)


def _route(h2, logits):
    b, n, d = h2.shape
    cap = EC_FACTOR * n // N_EXPERTS
    aff = jax.nn.softmax(logits[..., :N_EXPERTS], axis=-1)
    gate, idx = lax.top_k(jnp.swapaxes(aff, 1, 2), cap)
    idx_e = jnp.swapaxes(idx, 0, 1)
    bidx = jnp.arange(b, dtype=idx.dtype)[None, :, None]
    xs = h2[bidx, idx_e].reshape(N_EXPERTS, b * cap, d)
    gates = jnp.swapaxes(gate, 0, 1).reshape(N_EXPERTS, b * cap, 1)
    return xs, gates, idx


def _combine_routed(x, ga, g, idx, ys, row0, *, final_norm):
    b, n, d = x.shape
    cap = idx.shape[2]
    pairs = N_EXPERTS * cap
    m = ys.shape[1]
    slots = lax.broadcasted_iota(jnp.int32, (b, pairs), 1)
    tok_sorted, order = lax.sort((idx.reshape(b, pairs), slots), dimension=1, num_keys=1)
    flat = (order // cap + 1) * m + row0 + jnp.arange(b, dtype=jnp.int32)[:, None] * cap + order % cap
    z = ys.reshape((N_EXPERTS + 1) * m, d)[flat]
    bounds = jnp.arange(n // COMBINE_TT + 1, dtype=jnp.int32) * COMBINE_TT
    p0 = jnp.sum(tok_sorted[:, None, :] < bounds[None, :, None], axis=-1).astype(jnp.int32)
    return _combine(x, ga, g, tok_sorted.reshape(b, 1, pairs), z, p0, final_norm=final_norm)


def _rope_quarter(y, cos, sin, first):
    rot = jnp.where(first, pltpu.roll(y, LANES - ROPE_DIM // 2, 1), pltpu.roll(y, ROPE_DIM // 2, 1))
    return y * cos + rot * sin


def _l1_dqkv_kernel(x_ref, g_ref, sh_ref, sc_ref, w_ref, qg_ref, kvg_ref, cos_ref, sin_ref, *out_refs,
                    with_q, rope):
    if with_q:
        cq_ref, ckv_ref, kpe_ref = out_refs
    else:
        ckv_ref, kpe_ref = out_refs
    h = _prenorm(x_ref[0], g_ref[...], sh_ref[0], sc_ref[0]).astype(BF16)
    if with_q:
        cq = _dot(h, w_ref[:, 0:Q_LORA])
        ms = jnp.mean(cq * cq, axis=-1, keepdims=True)
        cq_ref[0] = (cq * lax.rsqrt(ms + EPS) * qg_ref[...]).astype(BF16)
    z = _dot(h, w_ref[:, Q_LORA:Q_LORA + KV_LORA + LANES])
    ckv = z[:, 0:KV_LORA]
    ms = jnp.mean(ckv * ckv, axis=-1, keepdims=True)
    ckv_ref[0] = (ckv * lax.rsqrt(ms + EPS) * kvg_ref[...]).astype(BF16)
    pe = z[:, KV_LORA:KV_LORA + LANES]
    if rope:
        lane = lax.broadcasted_iota(jnp.int32, pe.shape, 1)
        pe = _rope_quarter(pe, cos_ref[...], sin_ref[...], (lane % ROPE_DIM) < ROPE_DIM // 2)
    kpe_ref[0] = (pe + pltpu.roll(pe, ROPE_DIM, 1)).astype(BF16)


def _l1_dqkv(x, g, sh, sc, w, qg, kvg, cos, sin, *, rope, with_q, tm):
    b, l, d = x.shape
    row = lambda bi, i: (bi, i, 0)
    args = [x, g, sh, sc, w, qg, kvg, cos, sin]
    in_specs = [pl.BlockSpec((1, tm, d), row), _full_spec(g), _mod_spec(sh), _mod_spec(sc), _full_spec(w),
                _full_spec(qg), _full_spec(kvg),
                pl.BlockSpec((tm, LANES), lambda bi, i: (i, 0)), pl.BlockSpec((tm, LANES), lambda bi, i: (i, 0))]
    out_shape, out_specs = [], []
    if with_q:
        out_shape.append(jax.ShapeDtypeStruct((b, l, Q_LORA), BF16))
        out_specs.append(pl.BlockSpec((1, tm, Q_LORA), row))
    out_shape += [jax.ShapeDtypeStruct((b, l, KV_LORA), BF16), jax.ShapeDtypeStruct((b, l, LANES), BF16)]
    out_specs += [pl.BlockSpec((1, tm, KV_LORA), row), pl.BlockSpec((1, tm, LANES), row)]

    return pl.pallas_call(
        functools.partial(_l1_dqkv_kernel, with_q=with_q, rope=rope),
        out_shape=tuple(out_shape),
        grid=(b, l // tm),
        in_specs=in_specs,
        out_specs=tuple(out_specs),
        compiler_params=_cparams("parallel", "parallel"),
        name="l1_dqkv",
    )(*args)


def _l1_q_kernel(cq_ref, w_ref, cos_ref, sin_ref, qn_ref, qp_ref):
    a = cq_ref[0]
    cw = 4 * LANES
    n_nope = MLA_HEADS * NOPE_DIM
    qscale = MLA_SCALE * LOG2E
    for c in range(n_nope // cw):
        qn_ref[0, :, c * cw:(c + 1) * cw] = (_dot(a, w_ref[:, c * cw:(c + 1) * cw]) * qscale).astype(BF16)
    cos = cos_ref[...]
    sin = sin_ref[...]
    lane = lax.broadcasted_iota(jnp.int32, cos.shape, 1)
    first = (lane % ROPE_DIM) < ROPE_DIM // 2
    for c in range(MLA_HEADS * ROPE_DIM // cw):
        y = _dot(a, w_ref[:, n_nope + c * cw:n_nope + (c + 1) * cw])
        parts = [_rope_quarter(y[:, j * LANES:(j + 1) * LANES], cos, sin, first) * qscale for j in range(4)]
        qp_ref[0, :, c * cw:(c + 1) * cw] = jnp.concatenate(parts, axis=1).astype(BF16)


def _l1_q(cq, w, cos, sin, *, tm):
    b, l, k = cq.shape
    row = lambda bi, i: (bi, i, 0)
    return pl.pallas_call(
        _l1_q_kernel,
        out_shape=(jax.ShapeDtypeStruct((b, l, MLA_HEADS * NOPE_DIM), BF16),
                   jax.ShapeDtypeStruct((b, l, MLA_HEADS * ROPE_DIM), BF16)),
        grid=(b, l // tm),
        in_specs=[pl.BlockSpec((1, tm, k), row), _full_spec(w),
                  pl.BlockSpec((tm, LANES), lambda bi, i: (i, 0)), pl.BlockSpec((tm, LANES), lambda bi, i: (i, 0))],
        out_specs=(pl.BlockSpec((1, tm, MLA_HEADS * NOPE_DIM), row),
                   pl.BlockSpec((1, tm, MLA_HEADS * ROPE_DIM), row)),
        compiler_params=_cparams("parallel", "parallel"),
        name="l1_q",
    )(cq, w, cos, sin)


def _mm_kernel(a_ref, w_ref, o_ref):
    a = a_ref[0]
    cw = 4 * LANES
    for c in range(w_ref.shape[1] // cw):
        o_ref[0, :, c * cw:(c + 1) * cw] = _dot(a, w_ref[:, c * cw:(c + 1) * cw]).astype(o_ref.dtype)


def _mm(a, w, *, tm):
    b, l, k = a.shape
    n = w.shape[1]
    row = lambda bi, i: (bi, i, 0)
    return pl.pallas_call(
        _mm_kernel,
        out_shape=jax.ShapeDtypeStruct((b, l, n), BF16),
        grid=(b, l // tm),
        in_specs=[pl.BlockSpec((1, tm, k), row), _full_spec(w)],
        out_specs=pl.BlockSpec((1, tm, n), row),
        compiler_params=_cparams("parallel", "parallel"),
        name="l1_kv_up",
    )(a, w)


def _rope_tables(n_tokens, d_rot):
    rows = n_tokens // GRID_W
    row = jnp.repeat(jnp.arange(rows, dtype=F32), GRID_W)
    col = jnp.tile(jnp.arange(GRID_W, dtype=F32), rows)
    n_axis = d_rot // 4
    inv_freq = ROPE_THETA ** (-jnp.arange(n_axis, dtype=F32) / n_axis)
    ang = jnp.concatenate([row[:, None] * inv_freq, col[:, None] * inv_freq], axis=-1)
    cos, sin = jnp.cos(ang), jnp.sin(ang)
    cos_t = jnp.concatenate([cos, cos], axis=-1)
    sin_t = jnp.concatenate([-sin, sin], axis=-1)
    reps = LANES // d_rot
    return jnp.tile(cos_t, (1, reps)), jnp.tile(sin_t, (1, reps))


def _mods(mod, lo, hi):
    return [mod[lo:hi, None, i * D_MODEL:(i + 1) * D_MODEL] for i in range(6)]


def _pad_router(router):
    return jnp.pad(router, ((0, 0), (0, LANES - router.shape[1]))).astype(BF16)


def kernel(x, c, ctx, c_ctx, l0_mod_w, l0_mod_b, l0_norm1_g, l0_w_in, l0_q_norm_g, l0_k_norm_g, l0_dw_w, l0_dw_b, l0_conv_ln_g, l0_conv_ln_b, l0_w_out, l0_norm2_g, l0_router, l0_w_gate, l0_w_up, l0_w_down, l1_mod_w, l1_mod_b, l1_norm1_g, l1_w_dqkv, l1_q_lora_norm_g, l1_w_uq, l1_kv_lora_norm_g, l1_w_ukv, l1_w_out, l1_norm2_g, l1_router, l1_w_gate, l1_w_up, l1_w_down, final_norm_g):
    bsz, seq, d = x.shape
    n_ctx = ctx.shape[1]
    row2 = lambda v: v.reshape(1, -1)

    cond = jnp.zeros((16, d), F32).at[:bsz].set(c).at[bsz].set(c_ctx)
    mod0 = _ada(cond, l0_mod_w, l0_mod_b)
    mod1 = _ada(cond, l1_mod_w, l1_mod_b)

    sh1, sc1, ga1, sh2, sc2, ga2 = _mods(mod0, 0, bsz)
    csh1, csc1, cga1, csh2, csc2, cga2 = _mods(mod0, bsz, bsz + 1)
    s_q, s_k, s_v, s_u = ATTN_WIDTH, ATTN_WIDTH + KV_WIDTH, ATTN_WIDTH + 2 * KV_WIDTH, ATTN_WIDTH + 2 * KV_WIDTH + CONV_WIDTH
    w_in = jnp.concatenate([l0_w_in[:, :s_q], l0_w_in[:, s_v:s_u], l0_w_in[:, s_u:],
                            l0_w_in[:, s_q:s_k], l0_w_in[:, s_k:s_v]], axis=1).astype(BF16)
    cos0, sin0 = _rope_tables(seq, HEAD_DIM)
    g1 = row2(l0_norm1_g)
    qg, kg = row2(l0_q_norm_g), row2(l0_k_norm_g)
    q_l, ug_l, kv_l = _l0_inproj(x, g1, sh1, sc1, w_in, qg, kg, cos0, sin0, rope=True, tm=512)
    q_c, ug_c, kv_c = _l0_inproj(ctx, g1, csh1, csc1, w_in, qg, kg, cos0, sin0, rope=False, tm=n_ctx)
    kv_all = jnp.concatenate([kv_c, kv_l], axis=1)
    a_l = _gqa_attention(q_l, kv_all, tq=512)
    dw_w = l0_dw_w.reshape(CONV_K, CONV_WIDTH)
    dw_b, ln_g, ln_b = row2(l0_dw_b), row2(l0_conv_ln_g), row2(l0_conv_ln_b)
    cb_l = _conv_branch(ug_l, dw_w, dw_b, ln_g, ln_b)
    w_out0 = l0_w_out.astype(BF16)
    wo_a, wo_c = w_out0[:ATTN_WIDTH], w_out0[ATTN_WIDTH:]
    g2 = row2(l0_norm2_g)
    router0 = _pad_router(l0_router)
    x1, h2_l, lg_l = _outproj(x, [a_l, cb_l], [wo_a, wo_c], ga1, g2, sh2, sc2, router0, tm=256)
    xs_l, gates_l, idx_l = _route(h2_l, lg_l)
    a_c = _gqa_attention(q_c, kv_c, tq=n_ctx)
    cb_c = _conv_branch(ug_c, dw_w, dw_b, ln_g, ln_b)
    xc1, h2_c, lg_c = _outproj(ctx, [a_c, cb_c], [wo_a, wo_c], cga1, g2, csh2, csc2, router0, tm=n_ctx)
    xs_c, gates_c, idx_c = _route(h2_c, lg_c)
    ys0 = _moe_ffn(xs_l, gates_l, l0_w_gate, l0_w_up, l0_w_down, xs_c, gates_c)
    x2 = _combine_routed(x1, ga2, g2, idx_l, ys0, 0, final_norm=False)
    xc2 = _combine_routed(xc1, cga2, g2, idx_c, ys0, xs_l.shape[1], final_norm=False)

    sh1, sc1, ga1, sh2, sc2, ga2_1 = _mods(mod1, 0, bsz)
    csh1, csc1 = _mods(mod1, bsz, bsz + 1)[:2]
    w_dqkv = jnp.pad(l1_w_dqkv, ((0, 0), (0, LANES - ROPE_DIM))).astype(BF16)
    cos1, sin1 = _rope_tables(seq, ROPE_DIM)
    g1 = row2(l1_norm1_g)
    qlg, kvlg = row2(l1_q_lora_norm_g), row2(l1_kv_lora_norm_g)
    cq, ckv_l, kpe_l = _l1_dqkv(x2, g1, sh1, sc1, w_dqkv, qlg, kvlg, cos1, sin1, rope=True, with_q=True, tm=512)
    ckv_c, kpe_c = _l1_dqkv(xc2, g1, csh1, csc1, w_dqkv, qlg, kvlg, cos1, sin1, rope=False, with_q=False, tm=n_ctx)
    w_uq = l1_w_uq.reshape(Q_LORA, MLA_HEADS, QK_DIM)
    w_uq = jnp.concatenate([w_uq[:, :, :NOPE_DIM].reshape(Q_LORA, -1),
                            w_uq[:, :, NOPE_DIM:].reshape(Q_LORA, -1)], axis=1).astype(BF16)
    w_ukv = l1_w_ukv.reshape(KV_LORA, MLA_HEADS, NOPE_DIM + V_DIM)
    w_ukv = jnp.concatenate([w_ukv[:, :, :NOPE_DIM].reshape(KV_LORA, -1),
                             w_ukv[:, :, NOPE_DIM:].reshape(KV_LORA, -1)], axis=1).astype(BF16)
    qn, qp = _l1_q(cq, w_uq, cos1, sin1, tm=512)
    kv1 = _mm(jnp.concatenate([ckv_c, ckv_l], axis=1), w_ukv, tm=256)
    kpe = jnp.concatenate([kpe_c, kpe_l], axis=1)
    a1 = _mla_attention(qn, qp, kv1, kpe, tq=256)
    x3, h2, lg = _outproj(x2, [a1], [l1_w_out.astype(BF16)], ga1, row2(l1_norm2_g), sh2, sc2,
                          _pad_router(l1_router), tm=256)
    xs, gates, idx = _route(h2, lg)
    ys1 = _moe_ffn(xs, gates, l1_w_gate, l1_w_up, l1_w_down)
    return _combine_routed(x3, ga2_1, row2(final_norm_g), idx, ys1, 0, final_norm=True)
```

```python
import functools

import jax
import jax.numpy as jnp
from jax import lax
from jax.experimental import pallas as pl
from jax.experimental.pallas import tpu as pltpu

F32 = jnp.float32
BF16 = jnp.bfloat16

D_MODEL = 2048
GRID_W = 64
EPS = 1e-6
ROPE_THETA = 10000.0
HEAD_DIM = 128
N_Q_HEADS = 8
N_KV_HEADS = 2
ATTN_WIDTH = N_Q_HEADS * HEAD_DIM
KV_WIDTH = N_KV_HEADS * HEAD_DIM
CONV_WIDTH = D_MODEL - ATTN_WIDTH
CONV_K = 31
LOG2E = 1.4426950408889634
GQA_SCALE = HEAD_DIM ** -0.5
MLA_HEADS = 16
Q_LORA = 1536
KV_LORA = 512
NOPE_DIM = 128
ROPE_DIM = 64
V_DIM = 128
QK_DIM = NOPE_DIM + ROPE_DIM
MLA_SCALE = QK_DIM ** -0.5
N_EXPERTS = 16
D_EXPERT = 1024
EC_FACTOR = 2

LANES = 128
SUBLANES = 8
VMEM_LIMIT = 56 * 1024 * 1024
CONV_PAD = 16


def _cparams(*sem):
    return pltpu.CompilerParams(dimension_semantics=sem, vmem_limit_bytes=VMEM_LIMIT)


def _dot(a, b):
    return jnp.dot(a, b, preferred_element_type=F32)


def _sigmoid(x):
    return 1.0 / (1.0 + jnp.exp(-x))


def _prenorm(x, g, sh, sc):
    ms = jnp.mean(x * x, axis=-1, keepdims=True)
    y = x * lax.rsqrt(ms + EPS) * g
    return y * (1.0 + sc) + sh


def _mod_spec(arr, b0=0):
    if arr.shape[0] > 1:
        return pl.BlockSpec((1, 1, arr.shape[2]), lambda b, *_: (b0 + b, 0, 0))
    return pl.BlockSpec((1, 1, arr.shape[2]), lambda b, *_: (0, 0, 0))


def _full_spec(arr):
    nd = arr.ndim
    return pl.BlockSpec(arr.shape, lambda *_: (0,) * nd)


def _ada_kernel(c_ref, w_ref, b_ref, o_ref):
    c = c_ref[...]
    s = c * _sigmoid(c)
    s_hi = s.astype(BF16)
    s_lo = (s - s_hi.astype(F32)).astype(BF16)
    w = w_ref[...]
    w_hi = w.astype(BF16)
    w_lo = (w - w_hi.astype(F32)).astype(BF16)
    o_ref[...] = _dot(s_hi, w_hi) + _dot(s_lo, w_hi) + _dot(s_hi, w_lo) + b_ref[...]


def _ada(cond, w, b):
    m, d = cond.shape
    n = w.shape[1]
    tn = 512
    return pl.pallas_call(
        _ada_kernel,
        out_shape=jax.ShapeDtypeStruct((m, n), F32),
        grid=(n // tn,),
        in_specs=[pl.BlockSpec((m, d), lambda j: (0, 0)),
                  pl.BlockSpec((d, tn), lambda j: (0, j)),
                  pl.BlockSpec((1, tn), lambda j: (0, j))],
        out_specs=pl.BlockSpec((m, tn), lambda j: (0, j)),
        compiler_params=_cparams("parallel"),
        name="ada_params",
    )(cond, w, b.reshape(1, n))


def _rope_half(y, cos, sin):
    return y * cos + pltpu.roll(y, HEAD_DIM // 2, 1) * sin


def _l0_inproj_kernel(x_ref, g_ref, sh_ref, sc_ref, w_ref, qg_ref, kg_ref, cos_ref, sin_ref,
                      q_ref, ug_ref, kv_ref, *, rope):
    h = _prenorm(x_ref[0], g_ref[...], sh_ref[0], sc_ref[0]).astype(BF16)
    cos = cos_ref[...]
    sin = sin_ref[...]

    def head(y, gain, scale):
        ms = jnp.mean(y * y, axis=-1, keepdims=True)
        y = y * lax.rsqrt(ms + EPS) * gain
        if rope:
            y = _rope_half(y, cos, sin)
        return y * scale if scale != 1.0 else y

    cw = 4 * HEAD_DIM
    for c in range(ATTN_WIDTH // cw):
        y = _dot(h, w_ref[:, c * cw:(c + 1) * cw])
        parts = [head(y[:, j * HEAD_DIM:(j + 1) * HEAD_DIM], qg_ref[...], GQA_SCALE * LOG2E) for j in range(4)]
        q_ref[0, :, c * cw:(c + 1) * cw] = jnp.concatenate(parts, axis=1).astype(BF16)
    u0 = ATTN_WIDTH
    g0 = ATTN_WIDTH + CONV_WIDTH
    for c in range(CONV_WIDTH // cw):
        u = _dot(h, w_ref[:, u0 + c * cw:u0 + (c + 1) * cw])
        gt = _dot(h, w_ref[:, g0 + c * cw:g0 + (c + 1) * cw])
        ug_ref[0, :, c * cw:(c + 1) * cw] = (u * _sigmoid(gt)).astype(BF16)
    k0 = ATTN_WIDTH + 2 * CONV_WIDTH
    y = _dot(h, w_ref[:, k0:k0 + 2 * KV_WIDTH])
    parts = [head(y[:, j * HEAD_DIM:(j + 1) * HEAD_DIM], kg_ref[...], 1.0) for j in range(N_KV_HEADS)]
    parts.append(y[:, KV_WIDTH:])
    kv_ref[0] = jnp.concatenate(parts, axis=1).astype(BF16)


def _l0_inproj(x, g, sh, sc, w, qg, kg, cos, sin, *, rope, tm):
    b, l, d = x.shape
    n = w.shape[1]
    row = lambda bi, i: (bi, i, 0)
    return pl.pallas_call(
        functools.partial(_l0_inproj_kernel, rope=rope),
        out_shape=(jax.ShapeDtypeStruct((b, l, ATTN_WIDTH), BF16),
                   jax.ShapeDtypeStruct((b, l, CONV_WIDTH), BF16),
                   jax.ShapeDtypeStruct((b, l, 2 * KV_WIDTH), BF16)),
        grid=(b, l // tm),
        in_specs=[pl.BlockSpec((1, tm, d), row), _full_spec(g), _mod_spec(sh), _mod_spec(sc),
                  _full_spec(w), _full_spec(qg), _full_spec(kg),
                  pl.BlockSpec((tm, LANES), lambda bi, i: (i, 0)),
                  pl.BlockSpec((tm, LANES), lambda bi, i: (i, 0))],
        out_specs=(pl.BlockSpec((1, tm, ATTN_WIDTH), row),
                   pl.BlockSpec((1, tm, CONV_WIDTH), row),
                   pl.BlockSpec((1, tm, 2 * KV_WIDTH), row)),
        compiler_params=_cparams("parallel", "parallel"),
        name="l0_inproj",
    )(x, g, sh, sc, w, qg, kg, cos, sin)


ATT_ROWS = 256


def _scores(q, k):
    return lax.dot_general(q, k, (((1,), (1,)), ((), ())), preferred_element_type=F32)


def _softmax_pv(s, v_ones):
    dv = v_ones.shape[1] // 2
    m = jnp.max(s, axis=-1, keepdims=True)
    o = _dot(jnp.exp2(s - m).astype(BF16), v_ones)
    return o[:, :dv] / o[:, dv:]


def _attend_units(n_units, q_of, k_of, v_of, store):
    s = _scores(q_of(0), k_of(0))
    for n in range(n_units):
        s_cur = s
        if n + 1 < n_units:
            s = _scores(q_of(n + 1), k_of(n + 1))
        store(n, _softmax_pv(s_cur, v_of(n)).astype(BF16))


def _gqa_kernel(q_ref, k_ref, v_ref, o_ref, vones_ref):
    @pl.when(pl.program_id(2) == 0)
    def _():
        vones_ref[:, 0:HEAD_DIM] = v_ref[0]
        vones_ref[:, HEAD_DIM:2 * HEAD_DIM] = jnp.ones(v_ref.shape[1:], BF16)

    k = k_ref[0]
    group = N_Q_HEADS // N_KV_HEADS
    n_units = q_ref.shape[1] // ATT_ROWS * group

    def where(n):
        r, j = divmod(n, group)
        return slice(r * ATT_ROWS, (r + 1) * ATT_ROWS), slice(j * HEAD_DIM, (j + 1) * HEAD_DIM)

    def store(n, o):
        rows, cols = where(n)
        o_ref[0, rows, cols] = o

    def q_of(n):
        rows, cols = where(n)
        return q_ref[0, rows, cols]

    _attend_units(n_units, q_of, lambda n: k, lambda n: vones_ref[...], store)


def _gqa_attention(q, kv, *, tq):
    b, l, _ = q.shape
    lk = kv.shape[1]
    gw = ATTN_WIDTH // N_KV_HEADS
    return pl.pallas_call(
        _gqa_kernel,
        out_shape=jax.ShapeDtypeStruct((b, l, ATTN_WIDTH), BF16),
        grid=(b, N_KV_HEADS, l // tq),
        in_specs=[pl.BlockSpec((1, tq, gw), lambda bi, h, i: (bi, i, h)),
                  pl.BlockSpec((1, lk, HEAD_DIM), lambda bi, h, i: (bi, 0, h)),
                  pl.BlockSpec((1, lk, HEAD_DIM), lambda bi, h, i: (bi, 0, N_KV_HEADS + h))],
        out_specs=pl.BlockSpec((1, tq, gw), lambda bi, h, i: (bi, i, h)),
        scratch_shapes=[pltpu.VMEM((lk, 2 * HEAD_DIM), BF16)],
        compiler_params=_cparams("parallel", "parallel", "arbitrary"),
        name="gqa_attention",
    )(q, kv, kv)


MLA_HB = 4


def _mla_kernel(qn_ref, qp_ref, k_ref, v_ref, kpe_ref, o_ref, kcat_ref, vones_ref):
    @pl.when(pl.program_id(2) == 0)
    def _():
        for i in range(MLA_HB):
            kcat_ref[i, :, 0:NOPE_DIM] = k_ref[0, :, i * NOPE_DIM:(i + 1) * NOPE_DIM]
            kcat_ref[i, :, NOPE_DIM:2 * NOPE_DIM] = kpe_ref[0]
            vones_ref[i, :, 0:V_DIM] = v_ref[0, :, i * V_DIM:(i + 1) * V_DIM]
            vones_ref[i, :, V_DIM:2 * V_DIM] = jnp.ones((v_ref.shape[1], V_DIM), BF16)

    lane = lax.broadcasted_iota(jnp.int32, (ATT_ROWS, LANES), 1)
    per_block = LANES // ROPE_DIM
    n_units = qn_ref.shape[1] // ATT_ROWS * MLA_HB

    def q_of(n):
        r, i = divmod(n, MLA_HB)
        rows = slice(r * ATT_ROWS, (r + 1) * ATT_ROWS)
        blk, pos = divmod(i, per_block)
        qp = qp_ref[0, rows, blk * LANES:(blk + 1) * LANES]
        mine = (lane >= pos * ROPE_DIM) & (lane < (pos + 1) * ROPE_DIM)
        return jnp.concatenate(
            [qn_ref[0, rows, i * NOPE_DIM:(i + 1) * NOPE_DIM], jnp.where(mine, qp, jnp.zeros_like(qp))], axis=1)

    def store(n, o):
        r, i = divmod(n, MLA_HB)
        o_ref[0, r * ATT_ROWS:(r + 1) * ATT_ROWS, i * V_DIM:(i + 1) * V_DIM] = o

    _attend_units(n_units, q_of, lambda n: kcat_ref[n % MLA_HB], lambda n: vones_ref[n % MLA_HB], store)


def _mla_attention(qn, qp, kv, kpe, *, tq):
    b, l, _ = qn.shape
    lk = kv.shape[1]
    wn = MLA_HB * NOPE_DIM
    nv0 = MLA_HEADS * NOPE_DIM // wn
    return pl.pallas_call(
        _mla_kernel,
        out_shape=jax.ShapeDtypeStruct((b, l, MLA_HEADS * V_DIM), BF16),
        grid=(b, MLA_HEADS // MLA_HB, l // tq),
        in_specs=[pl.BlockSpec((1, tq, wn), lambda bi, h, i: (bi, i, h)),
                  pl.BlockSpec((1, tq, MLA_HB * ROPE_DIM), lambda bi, h, i: (bi, i, h)),
                  pl.BlockSpec((1, lk, wn), lambda bi, h, i: (bi, 0, h)),
                  pl.BlockSpec((1, lk, wn), lambda bi, h, i: (bi, 0, nv0 + h)),
                  pl.BlockSpec((1, lk, LANES), lambda bi, h, i: (bi, 0, 0))],
        out_specs=pl.BlockSpec((1, tq, wn), lambda bi, h, i: (bi, i, h)),
        scratch_shapes=[pltpu.VMEM((MLA_HB, lk, 2 * NOPE_DIM), BF16), pltpu.VMEM((MLA_HB, lk, 2 * V_DIM), BF16)],
        compiler_params=_cparams("parallel", "parallel", "arbitrary"),
        name="mla_attention",
    )(qn, qp, kv, kv, kpe)


CONV_ROWS = 64
CONV_COLS = 128


def _conv_kernel(ug_ref, w_ref, b_ref, lg_ref, lb_ref, o_ref, pad_ref, y_ref):
    l, c = ug_ref.shape[1], ug_ref.shape[2]
    pad_ref[0:CONV_PAD, :] = jnp.zeros((CONV_PAD, c), F32)
    pad_ref[CONV_PAD + l:2 * CONV_PAD + l, :] = jnp.zeros((CONV_PAD, c), F32)
    pad_ref[CONV_PAD:CONV_PAD + l, :] = ug_ref[0].astype(F32)
    off = CONV_PAD - CONV_K // 2
    nwin = CONV_ROWS + 2 * CONV_PAD

    def body(r, carry):
        r0 = pl.multiple_of(r * CONV_ROWS, CONV_ROWS)
        for cb in range(c // CONV_COLS):
            cs = slice(cb * CONV_COLS, (cb + 1) * CONV_COLS)
            win = pad_ref[pl.ds(r0, nwin), cs]
            acc = jnp.zeros((CONV_ROWS, CONV_COLS), F32) + b_ref[:, cs]
            for s in range(SUBLANES):
                ws = win if s == 0 else pltpu.roll(win, nwin - s, 0)
                for k in range(CONV_K):
                    if (off + k) % SUBLANES == s:
                        j = (off + k) // SUBLANES * SUBLANES
                        acc = acc + w_ref[k:k + 1, cs] * ws[j:j + CONV_ROWS, :]
            y_ref[:, cs] = acc
        y = y_ref[...]
        mu = jnp.mean(y, axis=-1, keepdims=True)
        yc = y - mu
        var = jnp.mean(yc * yc, axis=-1, keepdims=True)
        z = yc * lax.rsqrt(var + EPS) * lg_ref[...] + lb_ref[...]
        o_ref[0, pl.ds(r0, CONV_ROWS), :] = (z * _sigmoid(z)).astype(BF16)
        return carry

    lax.fori_loop(0, l // CONV_ROWS, body, 0)


def _conv_branch(ug, w, b, lg, lb):
    bsz, l, c = ug.shape
    return pl.pallas_call(
        _conv_kernel,
        out_shape=jax.ShapeDtypeStruct((bsz, l, c), BF16),
        grid=(bsz,),
        in_specs=[pl.BlockSpec((1, l, c), lambda bi: (bi, 0, 0)),
                  _full_spec(w), _full_spec(b), _full_spec(lg), _full_spec(lb)],
        out_specs=pl.BlockSpec((1, l, c), lambda bi: (bi, 0, 0)),
        scratch_shapes=[pltpu.VMEM((l + 2 * CONV_PAD, c), F32), pltpu.VMEM((CONV_ROWS, c), F32)],
        compiler_params=_cparams("parallel"),
        name="conv_branch",
    )(ug, w, b, lg, lb)


def _outproj_kernel(*refs, n_in):
    x_ref = refs[0]
    a_refs = refs[1:1 + n_in]
    w_refs = refs[1 + n_in:1 + 2 * n_in]
    ga_ref, g2_ref, sh2_ref, sc2_ref, r_ref, xo_ref, h2_ref, lg_ref = refs[1 + 2 * n_in:]
    acc = _dot(a_refs[0][0], w_refs[0][...])
    for a_ref, w_ref in zip(a_refs[1:], w_refs[1:]):
        acc = acc + _dot(a_ref[0], w_ref[...])
    xn = x_ref[0] + ga_ref[0] * acc
    xo_ref[0] = xn
    h2 = _prenorm(xn, g2_ref[...], sh2_ref[0], sc2_ref[0]).astype(BF16)
    h2_ref[0] = h2
    lg_ref[0] = _dot(h2, r_ref[...])


def _outproj(x, acts, w, ga, g2, sh2, sc2, router, *, tm):
    b, l, d = x.shape
    n_in = len(acts)
    row = lambda bi, i: (bi, i, 0)
    in_specs = [pl.BlockSpec((1, tm, d), row)]
    in_specs += [pl.BlockSpec((1, tm, a.shape[2]), row) for a in acts]
    in_specs += [pl.BlockSpec((acts[r].shape[2], w.shape[1]), lambda bi, i, r=r: (r, 0)) for r in range(n_in)]
    in_specs += [_mod_spec(ga), _full_spec(g2), _mod_spec(sh2), _mod_spec(sc2), _full_spec(router)]
    return pl.pallas_call(
        functools.partial(_outproj_kernel, n_in=n_in),
        out_shape=(jax.ShapeDtypeStruct((b, l, d), F32),
                   jax.ShapeDtypeStruct((b, l, d), BF16),
                   jax.ShapeDtypeStruct((b, l, LANES), F32)),
        grid=(b, l // tm),
        in_specs=in_specs,
        out_specs=(pl.BlockSpec((1, tm, d), row), pl.BlockSpec((1, tm, d), row),
                   pl.BlockSpec((1, tm, LANES), row)),
        compiler_params=_cparams("parallel", "parallel"),
        name="outproj",
    )(x, *acts, *([w] * n_in), ga, g2, sh2, sc2, router)


MOE_TM = 256
MOE_CHUNKS = 8


def _moe_kernel(*refs, tiles_a, n_chunks):
    if tiles_a is None:
        xs_ref, gate_ref = refs[:2]
        xs2_ref = gate2_ref = None
        refs = refs[2:]
    else:
        xs_ref, gate_ref, xs2_ref, gate2_ref = refs[:4]
        refs = refs[4:]
    wg_ref, wu_ref, wd_ref, o_ref, wgb_ref, wub_ref, wdb_ref = refs
    p = pl.program_id(0)
    i = pl.program_id(1)
    n_experts = pl.num_programs(0) - 1
    cg = wg_ref.shape[1]
    cd = wd_ref.shape[1]

    @pl.when((p < n_experts) & (i < n_chunks))
    def _():
        slot = p % 2
        rg = pl.multiple_of(i * cg, cg)
        rd = pl.multiple_of(i * cd, cd)
        wgb_ref[slot, pl.ds(rg, cg), :] = wg_ref[0].astype(BF16)
        wub_ref[slot, pl.ds(rg, cg), :] = wu_ref[0].astype(BF16)
        wdb_ref[slot, pl.ds(rd, cd), :] = wd_ref[0].astype(BF16)

    @pl.when(p == 0)
    def _():
        o_ref[...] = jnp.zeros_like(o_ref)

    @pl.when(p > 0)
    def _():
        slot = (p + 1) % 2
        x = xs_ref[0]
        gate = gate_ref[0]
        if tiles_a is not None:
            x = jnp.where(i < tiles_a, x, xs2_ref[0])
            gate = jnp.where(i < tiles_a, gate, gate2_ref[0])
        g = _dot(x, wgb_ref[slot])
        u = _dot(x, wub_ref[slot])
        hid = (g * _sigmoid(g) * u).astype(BF16)
        o_ref[0] = (_dot(hid, wdb_ref[slot]) * gate).astype(BF16)


def _moe_ffn(xs, gates, wg, wu, wd, xs2=None, gates2=None):
    e, m, d = xs.shape
    f = wg.shape[2]
    tiles_a = m // MOE_TM
    n_chunks = min(MOE_CHUNKS, tiles_a)
    assert m % MOE_TM == 0 and d % (n_chunks * 16) == 0 and f % (n_chunks * 16) == 0
    last_e, last_c, last_a = e - 1, n_chunks - 1, tiles_a - 1
    rows = lambda p, i: (jnp.maximum(p - 1, 0), jnp.minimum(i, last_a), 0)
    chunk = lambda p, i: (jnp.minimum(p, last_e), jnp.minimum(i, last_c), 0)
    args = [xs, gates]
    in_specs = [pl.BlockSpec((1, MOE_TM, d), rows), pl.BlockSpec((1, MOE_TM, 1), rows)]
    if xs2 is not None:
        assert xs2.shape == (e, MOE_TM, d)
        rows2 = lambda p, i: (jnp.maximum(p - 1, 0), 0, 0)
        args += [xs2, gates2]
        in_specs += [pl.BlockSpec((1, MOE_TM, d), rows2), pl.BlockSpec((1, MOE_TM, 1), rows2)]
    tiles = tiles_a + (xs2 is not None)
    in_specs += [pl.BlockSpec((1, d // n_chunks, f), chunk),
                 pl.BlockSpec((1, d // n_chunks, f), chunk),
                 pl.BlockSpec((1, f // n_chunks, d), chunk)]
    return pl.pallas_call(
        functools.partial(_moe_kernel, tiles_a=tiles_a if xs2 is not None else None, n_chunks=n_chunks),
        out_shape=jax.ShapeDtypeStruct((e + 1, tiles * MOE_TM, d), BF16),
        grid=(e + 1, tiles),
        in_specs=in_specs,
        out_specs=pl.BlockSpec((1, MOE_TM, d), lambda p, i: (p, i, 0)),
        scratch_shapes=[pltpu.VMEM((2, d, f), BF16), pltpu.VMEM((2, d, f), BF16), pltpu.VMEM((2, f, d), BF16)],
        compiler_params=_cparams("arbitrary", "arbitrary"),
        name="moe_ffn",
    )(*args, wg, wu, wd)


COMBINE_TT = 256
COMBINE_W = 256
COMBINE_KW = 1024


def _combine_kernel(p0_ref, tok_ref, z_ref, x_ref, ga_ref, g_ref, *rest, final_norm):
    o_ref, acc_ref = rest[-2:]
    bi = pl.program_id(0)
    ti = pl.program_id(1)
    p0 = p0_ref[bi, ti]
    p1 = p0_ref[bi, ti + 1]
    n_pairs = z_ref.shape[1]
    kw = min(COMBINE_KW, n_pairs)
    tok0 = ti * COMBINE_TT

    def window(start, width):
        tok_row = tok0 + lax.broadcasted_iota(jnp.int32, (COMBINE_TT, width), 0)
        toks = tok_ref[0, :, pl.ds(start, width)]
        onehot = jnp.where(toks == tok_row, 1.0, 0.0).astype(BF16)
        return _dot(onehot, z_ref[0, pl.ds(start, width), :])

    first = jnp.minimum(p0 // COMBINE_W * COMBINE_W, n_pairs - kw)
    fits = p1 <= first + kw

    @pl.when(fits)
    def _():
        acc_ref[...] = window(pl.multiple_of(first, COMBINE_W), kw)

    @pl.when(jnp.logical_not(fits))
    def _():
        acc_ref[...] = jnp.zeros_like(acc_ref)

        def body(w, carry):
            acc_ref[...] += window(pl.multiple_of(w * COMBINE_W, COMBINE_W), COMBINE_W)
            return carry

        lax.fori_loop(p0 // COMBINE_W, (p1 + COMBINE_W - 1) // COMBINE_W, body, 0)

    x = x_ref[0] + ga_ref[0] * acc_ref[...]
    if final_norm:
        ms = jnp.mean(x * x, axis=-1, keepdims=True)
        x = x * lax.rsqrt(ms + EPS) * g_ref[...]
    o_ref[0] = x


def _combine(x, ga, g, tok_sorted, z, p0, *, final_norm, b0=0, prev=None):
    b, n, d = x.shape
    nb, p, _ = z.shape
    row = lambda bi, ti, *_: (b0 + bi, ti, 0)
    args = [p0, tok_sorted, z, x, ga, g]
    in_specs = [pl.BlockSpec((1, 1, p), lambda bi, ti, *_: (bi, 0, 0)),
                pl.BlockSpec((1, p, d), lambda bi, ti, *_: (bi, 0, 0)),
                pl.BlockSpec((1, COMBINE_TT, d), row), _mod_spec(ga, b0), _full_spec(g)]
    aliases = {}
    if prev is not None:
        aliases = {len(args): 0}
        args.append(prev)
        in_specs.append(pl.BlockSpec(memory_space=pl.ANY))
    grid_spec = pltpu.PrefetchScalarGridSpec(
        num_scalar_prefetch=1,
        grid=(nb, n // COMBINE_TT),
        in_specs=in_specs,
        out_specs=pl.BlockSpec((1, COMBINE_TT, d), row),
        scratch_shapes=[pltpu.VMEM((COMBINE_TT, d), F32)])
    return pl.pallas_call(
        functools.partial(_combine_kernel, final_norm=final_norm),
        out_shape=jax.ShapeDtypeStruct((b, n, d), F32),
        grid_spec=grid_spec,
        input_output_aliases=aliases,
        compiler_params=_cparams("parallel", "arbitrary"),
        name="moe_combine",
    )(*args)


def _route(h2, logits, b0=0, nb=None):
    _, n, d = h2.shape
    nb = h2.shape[0] if nb is None else nb
    cap = EC_FACTOR * n // N_EXPERTS
    aff = jax.nn.softmax(logits[b0:b0 + nb, :, :N_EXPERTS], axis=-1)
    gate, idx = lax.top_k(jnp.swapaxes(aff, 1, 2), cap)
    idx_e = jnp.swapaxes(idx, 0, 1)
    bidx = b0 + jnp.arange(nb, dtype=idx.dtype)[None, :, None]
    xs = h2[bidx, idx_e].reshape(N_EXPERTS, nb * cap, d)
    gates = jnp.swapaxes(gate, 0, 1).reshape(N_EXPERTS, nb * cap, 1)
    return xs, gates, idx


def _combine_routed(x, ga, g, idx, ys, row0, *, final_norm, b0=0, prev=None):
    _, n, d = x.shape
    b = idx.shape[0]
    cap = idx.shape[2]
    pairs = N_EXPERTS * cap
    m = ys.shape[1]
    slots = lax.broadcasted_iota(jnp.int32, (b, pairs), 1)
    tok_sorted, order = lax.sort((idx.reshape(b, pairs), slots), dimension=1, num_keys=1)
    flat = (order // cap + 1) * m + row0 + jnp.arange(b, dtype=jnp.int32)[:, None] * cap + order % cap
    z = ys.reshape((N_EXPERTS + 1) * m, d)[flat]
    bounds = jnp.arange(n // COMBINE_TT + 1, dtype=jnp.int32) * COMBINE_TT
    p0 = jnp.sum(tok_sorted[:, None, :] < bounds[None, :, None], axis=-1).astype(jnp.int32)
    return _combine(x, ga, g, tok_sorted.reshape(b, 1, pairs), z, p0, final_norm=final_norm, b0=b0, prev=prev)


def _ec_moe(x, ga, g, h2, logits, wg, wu, wd, *, final_norm, extra=None):
    nb = x.shape[0] // 2
    routed = [_route(h2, logits, b0, nb) for b0 in (0, nb)]
    xs2 = gates2 = None
    if extra is not None:
        ex, ega, eh2, elogits = extra
        xs2, gates2, eidx = _route(eh2, elogits)
    ys = [_moe_ffn(routed[0][0], routed[0][1], wg, wu, wd, xs2, gates2),
          _moe_ffn(routed[1][0], routed[1][1], wg, wu, wd)]
    out = _combine_routed(x, ga, g, routed[0][2], ys[0], 0, final_norm=final_norm)
    out = _combine_routed(x, ga, g, routed[1][2], ys[1], 0, final_norm=final_norm, b0=nb, prev=out)
    if extra is None:
        return out
    return out, _combine_routed(ex, ega, g, eidx, ys[0], routed[0][0].shape[1], final_norm=final_norm)


def _rope_quarter(y, cos, sin, first):
    rot = jnp.where(first, pltpu.roll(y, LANES - ROPE_DIM // 2, 1), pltpu.roll(y, ROPE_DIM // 2, 1))
    return y * cos + rot * sin


def _l1_dqkv_kernel(x_ref, g_ref, sh_ref, sc_ref, w_ref, qg_ref, kvg_ref, cos_ref, sin_ref, *out_refs,
                    with_q, rope):
    if with_q:
        cq_ref, ckv_ref, kpe_ref = out_refs
    else:
        ckv_ref, kpe_ref = out_refs
    h = _prenorm(x_ref[0], g_ref[...], sh_ref[0], sc_ref[0]).astype(BF16)
    if with_q:
        cq = _dot(h, w_ref[:, 0:Q_LORA])
        ms = jnp.mean(cq * cq, axis=-1, keepdims=True)
        cq_ref[0] = (cq * lax.rsqrt(ms + EPS) * qg_ref[...]).astype(BF16)
    z = _dot(h, w_ref[:, Q_LORA:Q_LORA + KV_LORA + LANES])
    ckv = z[:, 0:KV_LORA]
    ms = jnp.mean(ckv * ckv, axis=-1, keepdims=True)
    ckv_ref[0] = (ckv * lax.rsqrt(ms + EPS) * kvg_ref[...]).astype(BF16)
    pe = z[:, KV_LORA:KV_LORA + LANES]
    if rope:
        lane = lax.broadcasted_iota(jnp.int32, pe.shape, 1)
        pe = _rope_quarter(pe, cos_ref[...], sin_ref[...], (lane % ROPE_DIM) < ROPE_DIM // 2)
    kpe_ref[0] = (pe + pltpu.roll(pe, ROPE_DIM, 1)).astype(BF16)


def _l1_dqkv(x, g, sh, sc, w, qg, kvg, cos, sin, *, rope, with_q, tm):
    b, l, d = x.shape
    row = lambda bi, i: (bi, i, 0)
    args = [x, g, sh, sc, w, qg, kvg, cos, sin]
    in_specs = [pl.BlockSpec((1, tm, d), row), _full_spec(g), _mod_spec(sh), _mod_spec(sc), _full_spec(w),
                _full_spec(qg), _full_spec(kvg),
                pl.BlockSpec((tm, LANES), lambda bi, i: (i, 0)), pl.BlockSpec((tm, LANES), lambda bi, i: (i, 0))]
    out_shape, out_specs = [], []
    if with_q:
        out_shape.append(jax.ShapeDtypeStruct((b, l, Q_LORA), BF16))
        out_specs.append(pl.BlockSpec((1, tm, Q_LORA), row))
    out_shape += [jax.ShapeDtypeStruct((b, l, KV_LORA), BF16), jax.ShapeDtypeStruct((b, l, LANES), BF16)]
    out_specs += [pl.BlockSpec((1, tm, KV_LORA), row), pl.BlockSpec((1, tm, LANES), row)]

    return pl.pallas_call(
        functools.partial(_l1_dqkv_kernel, with_q=with_q, rope=rope),
        out_shape=tuple(out_shape),
        grid=(b, l // tm),
        in_specs=in_specs,
        out_specs=tuple(out_specs),
        compiler_params=_cparams("parallel", "parallel"),
        name="l1_dqkv",
    )(*args)


def _l1_q_kernel(cq_ref, w_ref, cos_ref, sin_ref, qn_ref, qp_ref):
    a = cq_ref[0]
    cw = 4 * LANES
    n_nope = MLA_HEADS * NOPE_DIM
    qscale = MLA_SCALE * LOG2E
    for c in range(n_nope // cw):
        qn_ref[0, :, c * cw:(c + 1) * cw] = (_dot(a, w_ref[:, c * cw:(c + 1) * cw]) * qscale).astype(BF16)
    cos = cos_ref[...]
    sin = sin_ref[...]
    lane = lax.broadcasted_iota(jnp.int32, cos.shape, 1)
    first = (lane % ROPE_DIM) < ROPE_DIM // 2
    for c in range(MLA_HEADS * ROPE_DIM // cw):
        y = _dot(a, w_ref[:, n_nope + c * cw:n_nope + (c + 1) * cw])
        parts = [_rope_quarter(y[:, j * LANES:(j + 1) * LANES], cos, sin, first) * qscale for j in range(4)]
        qp_ref[0, :, c * cw:(c + 1) * cw] = jnp.concatenate(parts, axis=1).astype(BF16)


def _l1_q(cq, w, cos, sin, *, tm):
    b, l, k = cq.shape
    row = lambda bi, i: (bi, i, 0)
    return pl.pallas_call(
        _l1_q_kernel,
        out_shape=(jax.ShapeDtypeStruct((b, l, MLA_HEADS * NOPE_DIM), BF16),
                   jax.ShapeDtypeStruct((b, l, MLA_HEADS * ROPE_DIM), BF16)),
        grid=(b, l // tm),
        in_specs=[pl.BlockSpec((1, tm, k), row), _full_spec(w),
                  pl.BlockSpec((tm, LANES), lambda bi, i: (i, 0)), pl.BlockSpec((tm, LANES), lambda bi, i: (i, 0))],
        out_specs=(pl.BlockSpec((1, tm, MLA_HEADS * NOPE_DIM), row),
                   pl.BlockSpec((1, tm, MLA_HEADS * ROPE_DIM), row)),
        compiler_params=_cparams("parallel", "parallel"),
        name="l1_q",
    )(cq, w, cos, sin)


def _mm_kernel(a_ref, w_ref, o_ref):
    a = a_ref[0]
    cw = 4 * LANES
    for c in range(w_ref.shape[1] // cw):
        o_ref[0, :, c * cw:(c + 1) * cw] = _dot(a, w_ref[:, c * cw:(c + 1) * cw]).astype(o_ref.dtype)


def _mm(a, w, *, tm):
    b, l, k = a.shape
    n = w.shape[1]
    row = lambda bi, i: (bi, i, 0)
    return pl.pallas_call(
        _mm_kernel,
        out_shape=jax.ShapeDtypeStruct((b, l, n), BF16),
        grid=(b, l // tm),
        in_specs=[pl.BlockSpec((1, tm, k), row), _full_spec(w)],
        out_specs=pl.BlockSpec((1, tm, n), row),
        compiler_params=_cparams("parallel", "parallel"),
        name="l1_kv_up",
    )(a, w)


def _rope_tables(n_tokens, d_rot):
    rows = n_tokens // GRID_W
    row = jnp.repeat(jnp.arange(rows, dtype=F32), GRID_W)
    col = jnp.tile(jnp.arange(GRID_W, dtype=F32), rows)
    n_axis = d_rot // 4
    inv_freq = ROPE_THETA ** (-jnp.arange(n_axis, dtype=F32) / n_axis)
    ang = jnp.concatenate([row[:, None] * inv_freq, col[:, None] * inv_freq], axis=-1)
    cos, sin = jnp.cos(ang), jnp.sin(ang)
    cos_t = jnp.concatenate([cos, cos], axis=-1)
    sin_t = jnp.concatenate([-sin, sin], axis=-1)
    reps = LANES // d_rot
    return jnp.tile(cos_t, (1, reps)), jnp.tile(sin_t, (1, reps))


def _mods(mod, lo, hi):
    return [mod[lo:hi, None, i * D_MODEL:(i + 1) * D_MODEL] for i in range(6)]


def _pad_router(router):
    return jnp.pad(router, ((0, 0), (0, LANES - router.shape[1]))).astype(BF16)


def kernel(x, c, ctx, c_ctx, l0_mod_w, l0_mod_b, l0_norm1_g, l0_w_in, l0_q_norm_g, l0_k_norm_g, l0_dw_w, l0_dw_b, l0_conv_ln_g, l0_conv_ln_b, l0_w_out, l0_norm2_g, l0_router, l0_w_gate, l0_w_up, l0_w_down, l1_mod_w, l1_mod_b, l1_norm1_g, l1_w_dqkv, l1_q_lora_norm_g, l1_w_uq, l1_kv_lora_norm_g, l1_w_ukv, l1_w_out, l1_norm2_g, l1_router, l1_w_gate, l1_w_up, l1_w_down, final_norm_g):
    bsz, seq, d = x.shape
    n_ctx = ctx.shape[1]
    row2 = lambda v: v.reshape(1, -1)

    cond = jnp.zeros((16, d), F32).at[:bsz].set(c).at[bsz].set(c_ctx)
    mod0 = _ada(cond, l0_mod_w, l0_mod_b)
    mod1 = _ada(cond, l1_mod_w, l1_mod_b)

    sh1, sc1, ga1, sh2, sc2, ga2 = _mods(mod0, 0, bsz)
    csh1, csc1, cga1, csh2, csc2, cga2 = _mods(mod0, bsz, bsz + 1)
    s_q, s_k, s_v, s_u = ATTN_WIDTH, ATTN_WIDTH + KV_WIDTH, ATTN_WIDTH + 2 * KV_WIDTH, ATTN_WIDTH + 2 * KV_WIDTH + CONV_WIDTH
    w_in = jnp.concatenate([l0_w_in[:, :s_q], l0_w_in[:, s_v:s_u], l0_w_in[:, s_u:],
                            l0_w_in[:, s_q:s_k], l0_w_in[:, s_k:s_v]], axis=1).astype(BF16)
    cos0, sin0 = _rope_tables(seq, HEAD_DIM)
    g1 = row2(l0_norm1_g)
    qg, kg = row2(l0_q_norm_g), row2(l0_k_norm_g)
    q_l, ug_l, kv_l = _l0_inproj(x, g1, sh1, sc1, w_in, qg, kg, cos0, sin0, rope=True, tm=512)
    q_c, ug_c, kv_c = _l0_inproj(ctx, g1, csh1, csc1, w_in, qg, kg, cos0, sin0, rope=False, tm=n_ctx)
    kv_all = jnp.concatenate([kv_c, kv_l], axis=1)
    a_l = _gqa_attention(q_l, kv_all, tq=512)
    dw_w = l0_dw_w.reshape(CONV_K, CONV_WIDTH)
    dw_b, ln_g, ln_b = row2(l0_dw_b), row2(l0_conv_ln_g), row2(l0_conv_ln_b)
    cb_l = _conv_branch(ug_l, dw_w, dw_b, ln_g, ln_b)
    w_out0 = l0_w_out.astype(BF16)
    g2 = row2(l0_norm2_g)
    router0 = _pad_router(l0_router)
    x1, h2_l, lg_l = _outproj(x, [a_l, cb_l], w_out0, ga1, g2, sh2, sc2, router0, tm=256)
    a_c = _gqa_attention(q_c, kv_c, tq=n_ctx)
    cb_c = _conv_branch(ug_c, dw_w, dw_b, ln_g, ln_b)
    xc1, h2_c, lg_c = _outproj(ctx, [a_c, cb_c], w_out0, cga1, g2, csh2, csc2, router0, tm=n_ctx)
    x2, xc2 = _ec_moe(x1, ga2, g2, h2_l, lg_l, l0_w_gate, l0_w_up, l0_w_down, final_norm=False,
                      extra=(xc1, cga2, h2_c, lg_c))

    sh1, sc1, ga1, sh2, sc2, ga2_1 = _mods(mod1, 0, bsz)
    csh1, csc1 = _mods(mod1, bsz, bsz + 1)[:2]
    w_dqkv = jnp.pad(l1_w_dqkv, ((0, 0), (0, LANES - ROPE_DIM))).astype(BF16)
    cos1, sin1 = _rope_tables(seq, ROPE_DIM)
    g1 = row2(l1_norm1_g)
    qlg, kvlg = row2(l1_q_lora_norm_g), row2(l1_kv_lora_norm_g)
    cq, ckv_l, kpe_l = _l1_dqkv(x2, g1, sh1, sc1, w_dqkv, qlg, kvlg, cos1, sin1, rope=True, with_q=True, tm=512)
    ckv_c, kpe_c = _l1_dqkv(xc2, g1, csh1, csc1, w_dqkv, qlg, kvlg, cos1, sin1, rope=False, with_q=False, tm=n_ctx)
    w_uq = l1_w_uq.reshape(Q_LORA, MLA_HEADS, QK_DIM)
    w_uq = jnp.concatenate([w_uq[:, :, :NOPE_DIM].reshape(Q_LORA, -1),
                            w_uq[:, :, NOPE_DIM:].reshape(Q_LORA, -1)], axis=1).astype(BF16)
    w_ukv = l1_w_ukv.reshape(KV_LORA, MLA_HEADS, NOPE_DIM + V_DIM)
    w_ukv = jnp.concatenate([w_ukv[:, :, :NOPE_DIM].reshape(KV_LORA, -1),
                             w_ukv[:, :, NOPE_DIM:].reshape(KV_LORA, -1)], axis=1).astype(BF16)
    qn, qp = _l1_q(cq, w_uq, cos1, sin1, tm=512)
    kv1 = _mm(jnp.concatenate([ckv_c, ckv_l], axis=1), w_ukv, tm=256)
    kpe = jnp.concatenate([kpe_c, kpe_l], axis=1)
    a1 = _mla_attention(qn, qp, kv1, kpe, tq=512)
    x3, h2, lg = _outproj(x2, [a1], l1_w_out.astype(BF16), ga1, row2(l1_norm2_g), sh2, sc2,
                          _pad_router(l1_router), tm=256)
    return _ec_moe(x3, ga2_1, row2(final_norm_g), h2, lg, l1_w_gate, l1_w_up, l1_w_down, final_norm=True)
```

```python
import functools

import jax
import jax.numpy as jnp
from jax import lax
from jax.experimental import pallas as pl
from jax.experimental.pallas import tpu as pltpu

F32 = jnp.float32
BF16 = jnp.bfloat16

D_MODEL = 2048
GRID_W = 64
EPS = 1e-6
ROPE_THETA = 10000.0
HEAD_DIM = 128
N_Q_HEADS = 8
N_KV_HEADS = 2
ATTN_WIDTH = N_Q_HEADS * HEAD_DIM
KV_WIDTH = N_KV_HEADS * HEAD_DIM
CONV_WIDTH = D_MODEL - ATTN_WIDTH
CONV_K = 31
LOG2E = 1.4426950408889634
GQA_SCALE = HEAD_DIM ** -0.5
MLA_HEADS = 16
Q_LORA = 1536
KV_LORA = 512
NOPE_DIM = 128
ROPE_DIM = 64
V_DIM = 128
QK_DIM = NOPE_DIM + ROPE_DIM
MLA_SCALE = QK_DIM ** -0.5
N_EXPERTS = 16
D_EXPERT = 1024
EC_FACTOR = 2

LANES = 128
SUBLANES = 8
VMEM_LIMIT = 56 * 1024 * 1024
CONV_PAD = 16


def _cparams(*sem):
    return pltpu.CompilerParams(dimension_semantics=sem, vmem_limit_bytes=VMEM_LIMIT)


def _dot(a, b):
    return jnp.dot(a, b, preferred_element_type=F32)


def _sigmoid(x):
    return 1.0 / (1.0 + jnp.exp(-x))


def _prenorm(x, g, sh, sc):
    ms = jnp.mean(x * x, axis=-1, keepdims=True)
    y = x * lax.rsqrt(ms + EPS) * g
    return y * (1.0 + sc) + sh


def _mod_spec(arr, b0=0):
    if arr.shape[0] > 1:
        return pl.BlockSpec((1, 1, arr.shape[2]), lambda b, *_: (b0 + b, 0, 0))
    return pl.BlockSpec((1, 1, arr.shape[2]), lambda b, *_: (0, 0, 0))


def _full_spec(arr):
    nd = arr.ndim
    return pl.BlockSpec(arr.shape, lambda *_: (0,) * nd)


def _ada_kernel(c_ref, w_ref, b_ref, o_ref):
    c = c_ref[...]
    s = c * _sigmoid(c)
    s_hi = s.astype(BF16)
    s_lo = (s - s_hi.astype(F32)).astype(BF16)
    w = w_ref[...]
    w_hi = w.astype(BF16)
    w_lo = (w - w_hi.astype(F32)).astype(BF16)
    o_ref[...] = _dot(s_hi, w_hi) + _dot(s_lo, w_hi) + _dot(s_hi, w_lo) + b_ref[...]


def _ada(cond, w, b):
    m, d = cond.shape
    n = w.shape[1]
    tn = 512
    return pl.pallas_call(
        _ada_kernel,
        out_shape=jax.ShapeDtypeStruct((m, n), F32),
        grid=(n // tn,),
        in_specs=[pl.BlockSpec((m, d), lambda j: (0, 0)),
                  pl.BlockSpec((d, tn), lambda j: (0, j)),
                  pl.BlockSpec((1, tn), lambda j: (0, j))],
        out_specs=pl.BlockSpec((m, tn), lambda j: (0, j)),
        compiler_params=_cparams("parallel"),
        name="ada_params",
    )(cond, w, b.reshape(1, n))


def _rope_half(y, cos, sin):
    return y * cos + pltpu.roll(y, HEAD_DIM // 2, 1) * sin


def _l0_inproj_kernel(x_ref, g_ref, sh_ref, sc_ref, w_ref, qg_ref, kg_ref, cos_ref, sin_ref,
                      q_ref, ug_ref, kv_ref, *, rope):
    h = _prenorm(x_ref[0], g_ref[...], sh_ref[0], sc_ref[0]).astype(BF16)
    cos = cos_ref[...]
    sin = sin_ref[...]

    def head(y, gain, scale):
        ms = jnp.mean(y * y, axis=-1, keepdims=True)
        y = y * lax.rsqrt(ms + EPS) * gain
        if rope:
            y = _rope_half(y, cos, sin)
        return y * scale if scale != 1.0 else y

    cw = 4 * HEAD_DIM
    for c in range(ATTN_WIDTH // cw):
        y = _dot(h, w_ref[:, c * cw:(c + 1) * cw])
        parts = [head(y[:, j * HEAD_DIM:(j + 1) * HEAD_DIM], qg_ref[...], GQA_SCALE * LOG2E) for j in range(4)]
        q_ref[0, :, c * cw:(c + 1) * cw] = jnp.concatenate(parts, axis=1).astype(BF16)
    u0 = ATTN_WIDTH
    g0 = ATTN_WIDTH + CONV_WIDTH
    for c in range(CONV_WIDTH // cw):
        u = _dot(h, w_ref[:, u0 + c * cw:u0 + (c + 1) * cw])
        gt = _dot(h, w_ref[:, g0 + c * cw:g0 + (c + 1) * cw])
        ug_ref[0, :, c * cw:(c + 1) * cw] = (u * _sigmoid(gt)).astype(BF16)
    k0 = ATTN_WIDTH + 2 * CONV_WIDTH
    y = _dot(h, w_ref[:, k0:k0 + 2 * KV_WIDTH])
    parts = [head(y[:, j * HEAD_DIM:(j + 1) * HEAD_DIM], kg_ref[...], 1.0) for j in range(N_KV_HEADS)]
    parts.append(y[:, KV_WIDTH:])
    kv_ref[0] = jnp.concatenate(parts, axis=1).astype(BF16)


def _l0_inproj(x, g, sh, sc, w, qg, kg, cos, sin, *, rope, tm):
    b, l, d = x.shape
    n = w.shape[1]
    row = lambda bi, i: (bi, i, 0)
    return pl.pallas_call(
        functools.partial(_l0_inproj_kernel, rope=rope),
        out_shape=(jax.ShapeDtypeStruct((b, l, ATTN_WIDTH), BF16),
                   jax.ShapeDtypeStruct((b, l, CONV_WIDTH), BF16),
                   jax.ShapeDtypeStruct((b, l, 2 * KV_WIDTH), BF16)),
        grid=(b, l // tm),
        in_specs=[pl.BlockSpec((1, tm, d), row), _full_spec(g), _mod_spec(sh), _mod_spec(sc),
                  _full_spec(w), _full_spec(qg), _full_spec(kg),
                  pl.BlockSpec((tm, LANES), lambda bi, i: (i, 0)),
                  pl.BlockSpec((tm, LANES), lambda bi, i: (i, 0))],
        out_specs=(pl.BlockSpec((1, tm, ATTN_WIDTH), row),
                   pl.BlockSpec((1, tm, CONV_WIDTH), row),
                   pl.BlockSpec((1, tm, 2 * KV_WIDTH), row)),
        compiler_params=_cparams("parallel", "parallel"),
        name="l0_inproj",
    )(x, g, sh, sc, w, qg, kg, cos, sin)


ATT_ROWS = 256


def _scores(q, k):
    return lax.dot_general(q, k, (((1,), (1,)), ((), ())), preferred_element_type=F32)


def _softmax_pv(s, v_ones):
    dv = v_ones.shape[1] // 2
    m = jnp.max(s, axis=-1, keepdims=True)
    o = _dot(jnp.exp2(s - m).astype(BF16), v_ones)
    return o[:, :dv] / o[:, dv:]


def _attend_units(n_units, q_of, k_of, v_of, store):
    s = _scores(q_of(0), k_of(0))
    for n in range(n_units):
        s_cur = s
        if n + 1 < n_units:
            s = _scores(q_of(n + 1), k_of(n + 1))
        store(n, _softmax_pv(s_cur, v_of(n)).astype(BF16))


def _gqa_kernel(q_ref, k_ref, v_ref, o_ref, vones_ref):
    @pl.when(pl.program_id(2) == 0)
    def _():
        vones_ref[:, 0:HEAD_DIM] = v_ref[0]
        vones_ref[:, HEAD_DIM:2 * HEAD_DIM] = jnp.ones(v_ref.shape[1:], BF16)

    k = k_ref[0]
    group = N_Q_HEADS // N_KV_HEADS
    n_units = q_ref.shape[1] // ATT_ROWS * group

    def where(n):
        r, j = divmod(n, group)
        return slice(r * ATT_ROWS, (r + 1) * ATT_ROWS), slice(j * HEAD_DIM, (j + 1) * HEAD_DIM)

    def store(n, o):
        rows, cols = where(n)
        o_ref[0, rows, cols] = o

    def q_of(n):
        rows, cols = where(n)
        return q_ref[0, rows, cols]

    _attend_units(n_units, q_of, lambda n: k, lambda n: vones_ref[...], store)


def _gqa_attention(q, kv, *, tq):
    b, l, _ = q.shape
    lk = kv.shape[1]
    gw = ATTN_WIDTH // N_KV_HEADS
    return pl.pallas_call(
        _gqa_kernel,
        out_shape=jax.ShapeDtypeStruct((b, l, ATTN_WIDTH), BF16),
        grid=(b, N_KV_HEADS, l // tq),
        in_specs=[pl.BlockSpec((1, tq, gw), lambda bi, h, i: (bi, i, h)),
                  pl.BlockSpec((1, lk, HEAD_DIM), lambda bi, h, i: (bi, 0, h)),
                  pl.BlockSpec((1, lk, HEAD_DIM), lambda bi, h, i: (bi, 0, N_KV_HEADS + h))],
        out_specs=pl.BlockSpec((1, tq, gw), lambda bi, h, i: (bi, i, h)),
        scratch_shapes=[pltpu.VMEM((lk, 2 * HEAD_DIM), BF16)],
        compiler_params=_cparams("parallel", "parallel", "arbitrary"),
        name="gqa_attention",
    )(q, kv, kv)


MLA_HB = 4


def _mla_kernel(qn_ref, qp_ref, k_ref, v_ref, kpe_ref, o_ref, kcat_ref, vones_ref):
    @pl.when(pl.program_id(2) == 0)
    def _():
        for i in range(MLA_HB):
            kcat_ref[i, :, 0:NOPE_DIM] = k_ref[0, :, i * NOPE_DIM:(i + 1) * NOPE_DIM]
            kcat_ref[i, :, NOPE_DIM:2 * NOPE_DIM] = kpe_ref[0]
            vones_ref[i, :, 0:V_DIM] = v_ref[0, :, i * V_DIM:(i + 1) * V_DIM]
            vones_ref[i, :, V_DIM:2 * V_DIM] = jnp.ones((v_ref.shape[1], V_DIM), BF16)

    lane = lax.broadcasted_iota(jnp.int32, (ATT_ROWS, LANES), 1)
    per_block = LANES // ROPE_DIM
    n_units = qn_ref.shape[1] // ATT_ROWS * MLA_HB

    def q_of(n):
        r, i = divmod(n, MLA_HB)
        rows = slice(r * ATT_ROWS, (r + 1) * ATT_ROWS)
        blk, pos = divmod(i, per_block)
        qp = qp_ref[0, rows, blk * LANES:(blk + 1) * LANES]
        mine = (lane >= pos * ROPE_DIM) & (lane < (pos + 1) * ROPE_DIM)
        return jnp.concatenate(
            [qn_ref[0, rows, i * NOPE_DIM:(i + 1) * NOPE_DIM], jnp.where(mine, qp, jnp.zeros_like(qp))], axis=1)

    def store(n, o):
        r, i = divmod(n, MLA_HB)
        o_ref[0, r * ATT_ROWS:(r + 1) * ATT_ROWS, i * V_DIM:(i + 1) * V_DIM] = o

    _attend_units(n_units, q_of, lambda n: kcat_ref[n % MLA_HB], lambda n: vones_ref[n % MLA_HB], store)


def _mla_attention(qn, qp, kv, kpe, *, tq):
    b, l, _ = qn.shape
    lk = kv.shape[1]
    wn = MLA_HB * NOPE_DIM
    nv0 = MLA_HEADS * NOPE_DIM // wn
    return pl.pallas_call(
        _mla_kernel,
        out_shape=jax.ShapeDtypeStruct((b, l, MLA_HEADS * V_DIM), BF16),
        grid=(b, MLA_HEADS // MLA_HB, l // tq),
        in_specs=[pl.BlockSpec((1, tq, wn), lambda bi, h, i: (bi, i, h)),
                  pl.BlockSpec((1, tq, MLA_HB * ROPE_DIM), lambda bi, h, i: (bi, i, h)),
                  pl.BlockSpec((1, lk, wn), lambda bi, h, i: (bi, 0, h)),
                  pl.BlockSpec((1, lk, wn), lambda bi, h, i: (bi, 0, nv0 + h)),
                  pl.BlockSpec((1, lk, LANES), lambda bi, h, i: (bi, 0, 0))],
        out_specs=pl.BlockSpec((1, tq, wn), lambda bi, h, i: (bi, i, h)),
        scratch_shapes=[pltpu.VMEM((MLA_HB, lk, 2 * NOPE_DIM), BF16), pltpu.VMEM((MLA_HB, lk, 2 * V_DIM), BF16)],
        compiler_params=_cparams("parallel", "parallel", "arbitrary"),
        name="mla_attention",
    )(qn, qp, kv, kv, kpe)


CONV_ROWS = 64
CONV_COLS = 128


def _conv_kernel(ug_ref, w_ref, b_ref, lg_ref, lb_ref, o_ref, pad_ref, y_ref):
    l, c = ug_ref.shape[1], ug_ref.shape[2]
    pad_ref[0:CONV_PAD, :] = jnp.zeros((CONV_PAD, c), F32)
    pad_ref[CONV_PAD + l:2 * CONV_PAD + l, :] = jnp.zeros((CONV_PAD, c), F32)
    pad_ref[CONV_PAD:CONV_PAD + l, :] = ug_ref[0].astype(F32)
    off = CONV_PAD - CONV_K // 2
    nwin = CONV_ROWS + 2 * CONV_PAD

    def body(r, carry):
        r0 = pl.multiple_of(r * CONV_ROWS, CONV_ROWS)
        for cb in range(c // CONV_COLS):
            cs = slice(cb * CONV_COLS, (cb + 1) * CONV_COLS)
            win = pad_ref[pl.ds(r0, nwin), cs]
            acc = jnp.zeros((CONV_ROWS, CONV_COLS), F32) + b_ref[:, cs]
            for s in range(SUBLANES):
                ws = win if s == 0 else pltpu.roll(win, nwin - s, 0)
                for k in range(CONV_K):
                    if (off + k) % SUBLANES == s:
                        j = (off + k) // SUBLANES * SUBLANES
                        acc = acc + w_ref[k:k + 1, cs] * ws[j:j + CONV_ROWS, :]
            y_ref[:, cs] = acc
        y = y_ref[...]
        mu = jnp.mean(y, axis=-1, keepdims=True)
        yc = y - mu
        var = jnp.mean(yc * yc, axis=-1, keepdims=True)
        z = yc * lax.rsqrt(var + EPS) * lg_ref[...] + lb_ref[...]
        o_ref[0, pl.ds(r0, CONV_ROWS), :] = (z * _sigmoid(z)).astype(BF16)
        return carry

    lax.fori_loop(0, l // CONV_ROWS, body, 0)


def _conv_branch(ug, w, b, lg, lb):
    bsz, l, c = ug.shape
    return pl.pallas_call(
        _conv_kernel,
        out_shape=jax.ShapeDtypeStruct((bsz, l, c), BF16),
        grid=(bsz,),
        in_specs=[pl.BlockSpec((1, l, c), lambda bi: (bi, 0, 0)),
                  _full_spec(w), _full_spec(b), _full_spec(lg), _full_spec(lb)],
        out_specs=pl.BlockSpec((1, l, c), lambda bi: (bi, 0, 0)),
        scratch_shapes=[pltpu.VMEM((l + 2 * CONV_PAD, c), F32), pltpu.VMEM((CONV_ROWS, c), F32)],
        compiler_params=_cparams("parallel"),
        name="conv_branch",
    )(ug, w, b, lg, lb)


def _outproj_kernel(*refs, n_in):
    x_ref = refs[0]
    a_refs = refs[1:1 + n_in]
    w_refs = refs[1 + n_in:1 + 2 * n_in]
    ga_ref, g2_ref, sh2_ref, sc2_ref, r_ref, xo_ref, h2_ref, lg_ref = refs[1 + 2 * n_in:]
    acc = _dot(a_refs[0][0], w_refs[0][...])
    for a_ref, w_ref in zip(a_refs[1:], w_refs[1:]):
        acc = acc + _dot(a_ref[0], w_ref[...])
    xn = x_ref[0] + ga_ref[0] * acc
    xo_ref[0] = xn
    h2 = _prenorm(xn, g2_ref[...], sh2_ref[0], sc2_ref[0]).astype(BF16)
    h2_ref[0] = h2
    lg_ref[0] = _dot(h2, r_ref[...])


def _outproj(x, acts, w, ga, g2, sh2, sc2, router, *, tm):
    b, l, d = x.shape
    n_in = len(acts)
    row = lambda bi, i: (bi, i, 0)
    in_specs = [pl.BlockSpec((1, tm, d), row)]
    in_specs += [pl.BlockSpec((1, tm, a.shape[2]), row) for a in acts]
    in_specs += [pl.BlockSpec((acts[r].shape[2], w.shape[1]), lambda bi, i, r=r: (r, 0)) for r in range(n_in)]
    in_specs += [_mod_spec(ga), _full_spec(g2), _mod_spec(sh2), _mod_spec(sc2), _full_spec(router)]
    return pl.pallas_call(
        functools.partial(_outproj_kernel, n_in=n_in),
        out_shape=(jax.ShapeDtypeStruct((b, l, d), F32),
                   jax.ShapeDtypeStruct((b, l, d), BF16),
                   jax.ShapeDtypeStruct((b, l, LANES), F32)),
        grid=(b, l // tm),
        in_specs=in_specs,
        out_specs=(pl.BlockSpec((1, tm, d), row), pl.BlockSpec((1, tm, d), row),
                   pl.BlockSpec((1, tm, LANES), row)),
        compiler_params=_cparams("parallel", "parallel"),
        name="outproj",
    )(x, *acts, *([w] * n_in), ga, g2, sh2, sc2, router)


MOE_TM = 256
MOE_CHUNKS = 8


def _moe_kernel(*refs, tiles_a, n_chunks):
    if tiles_a is None:
        xs_ref, gate_ref = refs[:2]
        xs2_ref = gate2_ref = None
        refs = refs[2:]
    else:
        xs_ref, gate_ref, xs2_ref, gate2_ref = refs[:4]
        refs = refs[4:]
    wg_ref, wu_ref, wd_ref, o_ref, wgb_ref, wub_ref, wdb_ref = refs
    p = pl.program_id(0)
    i = pl.program_id(1)
    n_experts = pl.num_programs(0) - 1
    cg = wg_ref.shape[1]
    cd = wd_ref.shape[1]

    @pl.when((p < n_experts) & (i < n_chunks))
    def _():
        slot = p % 2
        rg = pl.multiple_of(i * cg, cg)
        rd = pl.multiple_of(i * cd, cd)
        wgb_ref[slot, pl.ds(rg, cg), :] = wg_ref[0].astype(BF16)
        wub_ref[slot, pl.ds(rg, cg), :] = wu_ref[0].astype(BF16)
        wdb_ref[slot, pl.ds(rd, cd), :] = wd_ref[0].astype(BF16)

    @pl.when(p == 0)
    def _():
        o_ref[...] = jnp.zeros_like(o_ref)

    @pl.when(p > 0)
    def _():
        slot = (p + 1) % 2
        x = xs_ref[0]
        gate = gate_ref[0]
        if tiles_a is not None:
            x = jnp.where(i < tiles_a, x, xs2_ref[0])
            gate = jnp.where(i < tiles_a, gate, gate2_ref[0])
        g = _dot(x, wgb_ref[slot])
        u = _dot(x, wub_ref[slot])
        hid = (g * _sigmoid(g) * u).astype(BF16)
        o_ref[0] = (_dot(hid, wdb_ref[slot]) * gate).astype(BF16)


def _moe_ffn(xs, gates, wg, wu, wd, xs2=None, gates2=None):
    e, m, d = xs.shape
    f = wg.shape[2]
    tiles_a = m // MOE_TM
    n_chunks = min(MOE_CHUNKS, tiles_a)
    assert m % MOE_TM == 0 and d % (n_chunks * 16) == 0 and f % (n_chunks * 16) == 0
    last_e, last_c, last_a = e - 1, n_chunks - 1, tiles_a - 1
    rows = lambda p, i: (jnp.maximum(p - 1, 0), jnp.minimum(i, last_a), 0)
    chunk = lambda p, i: (jnp.minimum(p, last_e), jnp.minimum(i, last_c), 0)
    args = [xs, gates]
    in_specs = [pl.BlockSpec((1, MOE_TM, d), rows), pl.BlockSpec((1, MOE_TM, 1), rows)]
    if xs2 is not None:
        assert xs2.shape == (e, MOE_TM, d)
        rows2 = lambda p, i: (jnp.maximum(p - 1, 0), 0, 0)
        args += [xs2, gates2]
        in_specs += [pl.BlockSpec((1, MOE_TM, d), rows2), pl.BlockSpec((1, MOE_TM, 1), rows2)]
    tiles = tiles_a + (xs2 is not None)
    in_specs += [pl.BlockSpec((1, d // n_chunks, f), chunk),
                 pl.BlockSpec((1, d // n_chunks, f), chunk),
                 pl.BlockSpec((1, f // n_chunks, d), chunk)]
    return pl.pallas_call(
        functools.partial(_moe_kernel, tiles_a=tiles_a if xs2 is not None else None, n_chunks=n_chunks),
        out_shape=jax.ShapeDtypeStruct((e + 1, tiles * MOE_TM, d), BF16),
        grid=(e + 1, tiles),
        in_specs=in_specs,
        out_specs=pl.BlockSpec((1, MOE_TM, d), lambda p, i: (p, i, 0)),
        scratch_shapes=[pltpu.VMEM((2, d, f), BF16), pltpu.VMEM((2, d, f), BF16), pltpu.VMEM((2, f, d), BF16)],
        compiler_params=_cparams("arbitrary", "arbitrary"),
        name="moe_ffn",
    )(*args, wg, wu, wd)


COMBINE_TT = 256
COMBINE_W = 256
COMBINE_KW = 1024


def _combine_kernel(p0_ref, tok_ref, z_ref, x_ref, ga_ref, g_ref, *rest, final_norm):
    o_ref, acc_ref = rest[-2:]
    bi = pl.program_id(0)
    ti = pl.program_id(1)
    p0 = p0_ref[bi, ti]
    p1 = p0_ref[bi, ti + 1]
    n_pairs = z_ref.shape[1]
    kw = min(COMBINE_KW, n_pairs)
    tok0 = ti * COMBINE_TT

    def window(start, width):
        tok_row = tok0 + lax.broadcasted_iota(jnp.int32, (COMBINE_TT, width), 0)
        toks = tok_ref[0, :, pl.ds(start, width)]
        onehot = jnp.where(toks == tok_row, 1.0, 0.0).astype(BF16)
        return _dot(onehot, z_ref[0, pl.ds(start, width), :])

    first = jnp.minimum(p0 // COMBINE_W * COMBINE_W, n_pairs - kw)
    fits = p1 <= first + kw

    @pl.when(fits)
    def _():
        acc_ref[...] = window(pl.multiple_of(first, COMBINE_W), kw)

    @pl.when(jnp.logical_not(fits))
    def _():
        acc_ref[...] = jnp.zeros_like(acc_ref)

        def body(w, carry):
            acc_ref[...] += window(pl.multiple_of(w * COMBINE_W, COMBINE_W), COMBINE_W)
            return carry

        lax.fori_loop(p0 // COMBINE_W, (p1 + COMBINE_W - 1) // COMBINE_W, body, 0)

    x = x_ref[0] + ga_ref[0] * acc_ref[...]
    if final_norm:
        ms = jnp.mean(x * x, axis=-1, keepdims=True)
        x = x * lax.rsqrt(ms + EPS) * g_ref[...]
    o_ref[0] = x


def _combine(x, ga, g, tok_sorted, z, p0, *, final_norm, b0=0, prev=None):
    b, n, d = x.shape
    nb, p, _ = z.shape
    row = lambda bi, ti, *_: (b0 + bi, ti, 0)
    args = [p0, tok_sorted, z, x, ga, g]
    in_specs = [pl.BlockSpec((1, 1, p), lambda bi, ti, *_: (bi, 0, 0)),
                pl.BlockSpec((1, p, d), lambda bi, ti, *_: (bi, 0, 0)),
                pl.BlockSpec((1, COMBINE_TT, d), row), _mod_spec(ga, b0), _full_spec(g)]
    aliases = {}
    if prev is not None:
        aliases = {len(args): 0}
        args.append(prev)
        in_specs.append(pl.BlockSpec(memory_space=pl.ANY))
    grid_spec = pltpu.PrefetchScalarGridSpec(
        num_scalar_prefetch=1,
        grid=(nb, n // COMBINE_TT),
        in_specs=in_specs,
        out_specs=pl.BlockSpec((1, COMBINE_TT, d), row),
        scratch_shapes=[pltpu.VMEM((COMBINE_TT, d), F32)])
    return pl.pallas_call(
        functools.partial(_combine_kernel, final_norm=final_norm),
        out_shape=jax.ShapeDtypeStruct((b, n, d), F32),
        grid_spec=grid_spec,
        input_output_aliases=aliases,
        compiler_params=_cparams("parallel", "arbitrary"),
        name="moe_combine",
    )(*args)


ONEHOT_GATHER_MAX_TOKENS = 256


def _onehot_gather_kernel(idx_ref, h_ref, o_ref):
    s, n = idx_ref.shape[1], h_ref.shape[1]
    tok = lax.broadcasted_iota(jnp.int32, (s, n), 1)
    onehot = jnp.where(idx_ref[0] == tok, 1.0, 0.0).astype(BF16)
    o_ref[...] = _dot(onehot, h_ref[0]).reshape(o_ref.shape).astype(o_ref.dtype)


def _onehot_gather(h2, idx):
    b, n, d = h2.shape
    _, e, cap = idx.shape
    return pl.pallas_call(
        _onehot_gather_kernel,
        out_shape=jax.ShapeDtypeStruct((e, b * cap, d), h2.dtype),
        grid=(b,),
        in_specs=[pl.BlockSpec((1, e * cap, 1), lambda bi: (bi, 0, 0)),
                  pl.BlockSpec((1, n, d), lambda bi: (bi, 0, 0))],
        out_specs=pl.BlockSpec((e, cap, d), lambda bi: (0, bi, 0)),
        compiler_params=_cparams("parallel"),
        name="onehot_gather",
    )(idx.reshape(b, e * cap, 1), h2)


def _select(logits):
    n = logits.shape[1]
    cap = EC_FACTOR * n // N_EXPERTS
    aff = jax.nn.softmax(logits[..., :N_EXPERTS], axis=-1)
    return lax.top_k(jnp.swapaxes(aff, 1, 2), cap)


def _routed_rows(h2, gate, idx, b0, nb):
    d = h2.shape[2]
    cap = idx.shape[2]
    idx, gate = idx[b0:b0 + nb], gate[b0:b0 + nb]
    if h2.shape[1] <= ONEHOT_GATHER_MAX_TOKENS and (b0, nb) == (0, h2.shape[0]):
        xs = _onehot_gather(h2, idx)
    else:
        bidx = b0 + jnp.arange(nb, dtype=idx.dtype)[None, :, None]
        xs = h2[bidx, jnp.swapaxes(idx, 0, 1)].reshape(N_EXPERTS, nb * cap, d)
    return xs, jnp.swapaxes(gate, 0, 1).reshape(N_EXPERTS, nb * cap, 1)


def _combine_routed(x, ga, g, idx, ys, row0, *, final_norm, b0=0, prev=None):
    _, n, d = x.shape
    b = idx.shape[0]
    cap = idx.shape[2]
    pairs = N_EXPERTS * cap
    m = ys.shape[1]
    slots = lax.broadcasted_iota(jnp.int32, (b, pairs), 1)
    tok_sorted, order = lax.sort((idx.reshape(b, pairs), slots), dimension=1, num_keys=1)
    flat = (order // cap + 1) * m + row0 + jnp.arange(b, dtype=jnp.int32)[:, None] * cap + order % cap
    z = ys.reshape((N_EXPERTS + 1) * m, d)[flat]
    bounds = jnp.arange(n // COMBINE_TT + 1, dtype=jnp.int32) * COMBINE_TT
    p0 = jnp.sum(tok_sorted[:, None, :] < bounds[None, :, None], axis=-1).astype(jnp.int32)
    return _combine(x, ga, g, tok_sorted.reshape(b, 1, pairs), z, p0, final_norm=final_norm, b0=b0, prev=prev)


def _ec_moe(x, ga, g, h2, logits, wg, wu, wd, *, final_norm, extra=None):
    nb = x.shape[0] // 2
    gate, idx = _select(logits)
    rows = [_routed_rows(h2, gate, idx, b0, nb) for b0 in (0, nb)]
    xs2 = gates2 = None
    if extra is not None:
        ex, ega, eh2, elogits = extra
        egate, eidx = _select(elogits)
        xs2, gates2 = _routed_rows(eh2, egate, eidx, 0, eh2.shape[0])
    ys = [_moe_ffn(*rows[0], wg, wu, wd, xs2, gates2), _moe_ffn(*rows[1], wg, wu, wd)]
    out = _combine_routed(x, ga, g, idx[:nb], ys[0], 0, final_norm=final_norm)
    out = _combine_routed(x, ga, g, idx[nb:], ys[1], 0, final_norm=final_norm, b0=nb, prev=out)
    if extra is None:
        return out
    return out, _combine_routed(ex, ega, g, eidx, ys[0], rows[0][0].shape[1], final_norm=final_norm)


def _rope_quarter(y, cos, sin, first):
    rot = jnp.where(first, pltpu.roll(y, LANES - ROPE_DIM // 2, 1), pltpu.roll(y, ROPE_DIM // 2, 1))
    return y * cos + rot * sin


def _l1_dqkv_kernel(x_ref, g_ref, sh_ref, sc_ref, w_ref, qg_ref, kvg_ref, cos_ref, sin_ref, *out_refs,
                    with_q, rope):
    if with_q:
        cq_ref, ckv_ref, kpe_ref = out_refs
    else:
        ckv_ref, kpe_ref = out_refs
    h = _prenorm(x_ref[0], g_ref[...], sh_ref[0], sc_ref[0]).astype(BF16)
    if with_q:
        cq = _dot(h, w_ref[:, 0:Q_LORA])
        ms = jnp.mean(cq * cq, axis=-1, keepdims=True)
        cq_ref[0] = (cq * lax.rsqrt(ms + EPS) * qg_ref[...]).astype(BF16)
    z = _dot(h, w_ref[:, Q_LORA:Q_LORA + KV_LORA + LANES])
    ckv = z[:, 0:KV_LORA]
    ms = jnp.mean(ckv * ckv, axis=-1, keepdims=True)
    ckv_ref[0] = (ckv * lax.rsqrt(ms + EPS) * kvg_ref[...]).astype(BF16)
    pe = z[:, KV_LORA:KV_LORA + LANES]
    if rope:
        lane = lax.broadcasted_iota(jnp.int32, pe.shape, 1)
        pe = _rope_quarter(pe, cos_ref[...], sin_ref[...], (lane % ROPE_DIM) < ROPE_DIM // 2)
    kpe_ref[0] = (pe + pltpu.roll(pe, ROPE_DIM, 1)).astype(BF16)


def _l1_dqkv(x, g, sh, sc, w, qg, kvg, cos, sin, *, rope, with_q, tm):
    b, l, d = x.shape
    row = lambda bi, i: (bi, i, 0)
    args = [x, g, sh, sc, w, qg, kvg, cos, sin]
    in_specs = [pl.BlockSpec((1, tm, d), row), _full_spec(g), _mod_spec(sh), _mod_spec(sc), _full_spec(w),
                _full_spec(qg), _full_spec(kvg),
                pl.BlockSpec((tm, LANES), lambda bi, i: (i, 0)), pl.BlockSpec((tm, LANES), lambda bi, i: (i, 0))]
    out_shape, out_specs = [], []
    if with_q:
        out_shape.append(jax.ShapeDtypeStruct((b, l, Q_LORA), BF16))
        out_specs.append(pl.BlockSpec((1, tm, Q_LORA), row))
    out_shape += [jax.ShapeDtypeStruct((b, l, KV_LORA), BF16), jax.ShapeDtypeStruct((b, l, LANES), BF16)]
    out_specs += [pl.BlockSpec((1, tm, KV_LORA), row), pl.BlockSpec((1, tm, LANES), row)]

    return pl.pallas_call(
        functools.partial(_l1_dqkv_kernel, with_q=with_q, rope=rope),
        out_shape=tuple(out_shape),
        grid=(b, l // tm),
        in_specs=in_specs,
        out_specs=tuple(out_specs),
        compiler_params=_cparams("parallel", "parallel"),
        name="l1_dqkv",
    )(*args)


def _l1_q_kernel(cq_ref, w_ref, cos_ref, sin_ref, qn_ref, qp_ref):
    a = cq_ref[0]
    cw = 4 * LANES
    n_nope = MLA_HEADS * NOPE_DIM
    qscale = MLA_SCALE * LOG2E
    for c in range(n_nope // cw):
        qn_ref[0, :, c * cw:(c + 1) * cw] = (_dot(a, w_ref[:, c * cw:(c + 1) * cw]) * qscale).astype(BF16)
    cos = cos_ref[...]
    sin = sin_ref[...]
    lane = lax.broadcasted_iota(jnp.int32, cos.shape, 1)
    first = (lane % ROPE_DIM) < ROPE_DIM // 2
    for c in range(MLA_HEADS * ROPE_DIM // cw):
        y = _dot(a, w_ref[:, n_nope + c * cw:n_nope + (c + 1) * cw])
        parts = [_rope_quarter(y[:, j * LANES:(j + 1) * LANES], cos, sin, first) * qscale for j in range(4)]
        qp_ref[0, :, c * cw:(c + 1) * cw] = jnp.concatenate(parts, axis=1).astype(BF16)


def _l1_q(cq, w, cos, sin, *, tm):
    b, l, k = cq.shape
    row = lambda bi, i: (bi, i, 0)
    return pl.pallas_call(
        _l1_q_kernel,
        out_shape=(jax.ShapeDtypeStruct((b, l, MLA_HEADS * NOPE_DIM), BF16),
                   jax.ShapeDtypeStruct((b, l, MLA_HEADS * ROPE_DIM), BF16)),
        grid=(b, l // tm),
        in_specs=[pl.BlockSpec((1, tm, k), row), _full_spec(w),
                  pl.BlockSpec((tm, LANES), lambda bi, i: (i, 0)), pl.BlockSpec((tm, LANES), lambda bi, i: (i, 0))],
        out_specs=(pl.BlockSpec((1, tm, MLA_HEADS * NOPE_DIM), row),
                   pl.BlockSpec((1, tm, MLA_HEADS * ROPE_DIM), row)),
        compiler_params=_cparams("parallel", "parallel"),
        name="l1_q",
    )(cq, w, cos, sin)


def _mm_kernel(a_ref, w_ref, o_ref):
    a = a_ref[0]
    cw = 4 * LANES
    for c in range(w_ref.shape[1] // cw):
        o_ref[0, :, c * cw:(c + 1) * cw] = _dot(a, w_ref[:, c * cw:(c + 1) * cw]).astype(o_ref.dtype)


def _mm(a, w, *, tm):
    b, l, k = a.shape
    n = w.shape[1]
    row = lambda bi, i: (bi, i, 0)
    return pl.pallas_call(
        _mm_kernel,
        out_shape=jax.ShapeDtypeStruct((b, l, n), BF16),
        grid=(b, l // tm),
        in_specs=[pl.BlockSpec((1, tm, k), row), _full_spec(w)],
        out_specs=pl.BlockSpec((1, tm, n), row),
        compiler_params=_cparams("parallel", "parallel"),
        name="l1_kv_up",
    )(a, w)


def _rope_tables(n_tokens, d_rot):
    rows = n_tokens // GRID_W
    row = jnp.repeat(jnp.arange(rows, dtype=F32), GRID_W)
    col = jnp.tile(jnp.arange(GRID_W, dtype=F32), rows)
    n_axis = d_rot // 4
    inv_freq = ROPE_THETA ** (-jnp.arange(n_axis, dtype=F32) / n_axis)
    ang = jnp.concatenate([row[:, None] * inv_freq, col[:, None] * inv_freq], axis=-1)
    cos, sin = jnp.cos(ang), jnp.sin(ang)
    cos_t = jnp.concatenate([cos, cos], axis=-1)
    sin_t = jnp.concatenate([-sin, sin], axis=-1)
    reps = LANES // d_rot
    return jnp.tile(cos_t, (1, reps)), jnp.tile(sin_t, (1, reps))


def _mods(mod, lo, hi):
    return [mod[lo:hi, None, i * D_MODEL:(i + 1) * D_MODEL] for i in range(6)]


def _pad_router(router):
    return jnp.pad(router, ((0, 0), (0, LANES - router.shape[1]))).astype(BF16)


def kernel(x, c, ctx, c_ctx, l0_mod_w, l0_mod_b, l0_norm1_g, l0_w_in, l0_q_norm_g, l0_k_norm_g, l0_dw_w, l0_dw_b, l0_conv_ln_g, l0_conv_ln_b, l0_w_out, l0_norm2_g, l0_router, l0_w_gate, l0_w_up, l0_w_down, l1_mod_w, l1_mod_b, l1_norm1_g, l1_w_dqkv, l1_q_lora_norm_g, l1_w_uq, l1_kv_lora_norm_g, l1_w_ukv, l1_w_out, l1_norm2_g, l1_router, l1_w_gate, l1_w_up, l1_w_down, final_norm_g):
    bsz, seq, d = x.shape
    n_ctx = ctx.shape[1]
    row2 = lambda v: v.reshape(1, -1)

    cond = jnp.zeros((16, d), F32).at[:bsz].set(c).at[bsz].set(c_ctx)
    mod0 = _ada(cond, l0_mod_w, l0_mod_b)
    mod1 = _ada(cond, l1_mod_w, l1_mod_b)

    sh1, sc1, ga1, sh2, sc2, ga2 = _mods(mod0, 0, bsz)
    csh1, csc1, cga1, csh2, csc2, cga2 = _mods(mod0, bsz, bsz + 1)
    s_q, s_k, s_v, s_u = ATTN_WIDTH, ATTN_WIDTH + KV_WIDTH, ATTN_WIDTH + 2 * KV_WIDTH, ATTN_WIDTH + 2 * KV_WIDTH + CONV_WIDTH
    w_in = jnp.concatenate([l0_w_in[:, :s_q], l0_w_in[:, s_v:s_u], l0_w_in[:, s_u:],
                            l0_w_in[:, s_q:s_k], l0_w_in[:, s_k:s_v]], axis=1).astype(BF16)
    cos0, sin0 = _rope_tables(seq, HEAD_DIM)
    g1 = row2(l0_norm1_g)
    qg, kg = row2(l0_q_norm_g), row2(l0_k_norm_g)
    q_l, ug_l, kv_l = _l0_inproj(x, g1, sh1, sc1, w_in, qg, kg, cos0, sin0, rope=True, tm=512)
    q_c, ug_c, kv_c = _l0_inproj(ctx, g1, csh1, csc1, w_in, qg, kg, cos0, sin0, rope=False, tm=n_ctx)
    kv_all = jnp.concatenate([kv_c, kv_l], axis=1)
    a_l = _gqa_attention(q_l, kv_all, tq=1024)
    dw_w = l0_dw_w.reshape(CONV_K, CONV_WIDTH)
    dw_b, ln_g, ln_b = row2(l0_dw_b), row2(l0_conv_ln_g), row2(l0_conv_ln_b)
    cb_l = _conv_branch(ug_l, dw_w, dw_b, ln_g, ln_b)
    w_out0 = l0_w_out.astype(BF16)
    g2 = row2(l0_norm2_g)
    router0 = _pad_router(l0_router)
    x1, h2_l, lg_l = _outproj(x, [a_l, cb_l], w_out0, ga1, g2, sh2, sc2, router0, tm=256)
    a_c = _gqa_attention(q_c, kv_c, tq=n_ctx)
    cb_c = _conv_branch(ug_c, dw_w, dw_b, ln_g, ln_b)
    xc1, h2_c, lg_c = _outproj(ctx, [a_c, cb_c], w_out0, cga1, g2, csh2, csc2, router0, tm=n_ctx)
    x2, xc2 = _ec_moe(x1, ga2, g2, h2_l, lg_l, l0_w_gate, l0_w_up, l0_w_down, final_norm=False,
                      extra=(xc1, cga2, h2_c, lg_c))

    sh1, sc1, ga1, sh2, sc2, ga2_1 = _mods(mod1, 0, bsz)
    csh1, csc1 = _mods(mod1, bsz, bsz + 1)[:2]
    w_dqkv = jnp.pad(l1_w_dqkv, ((0, 0), (0, LANES - ROPE_DIM))).astype(BF16)
    cos1, sin1 = _rope_tables(seq, ROPE_DIM)
    g1 = row2(l1_norm1_g)
    qlg, kvlg = row2(l1_q_lora_norm_g), row2(l1_kv_lora_norm_g)
    cq, ckv_l, kpe_l = _l1_dqkv(x2, g1, sh1, sc1, w_dqkv, qlg, kvlg, cos1, sin1, rope=True, with_q=True, tm=512)
    ckv_c, kpe_c = _l1_dqkv(xc2, g1, csh1, csc1, w_dqkv, qlg, kvlg, cos1, sin1, rope=False, with_q=False, tm=n_ctx)
    w_uq = l1_w_uq.reshape(Q_LORA, MLA_HEADS, QK_DIM)
    w_uq = jnp.concatenate([w_uq[:, :, :NOPE_DIM].reshape(Q_LORA, -1),
                            w_uq[:, :, NOPE_DIM:].reshape(Q_LORA, -1)], axis=1).astype(BF16)
    w_ukv = l1_w_ukv.reshape(KV_LORA, MLA_HEADS, NOPE_DIM + V_DIM)
    w_ukv = jnp.concatenate([w_ukv[:, :, :NOPE_DIM].reshape(KV_LORA, -1),
                             w_ukv[:, :, NOPE_DIM:].reshape(KV_LORA, -1)], axis=1).astype(BF16)
    qn, qp = _l1_q(cq, w_uq, cos1, sin1, tm=512)
    kv1 = _mm(jnp.concatenate([ckv_c, ckv_l], axis=1), w_ukv, tm=256)
    kpe = jnp.concatenate([kpe_c, kpe_l], axis=1)
    a1 = _mla_attention(qn, qp, kv1, kpe, tq=1024)
    x3, h2, lg = _outproj(x2, [a1], l1_w_out.astype(BF16), ga1, row2(l1_norm2_g), sh2, sc2,
                          _pad_router(l1_router), tm=256)
    return _ec_moe(x3, ga2_1, row2(final_norm_g), h2, lg, l1_w_gate, l1_w_up, l1_w_down, final_norm=True)
```

```python
import functools

import jax
import jax.numpy as jnp
from jax import lax
from jax.experimental import pallas as pl
from jax.experimental.pallas import tpu as pltpu

F32 = jnp.float32
BF16 = jnp.bfloat16

D_MODEL = 2048
GRID_W = 64
EPS = 1e-6
ROPE_THETA = 10000.0
HEAD_DIM = 128
N_Q_HEADS = 8
N_KV_HEADS = 2
ATTN_WIDTH = N_Q_HEADS * HEAD_DIM
KV_WIDTH = N_KV_HEADS * HEAD_DIM
CONV_WIDTH = D_MODEL - ATTN_WIDTH
CONV_K = 31
LOG2E = 1.4426950408889634
GQA_SCALE = HEAD_DIM ** -0.5
MLA_HEADS = 16
Q_LORA = 1536
KV_LORA = 512
NOPE_DIM = 128
ROPE_DIM = 64
V_DIM = 128
QK_DIM = NOPE_DIM + ROPE_DIM
MLA_SCALE = QK_DIM ** -0.5
N_EXPERTS = 16
D_EXPERT = 1024
EC_FACTOR = 2

LANES = 128
SUBLANES = 8
VMEM_LIMIT = 56 * 1024 * 1024
CONV_PAD = 16


def _cparams(*sem):
    return pltpu.CompilerParams(dimension_semantics=sem, vmem_limit_bytes=VMEM_LIMIT)


def _dot(a, b):
    return jnp.dot(a, b, preferred_element_type=F32)


def _sigmoid(x):
    return 1.0 / (1.0 + jnp.exp(-x))


def _prenorm(x, g, sh, sc):
    ms = jnp.mean(x * x, axis=-1, keepdims=True)
    y = x * lax.rsqrt(ms + EPS) * g
    return y * (1.0 + sc) + sh


def _mod_spec(arr, b0=0):
    if arr.shape[0] > 1:
        return pl.BlockSpec((1, 1, arr.shape[2]), lambda b, *_: (b0 + b, 0, 0))
    return pl.BlockSpec((1, 1, arr.shape[2]), lambda b, *_: (0, 0, 0))


def _full_spec(arr):
    nd = arr.ndim
    return pl.BlockSpec(arr.shape, lambda *_: (0,) * nd)


def _ada_kernel(c_ref, w_ref, b_ref, o_ref):
    c = c_ref[...]
    s = c * _sigmoid(c)
    s_hi = s.astype(BF16)
    s_lo = (s - s_hi.astype(F32)).astype(BF16)
    w = w_ref[...]
    w_hi = w.astype(BF16)
    w_lo = (w - w_hi.astype(F32)).astype(BF16)
    o_ref[...] = _dot(s_hi, w_hi) + _dot(s_lo, w_hi) + _dot(s_hi, w_lo) + b_ref[...]


def _ada(cond, w, b):
    m, d = cond.shape
    n = w.shape[1]
    tn = 512
    return pl.pallas_call(
        _ada_kernel,
        out_shape=jax.ShapeDtypeStruct((m, n), F32),
        grid=(n // tn,),
        in_specs=[pl.BlockSpec((m, d), lambda j: (0, 0)),
                  pl.BlockSpec((d, tn), lambda j: (0, j)),
                  pl.BlockSpec((1, tn), lambda j: (0, j))],
        out_specs=pl.BlockSpec((m, tn), lambda j: (0, j)),
        compiler_params=_cparams("parallel"),
        name="ada_params",
    )(cond, w, b.reshape(1, n))


def _rope_half(y, cos, sin):
    return y * cos + pltpu.roll(y, HEAD_DIM // 2, 1) * sin


def _l0_inproj_kernel(x_ref, g_ref, sh_ref, sc_ref, w_ref, qg_ref, kg_ref, cos_ref, sin_ref,
                      q_ref, ug_ref, kv_ref, *, rope):
    h = _prenorm(x_ref[0], g_ref[...], sh_ref[0], sc_ref[0]).astype(BF16)
    cos = cos_ref[...]
    sin = sin_ref[...]

    def head(y, gain, scale):
        ms = jnp.mean(y * y, axis=-1, keepdims=True)
        y = y * lax.rsqrt(ms + EPS) * gain
        if rope:
            y = _rope_half(y, cos, sin)
        return y * scale if scale != 1.0 else y

    cw = 4 * HEAD_DIM
    for c in range(ATTN_WIDTH // cw):
        y = _dot(h, w_ref[:, c * cw:(c + 1) * cw])
        parts = [head(y[:, j * HEAD_DIM:(j + 1) * HEAD_DIM], qg_ref[...], GQA_SCALE * LOG2E) for j in range(4)]
        q_ref[0, :, c * cw:(c + 1) * cw] = jnp.concatenate(parts, axis=1).astype(BF16)
    u0 = ATTN_WIDTH
    g0 = ATTN_WIDTH + CONV_WIDTH
    for c in range(CONV_WIDTH // cw):
        u = _dot(h, w_ref[:, u0 + c * cw:u0 + (c + 1) * cw])
        gt = _dot(h, w_ref[:, g0 + c * cw:g0 + (c + 1) * cw])
        ug_ref[0, :, c * cw:(c + 1) * cw] = (u * _sigmoid(gt)).astype(BF16)
    k0 = ATTN_WIDTH + 2 * CONV_WIDTH
    y = _dot(h, w_ref[:, k0:k0 + 2 * KV_WIDTH])
    parts = [head(y[:, j * HEAD_DIM:(j + 1) * HEAD_DIM], kg_ref[...], 1.0) for j in range(N_KV_HEADS)]
    parts.append(y[:, KV_WIDTH:])
    kv_ref[0] = jnp.concatenate(parts, axis=1).astype(BF16)


def _l0_inproj(x, g, sh, sc, w, qg, kg, cos, sin, *, rope, tm):
    b, l, d = x.shape
    n = w.shape[1]
    row = lambda bi, i: (bi, i, 0)
    return pl.pallas_call(
        functools.partial(_l0_inproj_kernel, rope=rope),
        out_shape=(jax.ShapeDtypeStruct((b, l, ATTN_WIDTH), BF16),
                   jax.ShapeDtypeStruct((b, l, CONV_WIDTH), BF16),
                   jax.ShapeDtypeStruct((b, l, 2 * KV_WIDTH), BF16)),
        grid=(b, l // tm),
        in_specs=[pl.BlockSpec((1, tm, d), row), _full_spec(g), _mod_spec(sh), _mod_spec(sc),
                  _full_spec(w), _full_spec(qg), _full_spec(kg),
                  pl.BlockSpec((tm, LANES), lambda bi, i: (i, 0)),
                  pl.BlockSpec((tm, LANES), lambda bi, i: (i, 0))],
        out_specs=(pl.BlockSpec((1, tm, ATTN_WIDTH), row),
                   pl.BlockSpec((1, tm, CONV_WIDTH), row),
                   pl.BlockSpec((1, tm, 2 * KV_WIDTH), row)),
        compiler_params=_cparams("parallel", "parallel"),
        name="l0_inproj",
    )(x, g, sh, sc, w, qg, kg, cos, sin)


ATT_ROWS = 256


def _scores(q, k):
    return lax.dot_general(q, k, (((1,), (1,)), ((), ())), preferred_element_type=F32)


def _softmax_pv(s, v_ones):
    dv = v_ones.shape[1] // 2
    m = jnp.max(s, axis=-1, keepdims=True)
    o = _dot(jnp.exp2(s - m).astype(BF16), v_ones)
    return o[:, :dv] / o[:, dv:]


def _attend_units(n_units, q_of, k_of, v_of, store):
    s = _scores(q_of(0), k_of(0))
    for n in range(n_units):
        s_cur = s
        if n + 1 < n_units:
            s = _scores(q_of(n + 1), k_of(n + 1))
        store(n, _softmax_pv(s_cur, v_of(n)).astype(BF16))


def _gqa_kernel(q_ref, k_ref, v_ref, o_ref, vones_ref):
    @pl.when(pl.program_id(2) == 0)
    def _():
        vones_ref[:, 0:HEAD_DIM] = v_ref[0]
        vones_ref[:, HEAD_DIM:2 * HEAD_DIM] = jnp.ones(v_ref.shape[1:], BF16)

    k = k_ref[0]
    group = N_Q_HEADS // N_KV_HEADS
    n_units = q_ref.shape[1] // ATT_ROWS * group

    def where(n):
        r, j = divmod(n, group)
        return slice(r * ATT_ROWS, (r + 1) * ATT_ROWS), slice(j * HEAD_DIM, (j + 1) * HEAD_DIM)

    def store(n, o):
        rows, cols = where(n)
        o_ref[0, rows, cols] = o

    def q_of(n):
        rows, cols = where(n)
        return q_ref[0, rows, cols]

    _attend_units(n_units, q_of, lambda n: k, lambda n: vones_ref[...], store)


def _gqa_attention(q, kv, *, tq):
    b, l, _ = q.shape
    lk = kv.shape[1]
    gw = ATTN_WIDTH // N_KV_HEADS
    return pl.pallas_call(
        _gqa_kernel,
        out_shape=jax.ShapeDtypeStruct((b, l, ATTN_WIDTH), BF16),
        grid=(b, N_KV_HEADS, l // tq),
        in_specs=[pl.BlockSpec((1, tq, gw), lambda bi, h, i: (bi, i, h)),
                  pl.BlockSpec((1, lk, HEAD_DIM), lambda bi, h, i: (bi, 0, h)),
                  pl.BlockSpec((1, lk, HEAD_DIM), lambda bi, h, i: (bi, 0, N_KV_HEADS + h))],
        out_specs=pl.BlockSpec((1, tq, gw), lambda bi, h, i: (bi, i, h)),
        scratch_shapes=[pltpu.VMEM((lk, 2 * HEAD_DIM), BF16)],
        compiler_params=_cparams("parallel", "parallel", "arbitrary"),
        name="gqa_attention",
    )(q, kv, kv)


MLA_HB = 4


def _mla_kernel(qn_ref, qp_ref, k_ref, v_ref, kpe_ref, o_ref, kcat_ref, vones_ref):
    @pl.when(pl.program_id(2) == 0)
    def _():
        for i in range(MLA_HB):
            kcat_ref[i, :, 0:NOPE_DIM] = k_ref[0, :, i * NOPE_DIM:(i + 1) * NOPE_DIM]
            kcat_ref[i, :, NOPE_DIM:2 * NOPE_DIM] = kpe_ref[0]
            vones_ref[i, :, 0:V_DIM] = v_ref[0, :, i * V_DIM:(i + 1) * V_DIM]
            vones_ref[i, :, V_DIM:2 * V_DIM] = jnp.ones((v_ref.shape[1], V_DIM), BF16)

    lane = lax.broadcasted_iota(jnp.int32, (ATT_ROWS, LANES), 1)
    per_block = LANES // ROPE_DIM
    n_units = qn_ref.shape[1] // ATT_ROWS * MLA_HB

    def q_of(n):
        r, i = divmod(n, MLA_HB)
        rows = slice(r * ATT_ROWS, (r + 1) * ATT_ROWS)
        blk, pos = divmod(i, per_block)
        qp = qp_ref[0, rows, blk * LANES:(blk + 1) * LANES]
        mine = (lane >= pos * ROPE_DIM) & (lane < (pos + 1) * ROPE_DIM)
        return jnp.concatenate(
            [qn_ref[0, rows, i * NOPE_DIM:(i + 1) * NOPE_DIM], jnp.where(mine, qp, jnp.zeros_like(qp))], axis=1)

    def store(n, o):
        r, i = divmod(n, MLA_HB)
        o_ref[0, r * ATT_ROWS:(r + 1) * ATT_ROWS, i * V_DIM:(i + 1) * V_DIM] = o

    _attend_units(n_units, q_of, lambda n: kcat_ref[n % MLA_HB], lambda n: vones_ref[n % MLA_HB], store)


def _mla_attention(qn, qp, kv, kpe, *, tq):
    b, l, _ = qn.shape
    lk = kv.shape[1]
    wn = MLA_HB * NOPE_DIM
    nv0 = MLA_HEADS * NOPE_DIM // wn
    return pl.pallas_call(
        _mla_kernel,
        out_shape=jax.ShapeDtypeStruct((b, l, MLA_HEADS * V_DIM), BF16),
        grid=(b, MLA_HEADS // MLA_HB, l // tq),
        in_specs=[pl.BlockSpec((1, tq, wn), lambda bi, h, i: (bi, i, h)),
                  pl.BlockSpec((1, tq, MLA_HB * ROPE_DIM), lambda bi, h, i: (bi, i, h)),
                  pl.BlockSpec((1, lk, wn), lambda bi, h, i: (bi, 0, h)),
                  pl.BlockSpec((1, lk, wn), lambda bi, h, i: (bi, 0, nv0 + h)),
                  pl.BlockSpec((1, lk, LANES), lambda bi, h, i: (bi, 0, 0))],
        out_specs=pl.BlockSpec((1, tq, wn), lambda bi, h, i: (bi, i, h)),
        scratch_shapes=[pltpu.VMEM((MLA_HB, lk, 2 * NOPE_DIM), BF16), pltpu.VMEM((MLA_HB, lk, 2 * V_DIM), BF16)],
        compiler_params=_cparams("parallel", "parallel", "arbitrary"),
        name="mla_attention",
    )(qn, qp, kv, kv, kpe)


CONV_ROWS = 64
CONV_COLS = 128


def _conv_kernel(ug_ref, w_ref, b_ref, lg_ref, lb_ref, o_ref, pad_ref, y_ref):
    l, c = ug_ref.shape[1], ug_ref.shape[2]
    pad_ref[0:CONV_PAD, :] = jnp.zeros((CONV_PAD, c), F32)
    pad_ref[CONV_PAD + l:2 * CONV_PAD + l, :] = jnp.zeros((CONV_PAD, c), F32)
    pad_ref[CONV_PAD:CONV_PAD + l, :] = ug_ref[0].astype(F32)
    off = CONV_PAD - CONV_K // 2
    nwin = CONV_ROWS + 2 * CONV_PAD

    def body(r, carry):
        r0 = pl.multiple_of(r * CONV_ROWS, CONV_ROWS)
        for cb in range(c // CONV_COLS):
            cs = slice(cb * CONV_COLS, (cb + 1) * CONV_COLS)
            win = pad_ref[pl.ds(r0, nwin), cs]
            acc = jnp.zeros((CONV_ROWS, CONV_COLS), F32) + b_ref[:, cs]
            for s in range(SUBLANES):
                ws = win if s == 0 else pltpu.roll(win, nwin - s, 0)
                for k in range(CONV_K):
                    if (off + k) % SUBLANES == s:
                        j = (off + k) // SUBLANES * SUBLANES
                        acc = acc + w_ref[k:k + 1, cs] * ws[j:j + CONV_ROWS, :]
            y_ref[:, cs] = acc
        y = y_ref[...]
        mu = jnp.mean(y, axis=-1, keepdims=True)
        yc = y - mu
        var = jnp.mean(yc * yc, axis=-1, keepdims=True)
        z = yc * lax.rsqrt(var + EPS) * lg_ref[...] + lb_ref[...]
        o_ref[0, pl.ds(r0, CONV_ROWS), :] = (z * _sigmoid(z)).astype(BF16)
        return carry

    lax.fori_loop(0, l // CONV_ROWS, body, 0)


def _conv_branch(ug, w, b, lg, lb):
    bsz, l, c = ug.shape
    return pl.pallas_call(
        _conv_kernel,
        out_shape=jax.ShapeDtypeStruct((bsz, l, c), BF16),
        grid=(bsz,),
        in_specs=[pl.BlockSpec((1, l, c), lambda bi: (bi, 0, 0)),
                  _full_spec(w), _full_spec(b), _full_spec(lg), _full_spec(lb)],
        out_specs=pl.BlockSpec((1, l, c), lambda bi: (bi, 0, 0)),
        scratch_shapes=[pltpu.VMEM((l + 2 * CONV_PAD, c), F32), pltpu.VMEM((CONV_ROWS, c), F32)],
        compiler_params=_cparams("parallel"),
        name="conv_branch",
    )(ug, w, b, lg, lb)


def _outproj_kernel(*refs, n_in):
    x_ref = refs[0]
    a_refs = refs[1:1 + n_in]
    w_refs = refs[1 + n_in:1 + 2 * n_in]
    ga_ref, g2_ref, sh2_ref, sc2_ref, r_ref, xo_ref, h2_ref, lg_ref = refs[1 + 2 * n_in:]
    acc = _dot(a_refs[0][0], w_refs[0][...])
    for a_ref, w_ref in zip(a_refs[1:], w_refs[1:]):
        acc = acc + _dot(a_ref[0], w_ref[...])
    xn = x_ref[0] + ga_ref[0] * acc
    xo_ref[0] = xn
    h2 = _prenorm(xn, g2_ref[...], sh2_ref[0], sc2_ref[0]).astype(BF16)
    h2_ref[0] = h2
    lg_ref[0] = _dot(h2, r_ref[...])


def _outproj(x, acts, w, ga, g2, sh2, sc2, router, *, tm):
    b, l, d = x.shape
    n_in = len(acts)
    row = lambda bi, i: (bi, i, 0)
    in_specs = [pl.BlockSpec((1, tm, d), row)]
    in_specs += [pl.BlockSpec((1, tm, a.shape[2]), row) for a in acts]
    in_specs += [pl.BlockSpec((acts[r].shape[2], w.shape[1]), lambda bi, i, r=r: (r, 0)) for r in range(n_in)]
    in_specs += [_mod_spec(ga), _full_spec(g2), _mod_spec(sh2), _mod_spec(sc2), _full_spec(router)]
    return pl.pallas_call(
        functools.partial(_outproj_kernel, n_in=n_in),
        out_shape=(jax.ShapeDtypeStruct((b, l, d), F32),
                   jax.ShapeDtypeStruct((b, l, d), BF16),
                   jax.ShapeDtypeStruct((b, l, LANES), F32)),
        grid=(b, l // tm),
        in_specs=in_specs,
        out_specs=(pl.BlockSpec((1, tm, d), row), pl.BlockSpec((1, tm, d), row),
                   pl.BlockSpec((1, tm, LANES), row)),
        compiler_params=_cparams("parallel", "parallel"),
        name="outproj",
    )(x, *acts, *([w] * n_in), ga, g2, sh2, sc2, router)


MOE_TM = 256
MOE_CHUNKS = 8


def _moe_kernel(*refs, tiles_a, n_chunks):
    if tiles_a is None:
        xs_ref, gate_ref = refs[:2]
        xs2_ref = gate2_ref = None
        refs = refs[2:]
    else:
        xs_ref, gate_ref, xs2_ref, gate2_ref = refs[:4]
        refs = refs[4:]
    wg_ref, wu_ref, wd_ref, o_ref, wgb_ref, wub_ref, wdb_ref = refs
    p = pl.program_id(0)
    i = pl.program_id(1)
    n_experts = pl.num_programs(0) - 1
    cg = wg_ref.shape[1]
    cd = wd_ref.shape[1]

    @pl.when((p < n_experts) & (i < n_chunks))
    def _():
        slot = p % 2
        rg = pl.multiple_of(i * cg, cg)
        rd = pl.multiple_of(i * cd, cd)
        wgb_ref[slot, pl.ds(rg, cg), :] = wg_ref[0].astype(BF16)
        wub_ref[slot, pl.ds(rg, cg), :] = wu_ref[0].astype(BF16)
        wdb_ref[slot, pl.ds(rd, cd), :] = wd_ref[0].astype(BF16)

    @pl.when(p == 0)
    def _():
        o_ref[...] = jnp.zeros_like(o_ref)

    @pl.when(p > 0)
    def _():
        slot = (p + 1) % 2
        x = xs_ref[0]
        gate = gate_ref[0]
        if tiles_a is not None:
            x = jnp.where(i < tiles_a, x, xs2_ref[0])
            gate = jnp.where(i < tiles_a, gate, gate2_ref[0])
        g = _dot(x, wgb_ref[slot])
        u = _dot(x, wub_ref[slot])
        hid = (g * _sigmoid(g) * u).astype(BF16)
        o_ref[0] = (_dot(hid, wdb_ref[slot]) * gate).astype(BF16)


def _moe_ffn(xs, gates, wg, wu, wd, xs2=None, gates2=None):
    e, m, d = xs.shape
    f = wg.shape[2]
    tiles_a = m // MOE_TM
    n_chunks = min(MOE_CHUNKS, tiles_a)
    assert m % MOE_TM == 0 and d % (n_chunks * 16) == 0 and f % (n_chunks * 16) == 0
    last_e, last_c, last_a = e - 1, n_chunks - 1, tiles_a - 1
    rows = lambda p, i: (jnp.maximum(p - 1, 0), jnp.minimum(i, last_a), 0)
    chunk = lambda p, i: (jnp.minimum(p, last_e), jnp.minimum(i, last_c), 0)
    args = [xs, gates]
    in_specs = [pl.BlockSpec((1, MOE_TM, d), rows), pl.BlockSpec((1, MOE_TM, 1), rows)]
    if xs2 is not None:
        assert xs2.shape == (e, MOE_TM, d)
        rows2 = lambda p, i: (jnp.maximum(p - 1, 0), 0, 0)
        args += [xs2, gates2]
        in_specs += [pl.BlockSpec((1, MOE_TM, d), rows2), pl.BlockSpec((1, MOE_TM, 1), rows2)]
    tiles = tiles_a + (xs2 is not None)
    in_specs += [pl.BlockSpec((1, d // n_chunks, f), chunk),
                 pl.BlockSpec((1, d // n_chunks, f), chunk),
                 pl.BlockSpec((1, f // n_chunks, d), chunk)]
    return pl.pallas_call(
        functools.partial(_moe_kernel, tiles_a=tiles_a if xs2 is not None else None, n_chunks=n_chunks),
        out_shape=jax.ShapeDtypeStruct((e + 1, tiles * MOE_TM, d), BF16),
        grid=(e + 1, tiles),
        in_specs=in_specs,
        out_specs=pl.BlockSpec((1, MOE_TM, d), lambda p, i: (p, i, 0)),
        scratch_shapes=[pltpu.VMEM((2, d, f), BF16), pltpu.VMEM((2, d, f), BF16), pltpu.VMEM((2, f, d), BF16)],
        compiler_params=_cparams("arbitrary", "arbitrary"),
        name="moe_ffn",
    )(*args, wg, wu, wd)


COMBINE_TT = 256
COMBINE_W = 256
COMBINE_KW = 1024


def _combine_kernel(p0_ref, tok_ref, z_ref, x_ref, ga_ref, g_ref, *rest, final_norm):
    o_ref, acc_ref = rest[-2:]
    bi = pl.program_id(0)
    ti = pl.program_id(1)
    p0 = p0_ref[bi, ti]
    p1 = p0_ref[bi, ti + 1]
    n_pairs = z_ref.shape[1]
    kw = min(COMBINE_KW, n_pairs)
    tok0 = ti * COMBINE_TT

    def window(start, width):
        tok_row = tok0 + lax.broadcasted_iota(jnp.int32, (COMBINE_TT, width), 0)
        toks = tok_ref[0, :, pl.ds(start, width)]
        onehot = jnp.where(toks == tok_row, 1.0, 0.0).astype(BF16)
        return _dot(onehot, z_ref[0, pl.ds(start, width), :])

    first = jnp.minimum(p0 // COMBINE_W * COMBINE_W, n_pairs - kw)
    fits = p1 <= first + kw

    @pl.when(fits)
    def _():
        acc_ref[...] = window(pl.multiple_of(first, COMBINE_W), kw)

    @pl.when(jnp.logical_not(fits))
    def _():
        acc_ref[...] = jnp.zeros_like(acc_ref)

        def body(w, carry):
            acc_ref[...] += window(pl.multiple_of(w * COMBINE_W, COMBINE_W), COMBINE_W)
            return carry

        lax.fori_loop(p0 // COMBINE_W, (p1 + COMBINE_W - 1) // COMBINE_W, body, 0)

    x = x_ref[0] + ga_ref[0] * acc_ref[...]
    if final_norm:
        ms = jnp.mean(x * x, axis=-1, keepdims=True)
        x = x * lax.rsqrt(ms + EPS) * g_ref[...]
    o_ref[0] = x


def _combine(x, ga, g, tok_sorted, z, p0, *, final_norm, b0=0, prev=None):
    b, n, d = x.shape
    nb, p, _ = z.shape
    row = lambda bi, ti, *_: (b0 + bi, ti, 0)
    args = [p0, tok_sorted, z, x, ga, g]
    in_specs = [pl.BlockSpec((1, 1, p), lambda bi, ti, *_: (bi, 0, 0)),
                pl.BlockSpec((1, p, d), lambda bi, ti, *_: (bi, 0, 0)),
                pl.BlockSpec((1, COMBINE_TT, d), row), _mod_spec(ga, b0), _full_spec(g)]
    aliases = {}
    if prev is not None:
        aliases = {len(args): 0}
        args.append(prev)
        in_specs.append(pl.BlockSpec(memory_space=pl.ANY))
    grid_spec = pltpu.PrefetchScalarGridSpec(
        num_scalar_prefetch=1,
        grid=(nb, n // COMBINE_TT),
        in_specs=in_specs,
        out_specs=pl.BlockSpec((1, COMBINE_TT, d), row),
        scratch_shapes=[pltpu.VMEM((COMBINE_TT, d), F32)])
    return pl.pallas_call(
        functools.partial(_combine_kernel, final_norm=final_norm),
        out_shape=jax.ShapeDtypeStruct((b, n, d), F32),
        grid_spec=grid_spec,
        input_output_aliases=aliases,
        compiler_params=_cparams("parallel", "arbitrary"),
        name="moe_combine",
    )(*args)


ONEHOT_GATHER_MAX_TOKENS = 256


def _onehot_gather_kernel(idx_ref, h_ref, o_ref):
    s, n = idx_ref.shape[1], h_ref.shape[1]
    tok = lax.broadcasted_iota(jnp.int32, (s, n), 1)
    onehot = jnp.where(idx_ref[0] == tok, 1.0, 0.0).astype(BF16)
    o_ref[...] = _dot(onehot, h_ref[0]).reshape(o_ref.shape).astype(o_ref.dtype)


def _onehot_gather(h2, idx):
    b, n, d = h2.shape
    _, e, cap = idx.shape
    return pl.pallas_call(
        _onehot_gather_kernel,
        out_shape=jax.ShapeDtypeStruct((e, b * cap, d), h2.dtype),
        grid=(b,),
        in_specs=[pl.BlockSpec((1, e * cap, 1), lambda bi: (bi, 0, 0)),
                  pl.BlockSpec((1, n, d), lambda bi: (bi, 0, 0))],
        out_specs=pl.BlockSpec((e, cap, d), lambda bi: (0, bi, 0)),
        compiler_params=_cparams("parallel"),
        name="onehot_gather",
    )(idx.reshape(b, e * cap, 1), h2)


def _select(logits):
    n = logits.shape[1]
    cap = EC_FACTOR * n // N_EXPERTS
    aff = jax.nn.softmax(logits[..., :N_EXPERTS], axis=-1)
    return lax.top_k(jnp.swapaxes(aff, 1, 2), cap)


def _routed_rows(h2, gate, idx, b0, nb):
    d = h2.shape[2]
    cap = idx.shape[2]
    idx, gate = idx[b0:b0 + nb], gate[b0:b0 + nb]
    if h2.shape[1] <= ONEHOT_GATHER_MAX_TOKENS and (b0, nb) == (0, h2.shape[0]):
        xs = _onehot_gather(h2, idx)
    else:
        bidx = b0 + jnp.arange(nb, dtype=idx.dtype)[None, :, None]
        xs = h2[bidx, jnp.swapaxes(idx, 0, 1)].reshape(N_EXPERTS, nb * cap, d)
    return xs, jnp.swapaxes(gate, 0, 1).reshape(N_EXPERTS, nb * cap, 1)


def _combine_routed(x, ga, g, idx, ys, row0, *, final_norm, b0=0, prev=None):
    _, n, d = x.shape
    b = idx.shape[0]
    cap = idx.shape[2]
    pairs = N_EXPERTS * cap
    m = ys.shape[1]
    slots = lax.broadcasted_iota(jnp.int32, (b, pairs), 1)
    tok_sorted, order = lax.sort((idx.reshape(b, pairs), slots), dimension=1, num_keys=1)
    flat = (order // cap + 1) * m + row0 + jnp.arange(b, dtype=jnp.int32)[:, None] * cap + order % cap
    z = ys.reshape((N_EXPERTS + 1) * m, d)[flat]
    bounds = jnp.arange(n // COMBINE_TT + 1, dtype=jnp.int32) * COMBINE_TT
    p0 = jnp.sum(tok_sorted[:, None, :] < bounds[None, :, None], axis=-1).astype(jnp.int32)
    return _combine(x, ga, g, tok_sorted.reshape(b, 1, pairs), z, p0, final_norm=final_norm, b0=b0, prev=prev)


def _ec_moe(x, ga, g, h2, logits, wg, wu, wd, *, final_norm, ffn_halves, extra=None):
    bsz = x.shape[0]
    nb = bsz // 2
    gate, idx = _select(logits)
    cap = idx.shape[2]
    xs2 = gates2 = None
    if extra is not None:
        ex, ega, eh2, elogits = extra
        egate, eidx = _select(elogits)
        xs2, gates2 = _routed_rows(eh2, egate, eidx, 0, eh2.shape[0])
    if ffn_halves == 2:
        rows = [_routed_rows(h2, gate, idx, b0, nb) for b0 in (0, nb)]
        ys = [_moe_ffn(*rows[0], wg, wu, wd, xs2, gates2), _moe_ffn(*rows[1], wg, wu, wd)]
        row0 = [0, 0]
        extra_row0 = nb * cap
    else:
        ys = [_moe_ffn(*_routed_rows(h2, gate, idx, 0, bsz), wg, wu, wd, xs2, gates2)] * 2
        row0 = [0, nb * cap]
        extra_row0 = bsz * cap
    out = _combine_routed(x, ga, g, idx[:nb], ys[0], row0[0], final_norm=final_norm)
    out = _combine_routed(x, ga, g, idx[nb:], ys[1], row0[1], final_norm=final_norm, b0=nb, prev=out)
    if extra is None:
        return out
    return out, _combine_routed(ex, ega, g, eidx, ys[0], extra_row0, final_norm=final_norm)


def _rope_quarter(y, cos, sin, first):
    rot = jnp.where(first, pltpu.roll(y, LANES - ROPE_DIM // 2, 1), pltpu.roll(y, ROPE_DIM // 2, 1))
    return y * cos + rot * sin


def _l1_dqkv_kernel(x_ref, g_ref, sh_ref, sc_ref, w_ref, qg_ref, kvg_ref, cos_ref, sin_ref, *out_refs,
                    with_q, rope):
    if with_q:
        cq_ref, ckv_ref, kpe_ref = out_refs
    else:
        ckv_ref, kpe_ref = out_refs
    h = _prenorm(x_ref[0], g_ref[...], sh_ref[0], sc_ref[0]).astype(BF16)
    if with_q:
        cq = _dot(h, w_ref[:, 0:Q_LORA])
        ms = jnp.mean(cq * cq, axis=-1, keepdims=True)
        cq_ref[0] = (cq * lax.rsqrt(ms + EPS) * qg_ref[...]).astype(BF16)
    z = _dot(h, w_ref[:, Q_LORA:Q_LORA + KV_LORA + LANES])
    ckv = z[:, 0:KV_LORA]
    ms = jnp.mean(ckv * ckv, axis=-1, keepdims=True)
    ckv_ref[0] = (ckv * lax.rsqrt(ms + EPS) * kvg_ref[...]).astype(BF16)
    pe = z[:, KV_LORA:KV_LORA + LANES]
    if rope:
        lane = lax.broadcasted_iota(jnp.int32, pe.shape, 1)
        pe = _rope_quarter(pe, cos_ref[...], sin_ref[...], (lane % ROPE_DIM) < ROPE_DIM // 2)
    kpe_ref[0] = (pe + pltpu.roll(pe, ROPE_DIM, 1)).astype(BF16)


def _l1_dqkv(x, g, sh, sc, w, qg, kvg, cos, sin, *, rope, with_q, tm):
    b, l, d = x.shape
    row = lambda bi, i: (bi, i, 0)
    args = [x, g, sh, sc, w, qg, kvg, cos, sin]
    in_specs = [pl.BlockSpec((1, tm, d), row), _full_spec(g), _mod_spec(sh), _mod_spec(sc), _full_spec(w),
                _full_spec(qg), _full_spec(kvg),
                pl.BlockSpec((tm, LANES), lambda bi, i: (i, 0)), pl.BlockSpec((tm, LANES), lambda bi, i: (i, 0))]
    out_shape, out_specs = [], []
    if with_q:
        out_shape.append(jax.ShapeDtypeStruct((b, l, Q_LORA), BF16))
        out_specs.append(pl.BlockSpec((1, tm, Q_LORA), row))
    out_shape += [jax.ShapeDtypeStruct((b, l, KV_LORA), BF16), jax.ShapeDtypeStruct((b, l, LANES), BF16)]
    out_specs += [pl.BlockSpec((1, tm, KV_LORA), row), pl.BlockSpec((1, tm, LANES), row)]

    return pl.pallas_call(
        functools.partial(_l1_dqkv_kernel, with_q=with_q, rope=rope),
        out_shape=tuple(out_shape),
        grid=(b, l // tm),
        in_specs=in_specs,
        out_specs=tuple(out_specs),
        compiler_params=_cparams("parallel", "parallel"),
        name="l1_dqkv",
    )(*args)


def _l1_q_kernel(cq_ref, w_ref, cos_ref, sin_ref, qn_ref, qp_ref):
    a = cq_ref[0]
    cw = 4 * LANES
    n_nope = MLA_HEADS * NOPE_DIM
    qscale = MLA_SCALE * LOG2E
    for c in range(n_nope // cw):
        qn_ref[0, :, c * cw:(c + 1) * cw] = (_dot(a, w_ref[:, c * cw:(c + 1) * cw]) * qscale).astype(BF16)
    cos = cos_ref[...]
    sin = sin_ref[...]
    lane = lax.broadcasted_iota(jnp.int32, cos.shape, 1)
    first = (lane % ROPE_DIM) < ROPE_DIM // 2
    for c in range(MLA_HEADS * ROPE_DIM // cw):
        y = _dot(a, w_ref[:, n_nope + c * cw:n_nope + (c + 1) * cw])
        parts = [_rope_quarter(y[:, j * LANES:(j + 1) * LANES], cos, sin, first) * qscale for j in range(4)]
        qp_ref[0, :, c * cw:(c + 1) * cw] = jnp.concatenate(parts, axis=1).astype(BF16)


def _l1_q(cq, w, cos, sin, *, tm):
    b, l, k = cq.shape
    row = lambda bi, i: (bi, i, 0)
    return pl.pallas_call(
        _l1_q_kernel,
        out_shape=(jax.ShapeDtypeStruct((b, l, MLA_HEADS * NOPE_DIM), BF16),
                   jax.ShapeDtypeStruct((b, l, MLA_HEADS * ROPE_DIM), BF16)),
        grid=(b, l // tm),
        in_specs=[pl.BlockSpec((1, tm, k), row), _full_spec(w),
                  pl.BlockSpec((tm, LANES), lambda bi, i: (i, 0)), pl.BlockSpec((tm, LANES), lambda bi, i: (i, 0))],
        out_specs=(pl.BlockSpec((1, tm, MLA_HEADS * NOPE_DIM), row),
                   pl.BlockSpec((1, tm, MLA_HEADS * ROPE_DIM), row)),
        compiler_params=_cparams("parallel", "parallel"),
        name="l1_q",
    )(cq, w, cos, sin)


def _mm_kernel(a_ref, w_ref, o_ref):
    a = a_ref[0]
    cw = 4 * LANES
    for c in range(w_ref.shape[1] // cw):
        o_ref[0, :, c * cw:(c + 1) * cw] = _dot(a, w_ref[:, c * cw:(c + 1) * cw]).astype(o_ref.dtype)


def _mm(a, w, *, tm):
    b, l, k = a.shape
    n = w.shape[1]
    row = lambda bi, i: (bi, i, 0)
    return pl.pallas_call(
        _mm_kernel,
        out_shape=jax.ShapeDtypeStruct((b, l, n), BF16),
        grid=(b, l // tm),
        in_specs=[pl.BlockSpec((1, tm, k), row), _full_spec(w)],
        out_specs=pl.BlockSpec((1, tm, n), row),
        compiler_params=_cparams("parallel", "parallel"),
        name="l1_kv_up",
    )(a, w)


def _rope_tables(n_tokens, d_rot):
    rows = n_tokens // GRID_W
    row = jnp.repeat(jnp.arange(rows, dtype=F32), GRID_W)
    col = jnp.tile(jnp.arange(GRID_W, dtype=F32), rows)
    n_axis = d_rot // 4
    inv_freq = ROPE_THETA ** (-jnp.arange(n_axis, dtype=F32) / n_axis)
    ang = jnp.concatenate([row[:, None] * inv_freq, col[:, None] * inv_freq], axis=-1)
    cos, sin = jnp.cos(ang), jnp.sin(ang)
    cos_t = jnp.concatenate([cos, cos], axis=-1)
    sin_t = jnp.concatenate([-sin, sin], axis=-1)
    reps = LANES // d_rot
    return jnp.tile(cos_t, (1, reps)), jnp.tile(sin_t, (1, reps))


def _mods(mod, lo, hi):
    return [mod[lo:hi, None, i * D_MODEL:(i + 1) * D_MODEL] for i in range(6)]


def _pad_router(router):
    return jnp.pad(router, ((0, 0), (0, LANES - router.shape[1]))).astype(BF16)


def kernel(x, c, ctx, c_ctx, l0_mod_w, l0_mod_b, l0_norm1_g, l0_w_in, l0_q_norm_g, l0_k_norm_g, l0_dw_w, l0_dw_b, l0_conv_ln_g, l0_conv_ln_b, l0_w_out, l0_norm2_g, l0_router, l0_w_gate, l0_w_up, l0_w_down, l1_mod_w, l1_mod_b, l1_norm1_g, l1_w_dqkv, l1_q_lora_norm_g, l1_w_uq, l1_kv_lora_norm_g, l1_w_ukv, l1_w_out, l1_norm2_g, l1_router, l1_w_gate, l1_w_up, l1_w_down, final_norm_g):
    bsz, seq, d = x.shape
    n_ctx = ctx.shape[1]
    row2 = lambda v: v.reshape(1, -1)

    cond = jnp.zeros((16, d), F32).at[:bsz].set(c).at[bsz].set(c_ctx)
    mod0 = _ada(cond, l0_mod_w, l0_mod_b)
    mod1 = _ada(cond, l1_mod_w, l1_mod_b)

    sh1, sc1, ga1, sh2, sc2, ga2 = _mods(mod0, 0, bsz)
    csh1, csc1, cga1, csh2, csc2, cga2 = _mods(mod0, bsz, bsz + 1)
    s_q, s_k, s_v, s_u = ATTN_WIDTH, ATTN_WIDTH + KV_WIDTH, ATTN_WIDTH + 2 * KV_WIDTH, ATTN_WIDTH + 2 * KV_WIDTH + CONV_WIDTH
    w_in = jnp.concatenate([l0_w_in[:, :s_q], l0_w_in[:, s_v:s_u], l0_w_in[:, s_u:],
                            l0_w_in[:, s_q:s_k], l0_w_in[:, s_k:s_v]], axis=1).astype(BF16)
    cos0, sin0 = _rope_tables(seq, HEAD_DIM)
    g1 = row2(l0_norm1_g)
    qg, kg = row2(l0_q_norm_g), row2(l0_k_norm_g)
    q_l, ug_l, kv_l = _l0_inproj(x, g1, sh1, sc1, w_in, qg, kg, cos0, sin0, rope=True, tm=512)
    q_c, ug_c, kv_c = _l0_inproj(ctx, g1, csh1, csc1, w_in, qg, kg, cos0, sin0, rope=False, tm=n_ctx)
    kv_all = jnp.concatenate([kv_c, kv_l], axis=1)
    a_l = _gqa_attention(q_l, kv_all, tq=1024)
    dw_w = l0_dw_w.reshape(CONV_K, CONV_WIDTH)
    dw_b, ln_g, ln_b = row2(l0_dw_b), row2(l0_conv_ln_g), row2(l0_conv_ln_b)
    cb_l = _conv_branch(ug_l, dw_w, dw_b, ln_g, ln_b)
    w_out0 = l0_w_out.astype(BF16)
    g2 = row2(l0_norm2_g)
    router0 = _pad_router(l0_router)
    x1, h2_l, lg_l = _outproj(x, [a_l, cb_l], w_out0, ga1, g2, sh2, sc2, router0, tm=256)
    a_c = _gqa_attention(q_c, kv_c, tq=n_ctx)
    cb_c = _conv_branch(ug_c, dw_w, dw_b, ln_g, ln_b)
    xc1, h2_c, lg_c = _outproj(ctx, [a_c, cb_c], w_out0, cga1, g2, csh2, csc2, router0, tm=n_ctx)
    x2, xc2 = _ec_moe(x1, ga2, g2, h2_l, lg_l, l0_w_gate, l0_w_up, l0_w_down, final_norm=False, ffn_halves=1,
                      extra=(xc1, cga2, h2_c, lg_c))

    sh1, sc1, ga1, sh2, sc2, ga2_1 = _mods(mod1, 0, bsz)
    csh1, csc1 = _mods(mod1, bsz, bsz + 1)[:2]
    w_dqkv = jnp.pad(l1_w_dqkv, ((0, 0), (0, LANES - ROPE_DIM))).astype(BF16)
    cos1, sin1 = _rope_tables(seq, ROPE_DIM)
    g1 = row2(l1_norm1_g)
    qlg, kvlg = row2(l1_q_lora_norm_g), row2(l1_kv_lora_norm_g)
    cq, ckv_l, kpe_l = _l1_dqkv(x2, g1, sh1, sc1, w_dqkv, qlg, kvlg, cos1, sin1, rope=True, with_q=True, tm=512)
    ckv_c, kpe_c = _l1_dqkv(xc2, g1, csh1, csc1, w_dqkv, qlg, kvlg, cos1, sin1, rope=False, with_q=False, tm=n_ctx)
    w_uq = l1_w_uq.reshape(Q_LORA, MLA_HEADS, QK_DIM)
    w_uq = jnp.concatenate([w_uq[:, :, :NOPE_DIM].reshape(Q_LORA, -1),
                            w_uq[:, :, NOPE_DIM:].reshape(Q_LORA, -1)], axis=1).astype(BF16)
    w_ukv = l1_w_ukv.reshape(KV_LORA, MLA_HEADS, NOPE_DIM + V_DIM)
    w_ukv = jnp.concatenate([w_ukv[:, :, :NOPE_DIM].reshape(KV_LORA, -1),
                             w_ukv[:, :, NOPE_DIM:].reshape(KV_LORA, -1)], axis=1).astype(BF16)
    qn, qp = _l1_q(cq, w_uq, cos1, sin1, tm=512)
    kv1 = _mm(jnp.concatenate([ckv_c, ckv_l], axis=1), w_ukv, tm=768)
    kpe = jnp.concatenate([kpe_c, kpe_l], axis=1)
    a1 = _mla_attention(qn, qp, kv1, kpe, tq=1024)
    x3, h2, lg = _outproj(x2, [a1], l1_w_out.astype(BF16), ga1, row2(l1_norm2_g), sh2, sc2,
                          _pad_router(l1_router), tm=256)
    return _ec_moe(x3, ga2_1, row2(final_norm_g), h2, lg, l1_w_gate, l1_w_up, l1_w_down, final_norm=True,
                   ffn_halves=2)
```

```python
import functools

import jax
import jax.numpy as jnp
from jax import lax
from jax.experimental import pallas as pl
from jax.experimental.pallas import tpu as pltpu

F32 = jnp.float32
BF16 = jnp.bfloat16

D_MODEL = 2048
GRID_W = 64
EPS = 1e-6
ROPE_THETA = 10000.0
HEAD_DIM = 128
N_Q_HEADS = 8
N_KV_HEADS = 2
ATTN_WIDTH = N_Q_HEADS * HEAD_DIM
KV_WIDTH = N_KV_HEADS * HEAD_DIM
CONV_WIDTH = D_MODEL - ATTN_WIDTH
CONV_K = 31
LOG2E = 1.4426950408889634
GQA_SCALE = HEAD_DIM ** -0.5
MLA_HEADS = 16
Q_LORA = 1536
KV_LORA = 512
NOPE_DIM = 128
ROPE_DIM = 64
V_DIM = 128
QK_DIM = NOPE_DIM + ROPE_DIM
MLA_SCALE = QK_DIM ** -0.5
N_EXPERTS = 16
D_EXPERT = 1024
EC_FACTOR = 2

LANES = 128
SUBLANES = 8
VMEM_LIMIT = 56 * 1024 * 1024
CONV_PAD = 16


def _cparams(*sem):
    return pltpu.CompilerParams(dimension_semantics=sem, vmem_limit_bytes=VMEM_LIMIT)


def _dot(a, b):
    return jnp.dot(a, b, preferred_element_type=F32)


def _sigmoid(x):
    return 1.0 / (1.0 + jnp.exp(-x))


def _prenorm(x, g, sh, sc):
    ms = jnp.mean(x * x, axis=-1, keepdims=True)
    y = x * lax.rsqrt(ms + EPS) * g
    return y * (1.0 + sc) + sh


def _mod_spec(arr, b0=0):
    if arr.shape[0] > 1:
        return pl.BlockSpec((1, 1, arr.shape[2]), lambda b, *_: (b0 + b, 0, 0))
    return pl.BlockSpec((1, 1, arr.shape[2]), lambda b, *_: (0, 0, 0))


def _full_spec(arr):
    nd = arr.ndim
    return pl.BlockSpec(arr.shape, lambda *_: (0,) * nd)


def _ada_kernel(c_ref, w_ref, b_ref, o_ref):
    c = c_ref[...]
    s = c * _sigmoid(c)
    s_hi = s.astype(BF16)
    s_lo = (s - s_hi.astype(F32)).astype(BF16)
    w = w_ref[...]
    w_hi = w.astype(BF16)
    w_lo = (w - w_hi.astype(F32)).astype(BF16)
    o_ref[...] = _dot(s_hi, w_hi) + _dot(s_lo, w_hi) + _dot(s_hi, w_lo) + b_ref[...]


def _ada(cond, w, b):
    m, d = cond.shape
    n = w.shape[1]
    tn = 512
    return pl.pallas_call(
        _ada_kernel,
        out_shape=jax.ShapeDtypeStruct((m, n), F32),
        grid=(n // tn,),
        in_specs=[pl.BlockSpec((m, d), lambda j: (0, 0)),
                  pl.BlockSpec((d, tn), lambda j: (0, j)),
                  pl.BlockSpec((1, tn), lambda j: (0, j))],
        out_specs=pl.BlockSpec((m, tn), lambda j: (0, j)),
        compiler_params=_cparams("parallel"),
        name="ada_params",
    )(cond, w, b.reshape(1, n))


def _rope_half(y, cos, sin):
    return y * cos + pltpu.roll(y, HEAD_DIM // 2, 1) * sin


def _l0_inproj_kernel(x_ref, g_ref, sh_ref, sc_ref, w_ref, qg_ref, kg_ref, cos_ref, sin_ref,
                      q_ref, ug_ref, kv_ref, *, rope):
    h = _prenorm(x_ref[0], g_ref[...], sh_ref[0], sc_ref[0]).astype(BF16)
    cos = cos_ref[...]
    sin = sin_ref[...]

    def head(y, gain, scale):
        ms = jnp.mean(y * y, axis=-1, keepdims=True)
        y = y * lax.rsqrt(ms + EPS) * gain
        if rope:
            y = _rope_half(y, cos, sin)
        return y * scale if scale != 1.0 else y

    cw = 4 * HEAD_DIM
    for c in range(ATTN_WIDTH // cw):
        y = _dot(h, w_ref[:, c * cw:(c + 1) * cw])
        parts = [head(y[:, j * HEAD_DIM:(j + 1) * HEAD_DIM], qg_ref[...], GQA_SCALE * LOG2E) for j in range(4)]
        q_ref[0, :, c * cw:(c + 1) * cw] = jnp.concatenate(parts, axis=1).astype(BF16)
    u0 = ATTN_WIDTH
    g0 = ATTN_WIDTH + CONV_WIDTH
    for c in range(CONV_WIDTH // cw):
        u = _dot(h, w_ref[:, u0 + c * cw:u0 + (c + 1) * cw])
        gt = _dot(h, w_ref[:, g0 + c * cw:g0 + (c + 1) * cw])
        ug_ref[0, :, c * cw:(c + 1) * cw] = (u * _sigmoid(gt)).astype(BF16)
    k0 = ATTN_WIDTH + 2 * CONV_WIDTH
    y = _dot(h, w_ref[:, k0:k0 + 2 * KV_WIDTH])
    parts = [head(y[:, j * HEAD_DIM:(j + 1) * HEAD_DIM], kg_ref[...], 1.0) for j in range(N_KV_HEADS)]
    parts.append(y[:, KV_WIDTH:])
    kv_ref[0] = jnp.concatenate(parts, axis=1).astype(BF16)


def _l0_inproj(x, g, sh, sc, w, qg, kg, cos, sin, *, rope, tm):
    b, l, d = x.shape
    n = w.shape[1]
    row = lambda bi, i: (bi, i, 0)
    return pl.pallas_call(
        functools.partial(_l0_inproj_kernel, rope=rope),
        out_shape=(jax.ShapeDtypeStruct((b, l, ATTN_WIDTH), BF16),
                   jax.ShapeDtypeStruct((b, l, CONV_WIDTH), BF16),
                   jax.ShapeDtypeStruct((b, l, 2 * KV_WIDTH), BF16)),
        grid=(b, l // tm),
        in_specs=[pl.BlockSpec((1, tm, d), row), _full_spec(g), _mod_spec(sh), _mod_spec(sc),
                  _full_spec(w), _full_spec(qg), _full_spec(kg),
                  pl.BlockSpec((tm, LANES), lambda bi, i: (i, 0)),
                  pl.BlockSpec((tm, LANES), lambda bi, i: (i, 0))],
        out_specs=(pl.BlockSpec((1, tm, ATTN_WIDTH), row),
                   pl.BlockSpec((1, tm, CONV_WIDTH), row),
                   pl.BlockSpec((1, tm, 2 * KV_WIDTH), row)),
        compiler_params=_cparams("parallel", "parallel"),
        name="l0_inproj",
    )(x, g, sh, sc, w, qg, kg, cos, sin)


ATT_ROWS = 256


def _scores(q, k):
    return lax.dot_general(q, k, (((1,), (1,)), ((), ())), preferred_element_type=F32)


def _softmax_pv(s, v_ones):
    dv = v_ones.shape[1] // 2
    m = jnp.max(s, axis=-1, keepdims=True)
    o = _dot(jnp.exp2(s - m).astype(BF16), v_ones)
    return o[:, :dv] / o[:, dv:]


def _attend_units(n_units, q_of, k_of, v_of, store):
    s = _scores(q_of(0), k_of(0))
    for n in range(n_units):
        s_cur = s
        if n + 1 < n_units:
            s = _scores(q_of(n + 1), k_of(n + 1))
        store(n, _softmax_pv(s_cur, v_of(n)).astype(BF16))


def _gqa_kernel(q_ref, *refs):
    kv_refs, (o_ref, kall_ref, vones_ref) = refs[:-3], refs[-3:]

    @pl.when(pl.program_id(2) == 0)
    def _():
        r0 = 0
        for k_ref, v_ref in zip(kv_refs[0::2], kv_refs[1::2]):
            rows = slice(r0, r0 + k_ref.shape[1])
            kall_ref[rows, :] = k_ref[0]
            vones_ref[rows, 0:HEAD_DIM] = v_ref[0]
            r0 += k_ref.shape[1]
        vones_ref[:, HEAD_DIM:2 * HEAD_DIM] = jnp.ones((vones_ref.shape[0], HEAD_DIM), BF16)

    k = kall_ref[...]
    group = N_Q_HEADS // N_KV_HEADS
    n_units = q_ref.shape[1] // ATT_ROWS * group

    def where(n):
        r, j = divmod(n, group)
        return slice(r * ATT_ROWS, (r + 1) * ATT_ROWS), slice(j * HEAD_DIM, (j + 1) * HEAD_DIM)

    def store(n, o):
        rows, cols = where(n)
        o_ref[0, rows, cols] = o

    def q_of(n):
        rows, cols = where(n)
        return q_ref[0, rows, cols]

    _attend_units(n_units, q_of, lambda n: k, lambda n: vones_ref[...], store)


def _gqa_attention(q, kv_parts, *, tq):
    b, l, _ = q.shape
    lk = sum(kv.shape[1] for kv in kv_parts)
    gw = ATTN_WIDTH // N_KV_HEADS
    in_specs = [pl.BlockSpec((1, tq, gw), lambda bi, h, i: (bi, i, h))]
    args = [q]
    for kv in kv_parts:
        in_specs += [pl.BlockSpec((1, kv.shape[1], HEAD_DIM), lambda bi, h, i: (bi, 0, h)),
                     pl.BlockSpec((1, kv.shape[1], HEAD_DIM), lambda bi, h, i: (bi, 0, N_KV_HEADS + h))]
        args += [kv, kv]
    return pl.pallas_call(
        _gqa_kernel,
        out_shape=jax.ShapeDtypeStruct((b, l, ATTN_WIDTH), BF16),
        grid=(b, N_KV_HEADS, l // tq),
        in_specs=in_specs,
        out_specs=pl.BlockSpec((1, tq, gw), lambda bi, h, i: (bi, i, h)),
        scratch_shapes=[pltpu.VMEM((lk, HEAD_DIM), BF16), pltpu.VMEM((lk, 2 * HEAD_DIM), BF16)],
        compiler_params=_cparams("parallel", "parallel", "arbitrary"),
        name="gqa_attention",
    )(*args)


MLA_HB = 4


def _mla_kernel(qn_ref, qp_ref, *refs):
    seg_refs, (o_ref, kcat_ref, vones_ref) = refs[:-3], refs[-3:]

    @pl.when(pl.program_id(2) == 0)
    def _():
        r0 = 0
        for k_ref, v_ref, kpe_ref in zip(seg_refs[0::3], seg_refs[1::3], seg_refs[2::3]):
            rows = slice(r0, r0 + k_ref.shape[1])
            for i in range(MLA_HB):
                kcat_ref[i, rows, 0:NOPE_DIM] = k_ref[0, :, i * NOPE_DIM:(i + 1) * NOPE_DIM]
                kcat_ref[i, rows, NOPE_DIM:2 * NOPE_DIM] = kpe_ref[0]
                vones_ref[i, rows, 0:V_DIM] = v_ref[0, :, i * V_DIM:(i + 1) * V_DIM]
            r0 += k_ref.shape[1]
        for i in range(MLA_HB):
            vones_ref[i, :, V_DIM:2 * V_DIM] = jnp.ones((vones_ref.shape[1], V_DIM), BF16)

    lane = lax.broadcasted_iota(jnp.int32, (ATT_ROWS, LANES), 1)
    per_block = LANES // ROPE_DIM
    n_units = qn_ref.shape[1] // ATT_ROWS * MLA_HB

    def q_of(n):
        r, i = divmod(n, MLA_HB)
        rows = slice(r * ATT_ROWS, (r + 1) * ATT_ROWS)
        blk, pos = divmod(i, per_block)
        qp = qp_ref[0, rows, blk * LANES:(blk + 1) * LANES]
        mine = (lane >= pos * ROPE_DIM) & (lane < (pos + 1) * ROPE_DIM)
        return jnp.concatenate(
            [qn_ref[0, rows, i * NOPE_DIM:(i + 1) * NOPE_DIM], jnp.where(mine, qp, jnp.zeros_like(qp))], axis=1)

    def store(n, o):
        r, i = divmod(n, MLA_HB)
        o_ref[0, r * ATT_ROWS:(r + 1) * ATT_ROWS, i * V_DIM:(i + 1) * V_DIM] = o

    _attend_units(n_units, q_of, lambda n: kcat_ref[n % MLA_HB], lambda n: vones_ref[n % MLA_HB], store)


def _mla_attention(qn, qp, segments, *, tq):
    b, l, _ = qn.shape
    lk = sum(kv.shape[1] for kv, _ in segments)
    wn = MLA_HB * NOPE_DIM
    nv0 = MLA_HEADS * NOPE_DIM // wn
    in_specs = [pl.BlockSpec((1, tq, wn), lambda bi, h, i: (bi, i, h)),
                pl.BlockSpec((1, tq, MLA_HB * ROPE_DIM), lambda bi, h, i: (bi, i, h))]
    args = [qn, qp]
    for kv, kpe in segments:
        lj = kv.shape[1]
        in_specs += [pl.BlockSpec((1, lj, wn), lambda bi, h, i: (bi, 0, h)),
                     pl.BlockSpec((1, lj, wn), lambda bi, h, i: (bi, 0, nv0 + h)),
                     pl.BlockSpec((1, lj, LANES), lambda bi, h, i: (bi, 0, 0))]
        args += [kv, kv, kpe]
    return pl.pallas_call(
        _mla_kernel,
        out_shape=jax.ShapeDtypeStruct((b, l, MLA_HEADS * V_DIM), BF16),
        grid=(b, MLA_HEADS // MLA_HB, l // tq),
        in_specs=in_specs,
        out_specs=pl.BlockSpec((1, tq, wn), lambda bi, h, i: (bi, i, h)),
        scratch_shapes=[pltpu.VMEM((MLA_HB, lk, 2 * NOPE_DIM), BF16), pltpu.VMEM((MLA_HB, lk, 2 * V_DIM), BF16)],
        compiler_params=_cparams("parallel", "parallel", "arbitrary"),
        name="mla_attention",
    )(*args)


CONV_ROWS = 64
CONV_COLS = 128


def _conv_kernel(ug_ref, w_ref, b_ref, lg_ref, lb_ref, o_ref, pad_ref, y_ref):
    l, c = ug_ref.shape[1], ug_ref.shape[2]
    pad_ref[0:CONV_PAD, :] = jnp.zeros((CONV_PAD, c), F32)
    pad_ref[CONV_PAD + l:2 * CONV_PAD + l, :] = jnp.zeros((CONV_PAD, c), F32)
    pad_ref[CONV_PAD:CONV_PAD + l, :] = ug_ref[0].astype(F32)
    off = CONV_PAD - CONV_K // 2
    nwin = CONV_ROWS + 2 * CONV_PAD

    def body(r, carry):
        r0 = pl.multiple_of(r * CONV_ROWS, CONV_ROWS)
        for cb in range(c // CONV_COLS):
            cs = slice(cb * CONV_COLS, (cb + 1) * CONV_COLS)
            win = pad_ref[pl.ds(r0, nwin), cs]
            acc = jnp.zeros((CONV_ROWS, CONV_COLS), F32) + b_ref[:, cs]
            for s in range(SUBLANES):
                ws = win if s == 0 else pltpu.roll(win, nwin - s, 0)
                for k in range(CONV_K):
                    if (off + k) % SUBLANES == s:
                        j = (off + k) // SUBLANES * SUBLANES
                        acc = acc + w_ref[k:k + 1, cs] * ws[j:j + CONV_ROWS, :]
            y_ref[:, cs] = acc
        y = y_ref[...]
        mu = jnp.mean(y, axis=-1, keepdims=True)
        yc = y - mu
        var = jnp.mean(yc * yc, axis=-1, keepdims=True)
        z = yc * lax.rsqrt(var + EPS) * lg_ref[...] + lb_ref[...]
        o_ref[0, pl.ds(r0, CONV_ROWS), :] = (z * _sigmoid(z)).astype(BF16)
        return carry

    lax.fori_loop(0, l // CONV_ROWS, body, 0)


def _conv_branch(ug, w, b, lg, lb):
    bsz, l, c = ug.shape
    return pl.pallas_call(
        _conv_kernel,
        out_shape=jax.ShapeDtypeStruct((bsz, l, c), BF16),
        grid=(bsz,),
        in_specs=[pl.BlockSpec((1, l, c), lambda bi: (bi, 0, 0)),
                  _full_spec(w), _full_spec(b), _full_spec(lg), _full_spec(lb)],
        out_specs=pl.BlockSpec((1, l, c), lambda bi: (bi, 0, 0)),
        scratch_shapes=[pltpu.VMEM((l + 2 * CONV_PAD, c), F32), pltpu.VMEM((CONV_ROWS, c), F32)],
        compiler_params=_cparams("parallel"),
        name="conv_branch",
    )(ug, w, b, lg, lb)


def _outproj_kernel(*refs, n_in):
    x_ref = refs[0]
    a_refs = refs[1:1 + n_in]
    w_refs = refs[1 + n_in:1 + 2 * n_in]
    ga_ref, g2_ref, sh2_ref, sc2_ref, r_ref, xo_ref, h2_ref, lg_ref = refs[1 + 2 * n_in:]
    acc = _dot(a_refs[0][0], w_refs[0][...])
    for a_ref, w_ref in zip(a_refs[1:], w_refs[1:]):
        acc = acc + _dot(a_ref[0], w_ref[...])
    xn = x_ref[0] + ga_ref[0] * acc
    xo_ref[0] = xn
    h2 = _prenorm(xn, g2_ref[...], sh2_ref[0], sc2_ref[0]).astype(BF16)
    h2_ref[0] = h2
    lg_ref[0] = _dot(h2, r_ref[...])


def _outproj(x, acts, w, ga, g2, sh2, sc2, router, *, tm):
    b, l, d = x.shape
    n_in = len(acts)
    row = lambda bi, i: (bi, i, 0)
    in_specs = [pl.BlockSpec((1, tm, d), row)]
    in_specs += [pl.BlockSpec((1, tm, a.shape[2]), row) for a in acts]
    in_specs += [pl.BlockSpec((acts[r].shape[2], w.shape[1]), lambda bi, i, r=r: (r, 0)) for r in range(n_in)]
    in_specs += [_mod_spec(ga), _full_spec(g2), _mod_spec(sh2), _mod_spec(sc2), _full_spec(router)]
    return pl.pallas_call(
        functools.partial(_outproj_kernel, n_in=n_in),
        out_shape=(jax.ShapeDtypeStruct((b, l, d), F32),
                   jax.ShapeDtypeStruct((b, l, d), BF16),
                   jax.ShapeDtypeStruct((b, l, LANES), F32)),
        grid=(b, l // tm),
        in_specs=in_specs,
        out_specs=(pl.BlockSpec((1, tm, d), row), pl.BlockSpec((1, tm, d), row),
                   pl.BlockSpec((1, tm, LANES), row)),
        compiler_params=_cparams("parallel", "parallel"),
        name="outproj",
    )(x, *acts, *([w] * n_in), ga, g2, sh2, sc2, router)


MOE_TM = 256
MOE_CHUNKS = 8


def _moe_kernel(*refs, tiles_a, n_chunks):
    if tiles_a is None:
        xs_ref, xs2_ref = refs[0], None
        refs = refs[1:]
    else:
        xs_ref, xs2_ref = refs[:2]
        refs = refs[2:]
    wg_ref, wu_ref, wd_ref, o_ref, wgb_ref, wub_ref, wdb_ref = refs
    p = pl.program_id(0)
    i = pl.program_id(1)
    n_experts = pl.num_programs(0) - 1
    cg = wg_ref.shape[1]
    cd = wd_ref.shape[1]

    @pl.when((p < n_experts) & (i < n_chunks))
    def _():
        slot = p % 2
        rg = pl.multiple_of(i * cg, cg)
        rd = pl.multiple_of(i * cd, cd)
        wgb_ref[slot, pl.ds(rg, cg), :] = wg_ref[0].astype(BF16)
        wub_ref[slot, pl.ds(rg, cg), :] = wu_ref[0].astype(BF16)
        wdb_ref[slot, pl.ds(rd, cd), :] = wd_ref[0].astype(BF16)

    @pl.when(p == 0)
    def _():
        o_ref[...] = jnp.zeros_like(o_ref)

    @pl.when(p > 0)
    def _():
        slot = (p + 1) % 2
        x = xs_ref[0]
        if tiles_a is not None:
            x = jnp.where(i < tiles_a, x, xs2_ref[0])
        g = _dot(x, wgb_ref[slot])
        u = _dot(x, wub_ref[slot])
        hid = (g * _sigmoid(g) * u).astype(BF16)
        o_ref[0] = _dot(hid, wdb_ref[slot]).astype(BF16)


def _moe_ffn(xs, wg, wu, wd, xs2=None):
    e, m, d = xs.shape
    f = wg.shape[2]
    tiles_a = m // MOE_TM
    n_chunks = min(MOE_CHUNKS, tiles_a)
    assert m % MOE_TM == 0 and d % (n_chunks * 16) == 0 and f % (n_chunks * 16) == 0
    last_e, last_c, last_a = e - 1, n_chunks - 1, tiles_a - 1
    rows = lambda p, i: (jnp.maximum(p - 1, 0), jnp.minimum(i, last_a), 0)
    chunk = lambda p, i: (jnp.minimum(p, last_e), jnp.minimum(i, last_c), 0)
    args = [xs]
    in_specs = [pl.BlockSpec((1, MOE_TM, d), rows)]
    if xs2 is not None:
        assert xs2.shape == (e, MOE_TM, d)
        args.append(xs2)
        in_specs.append(pl.BlockSpec((1, MOE_TM, d), lambda p, i: (jnp.maximum(p - 1, 0), 0, 0)))
    tiles = tiles_a + (xs2 is not None)
    in_specs += [pl.BlockSpec((1, d // n_chunks, f), chunk),
                 pl.BlockSpec((1, d // n_chunks, f), chunk),
                 pl.BlockSpec((1, f // n_chunks, d), chunk)]
    return pl.pallas_call(
        functools.partial(_moe_kernel, tiles_a=tiles_a if xs2 is not None else None, n_chunks=n_chunks),
        out_shape=jax.ShapeDtypeStruct((e + 1, tiles * MOE_TM, d), BF16),
        grid=(e + 1, tiles),
        in_specs=in_specs,
        out_specs=pl.BlockSpec((1, MOE_TM, d), lambda p, i: (p, i, 0)),
        scratch_shapes=[pltpu.VMEM((2, d, f), BF16), pltpu.VMEM((2, d, f), BF16), pltpu.VMEM((2, f, d), BF16)],
        compiler_params=_cparams("arbitrary", "arbitrary"),
        name="moe_ffn",
    )(*args, wg, wu, wd)


COMBINE_TT = 256
COMBINE_W = 256
COMBINE_KW = 1024


def _combine_kernel(p0_ref, tok_ref, gate_ref, z_ref, x_ref, ga_ref, g_ref, *rest, final_norm):
    o_ref, acc_ref = rest[-2:]
    bi = pl.program_id(0)
    ti = pl.program_id(1)
    p0 = p0_ref[bi, ti]
    p1 = p0_ref[bi, ti + 1]
    n_pairs = z_ref.shape[1]
    kw = min(COMBINE_KW, n_pairs)
    tok0 = ti * COMBINE_TT

    def window(start, width):
        tok_row = tok0 + lax.broadcasted_iota(jnp.int32, (COMBINE_TT, width), 0)
        toks = tok_ref[0, :, pl.ds(start, width)]
        gates = gate_ref[0, :, pl.ds(start, width)]
        weights = jnp.where(toks == tok_row, gates, 0.0).astype(BF16)
        return _dot(weights, z_ref[0, pl.ds(start, width), :])

    first = jnp.minimum(p0 // COMBINE_W * COMBINE_W, n_pairs - kw)
    fits = p1 <= first + kw

    @pl.when(fits)
    def _():
        acc_ref[...] = window(pl.multiple_of(first, COMBINE_W), kw)

    @pl.when(jnp.logical_not(fits))
    def _():
        acc_ref[...] = jnp.zeros_like(acc_ref)

        def body(w, carry):
            acc_ref[...] += window(pl.multiple_of(w * COMBINE_W, COMBINE_W), COMBINE_W)
            return carry

        lax.fori_loop(p0 // COMBINE_W, (p1 + COMBINE_W - 1) // COMBINE_W, body, 0)

    x = x_ref[0] + ga_ref[0] * acc_ref[...]
    if final_norm:
        ms = jnp.mean(x * x, axis=-1, keepdims=True)
        x = x * lax.rsqrt(ms + EPS) * g_ref[...]
    o_ref[0] = x


def _combine(x, ga, g, tok_sorted, gate_sorted, z, p0, *, final_norm, b0=0, prev=None):
    b, n, d = x.shape
    nb, p, _ = z.shape
    row = lambda bi, ti, *_: (b0 + bi, ti, 0)
    args = [p0, tok_sorted, gate_sorted, z, x, ga, g]
    in_specs = [pl.BlockSpec((1, 1, p), lambda bi, ti, *_: (bi, 0, 0)),
                pl.BlockSpec((1, 1, p), lambda bi, ti, *_: (bi, 0, 0)),
                pl.BlockSpec((1, p, d), lambda bi, ti, *_: (bi, 0, 0)),
                pl.BlockSpec((1, COMBINE_TT, d), row), _mod_spec(ga, b0), _full_spec(g)]
    aliases = {}
    if prev is not None:
        aliases = {len(args): 0}
        args.append(prev)
        in_specs.append(pl.BlockSpec(memory_space=pl.ANY))
    grid_spec = pltpu.PrefetchScalarGridSpec(
        num_scalar_prefetch=1,
        grid=(nb, n // COMBINE_TT),
        in_specs=in_specs,
        out_specs=pl.BlockSpec((1, COMBINE_TT, d), row),
        scratch_shapes=[pltpu.VMEM((COMBINE_TT, d), F32)])
    return pl.pallas_call(
        functools.partial(_combine_kernel, final_norm=final_norm),
        out_shape=jax.ShapeDtypeStruct((b, n, d), F32),
        grid_spec=grid_spec,
        input_output_aliases=aliases,
        compiler_params=_cparams("parallel", "arbitrary"),
        name="moe_combine",
    )(*args)


ONEHOT_GATHER_MAX_TOKENS = 256


def _onehot_gather_kernel(idx_ref, h_ref, o_ref):
    s, n = idx_ref.shape[1], h_ref.shape[1]
    tok = lax.broadcasted_iota(jnp.int32, (s, n), 1)
    onehot = jnp.where(idx_ref[0] == tok, 1.0, 0.0).astype(BF16)
    o_ref[...] = _dot(onehot, h_ref[0]).reshape(o_ref.shape).astype(o_ref.dtype)


def _onehot_gather(h2, idx):
    b, n, d = h2.shape
    _, e, cap = idx.shape
    return pl.pallas_call(
        _onehot_gather_kernel,
        out_shape=jax.ShapeDtypeStruct((e, b * cap, d), h2.dtype),
        grid=(b,),
        in_specs=[pl.BlockSpec((1, e * cap, 1), lambda bi: (bi, 0, 0)),
                  pl.BlockSpec((1, n, d), lambda bi: (bi, 0, 0))],
        out_specs=pl.BlockSpec((e, cap, d), lambda bi: (0, bi, 0)),
        compiler_params=_cparams("parallel"),
        name="onehot_gather",
    )(idx.reshape(b, e * cap, 1), h2)


def _select(logits):
    n = logits.shape[1]
    cap = EC_FACTOR * n // N_EXPERTS
    aff = jax.nn.softmax(logits[..., :N_EXPERTS], axis=-1)
    return lax.top_k(jnp.swapaxes(aff, 1, 2), cap)


def _routed_rows(h2, idx, b0, nb):
    d = h2.shape[2]
    cap = idx.shape[2]
    idx = idx[b0:b0 + nb]
    if h2.shape[1] <= ONEHOT_GATHER_MAX_TOKENS and (b0, nb) == (0, h2.shape[0]):
        return _onehot_gather(h2, idx)
    bidx = b0 + jnp.arange(nb, dtype=idx.dtype)[None, :, None]
    return h2[bidx, jnp.swapaxes(idx, 0, 1)].reshape(N_EXPERTS, nb * cap, d)


def _combine_routed(x, ga, g, gate, idx, ys, row0, *, final_norm, b0=0, prev=None):
    _, n, d = x.shape
    b = idx.shape[0]
    cap = idx.shape[2]
    pairs = N_EXPERTS * cap
    m = ys.shape[1]
    slots = lax.broadcasted_iota(jnp.int32, (b, pairs), 1)
    tok_sorted, order, gate_sorted = lax.sort((idx.reshape(b, pairs), slots, gate.reshape(b, pairs)),
                                              dimension=1, num_keys=1)
    flat = (order // cap + 1) * m + row0 + jnp.arange(b, dtype=jnp.int32)[:, None] * cap + order % cap
    z = ys.reshape((N_EXPERTS + 1) * m, d)[flat]
    bounds = jnp.arange(n // COMBINE_TT + 1, dtype=jnp.int32) * COMBINE_TT
    p0 = jnp.sum(tok_sorted[:, None, :] < bounds[None, :, None], axis=-1).astype(jnp.int32)
    return _combine(x, ga, g, tok_sorted.reshape(b, 1, pairs), gate_sorted.reshape(b, 1, pairs), z, p0,
                    final_norm=final_norm, b0=b0, prev=prev)


def _ec_moe(x, ga, g, h2, logits, wg, wu, wd, *, final_norm, ffn_halves, extra=None):
    bsz = x.shape[0]
    nb = bsz // 2
    gate, idx = _select(logits)
    cap = idx.shape[2]
    xs2 = None
    if extra is not None:
        ex, ega, eh2, elogits = extra
        egate, eidx = _select(elogits)
        xs2 = _routed_rows(eh2, eidx, 0, eh2.shape[0])
    if ffn_halves == 2:
        ys = [_moe_ffn(_routed_rows(h2, idx, 0, nb), wg, wu, wd, xs2),
              _moe_ffn(_routed_rows(h2, idx, nb, nb), wg, wu, wd)]
        row0 = [0, 0]
        extra_row0 = nb * cap
    else:
        ys = [_moe_ffn(_routed_rows(h2, idx, 0, bsz), wg, wu, wd, xs2)] * 2
        row0 = [0, nb * cap]
        extra_row0 = bsz * cap
    out = _combine_routed(x, ga, g, gate[:nb], idx[:nb], ys[0], row0[0], final_norm=final_norm)
    out = _combine_routed(x, ga, g, gate[nb:], idx[nb:], ys[1], row0[1], final_norm=final_norm, b0=nb, prev=out)
    if extra is None:
        return out
    return out, _combine_routed(ex, ega, g, egate, eidx, ys[0], extra_row0, final_norm=final_norm)


def _rope_quarter(y, cos, sin, first):
    rot = jnp.where(first, pltpu.roll(y, LANES - ROPE_DIM // 2, 1), pltpu.roll(y, ROPE_DIM // 2, 1))
    return y * cos + rot * sin


def _l1_dqkv_kernel(x_ref, g_ref, sh_ref, sc_ref, w_ref, qg_ref, kvg_ref, cos_ref, sin_ref, *out_refs,
                    with_q, rope):
    if with_q:
        cq_ref, ckv_ref, kpe_ref = out_refs
    else:
        ckv_ref, kpe_ref = out_refs
    h = _prenorm(x_ref[0], g_ref[...], sh_ref[0], sc_ref[0]).astype(BF16)
    if with_q:
        cq = _dot(h, w_ref[:, 0:Q_LORA])
        ms = jnp.mean(cq * cq, axis=-1, keepdims=True)
        cq_ref[0] = (cq * lax.rsqrt(ms + EPS) * qg_ref[...]).astype(BF16)
    z = _dot(h, w_ref[:, Q_LORA:Q_LORA + KV_LORA + LANES])
    ckv = z[:, 0:KV_LORA]
    ms = jnp.mean(ckv * ckv, axis=-1, keepdims=True)
    ckv_ref[0] = (ckv * lax.rsqrt(ms + EPS) * kvg_ref[...]).astype(BF16)
    pe = z[:, KV_LORA:KV_LORA + LANES]
    if rope:
        lane = lax.broadcasted_iota(jnp.int32, pe.shape, 1)
        pe = _rope_quarter(pe, cos_ref[...], sin_ref[...], (lane % ROPE_DIM) < ROPE_DIM // 2)
    kpe_ref[0] = (pe + pltpu.roll(pe, ROPE_DIM, 1)).astype(BF16)


def _l1_dqkv(x, g, sh, sc, w, qg, kvg, cos, sin, *, rope, with_q, tm):
    b, l, d = x.shape
    row = lambda bi, i: (bi, i, 0)
    args = [x, g, sh, sc, w, qg, kvg, cos, sin]
    in_specs = [pl.BlockSpec((1, tm, d), row), _full_spec(g), _mod_spec(sh), _mod_spec(sc), _full_spec(w),
                _full_spec(qg), _full_spec(kvg),
                pl.BlockSpec((tm, LANES), lambda bi, i: (i, 0)), pl.BlockSpec((tm, LANES), lambda bi, i: (i, 0))]
    out_shape, out_specs = [], []
    if with_q:
        out_shape.append(jax.ShapeDtypeStruct((b, l, Q_LORA), BF16))
        out_specs.append(pl.BlockSpec((1, tm, Q_LORA), row))
    out_shape += [jax.ShapeDtypeStruct((b, l, KV_LORA), BF16), jax.ShapeDtypeStruct((b, l, LANES), BF16)]
    out_specs += [pl.BlockSpec((1, tm, KV_LORA), row), pl.BlockSpec((1, tm, LANES), row)]

    return pl.pallas_call(
        functools.partial(_l1_dqkv_kernel, with_q=with_q, rope=rope),
        out_shape=tuple(out_shape),
        grid=(b, l // tm),
        in_specs=in_specs,
        out_specs=tuple(out_specs),
        compiler_params=_cparams("parallel", "parallel"),
        name="l1_dqkv",
    )(*args)


def _l1_q_kernel(cq_ref, w_ref, cos_ref, sin_ref, qn_ref, qp_ref):
    a = cq_ref[0]
    cw = 4 * LANES
    n_nope = MLA_HEADS * NOPE_DIM
    qscale = MLA_SCALE * LOG2E
    for c in range(n_nope // cw):
        qn_ref[0, :, c * cw:(c + 1) * cw] = (_dot(a, w_ref[:, c * cw:(c + 1) * cw]) * qscale).astype(BF16)
    cos = cos_ref[...]
    sin = sin_ref[...]
    lane = lax.broadcasted_iota(jnp.int32, cos.shape, 1)
    first = (lane % ROPE_DIM) < ROPE_DIM // 2
    for c in range(MLA_HEADS * ROPE_DIM // cw):
        y = _dot(a, w_ref[:, n_nope + c * cw:n_nope + (c + 1) * cw])
        parts = [_rope_quarter(y[:, j * LANES:(j + 1) * LANES], cos, sin, first) * qscale for j in range(4)]
        qp_ref[0, :, c * cw:(c + 1) * cw] = jnp.concatenate(parts, axis=1).astype(BF16)


def _l1_q(cq, w, cos, sin, *, tm):
    b, l, k = cq.shape
    row = lambda bi, i: (bi, i, 0)
    return pl.pallas_call(
        _l1_q_kernel,
        out_shape=(jax.ShapeDtypeStruct((b, l, MLA_HEADS * NOPE_DIM), BF16),
                   jax.ShapeDtypeStruct((b, l, MLA_HEADS * ROPE_DIM), BF16)),
        grid=(b, l // tm),
        in_specs=[pl.BlockSpec((1, tm, k), row), _full_spec(w),
                  pl.BlockSpec((tm, LANES), lambda bi, i: (i, 0)), pl.BlockSpec((tm, LANES), lambda bi, i: (i, 0))],
        out_specs=(pl.BlockSpec((1, tm, MLA_HEADS * NOPE_DIM), row),
                   pl.BlockSpec((1, tm, MLA_HEADS * ROPE_DIM), row)),
        compiler_params=_cparams("parallel", "parallel"),
        name="l1_q",
    )(cq, w, cos, sin)


def _mm_kernel(a_ref, w_ref, o_ref):
    a = a_ref[0]
    cw = 4 * LANES
    for c in range(w_ref.shape[1] // cw):
        o_ref[0, :, c * cw:(c + 1) * cw] = _dot(a, w_ref[:, c * cw:(c + 1) * cw]).astype(o_ref.dtype)


def _mm(a, w, *, tm):
    b, l, k = a.shape
    n = w.shape[1]
    row = lambda bi, i: (bi, i, 0)
    return pl.pallas_call(
        _mm_kernel,
        out_shape=jax.ShapeDtypeStruct((b, l, n), BF16),
        grid=(b, l // tm),
        in_specs=[pl.BlockSpec((1, tm, k), row), _full_spec(w)],
        out_specs=pl.BlockSpec((1, tm, n), row),
        compiler_params=_cparams("parallel", "parallel"),
        name="l1_kv_up",
    )(a, w)


def _rope_tables(n_tokens, d_rot):
    rows = n_tokens // GRID_W
    row = jnp.repeat(jnp.arange(rows, dtype=F32), GRID_W)
    col = jnp.tile(jnp.arange(GRID_W, dtype=F32), rows)
    n_axis = d_rot // 4
    inv_freq = ROPE_THETA ** (-jnp.arange(n_axis, dtype=F32) / n_axis)
    ang = jnp.concatenate([row[:, None] * inv_freq, col[:, None] * inv_freq], axis=-1)
    cos, sin = jnp.cos(ang), jnp.sin(ang)
    cos_t = jnp.concatenate([cos, cos], axis=-1)
    sin_t = jnp.concatenate([-sin, sin], axis=-1)
    reps = LANES // d_rot
    return jnp.tile(cos_t, (1, reps)), jnp.tile(sin_t, (1, reps))


def _mods(mod, lo, hi):
    return [mod[lo:hi, None, i * D_MODEL:(i + 1) * D_MODEL] for i in range(6)]


def _pad_router(router):
    return jnp.pad(router, ((0, 0), (0, LANES - router.shape[1]))).astype(BF16)


def kernel(x, c, ctx, c_ctx, l0_mod_w, l0_mod_b, l0_norm1_g, l0_w_in, l0_q_norm_g, l0_k_norm_g, l0_dw_w, l0_dw_b, l0_conv_ln_g, l0_conv_ln_b, l0_w_out, l0_norm2_g, l0_router, l0_w_gate, l0_w_up, l0_w_down, l1_mod_w, l1_mod_b, l1_norm1_g, l1_w_dqkv, l1_q_lora_norm_g, l1_w_uq, l1_kv_lora_norm_g, l1_w_ukv, l1_w_out, l1_norm2_g, l1_router, l1_w_gate, l1_w_up, l1_w_down, final_norm_g):
    bsz, seq, d = x.shape
    n_ctx = ctx.shape[1]
    row2 = lambda v: v.reshape(1, -1)

    cond = jnp.zeros((16, d), F32).at[:bsz].set(c).at[bsz].set(c_ctx)
    mod0 = _ada(cond, l0_mod_w, l0_mod_b)
    mod1 = _ada(cond, l1_mod_w, l1_mod_b)

    sh1, sc1, ga1, sh2, sc2, ga2 = _mods(mod0, 0, bsz)
    csh1, csc1, cga1, csh2, csc2, cga2 = _mods(mod0, bsz, bsz + 1)
    s_q, s_k, s_v, s_u = ATTN_WIDTH, ATTN_WIDTH + KV_WIDTH, ATTN_WIDTH + 2 * KV_WIDTH, ATTN_WIDTH + 2 * KV_WIDTH + CONV_WIDTH
    w_in = jnp.concatenate([l0_w_in[:, :s_q], l0_w_in[:, s_v:s_u], l0_w_in[:, s_u:],
                            l0_w_in[:, s_q:s_k], l0_w_in[:, s_k:s_v]], axis=1).astype(BF16)
    cos0, sin0 = _rope_tables(seq, HEAD_DIM)
    g1 = row2(l0_norm1_g)
    qg, kg = row2(l0_q_norm_g), row2(l0_k_norm_g)
    q_l, ug_l, kv_l = _l0_inproj(x, g1, sh1, sc1, w_in, qg, kg, cos0, sin0, rope=True, tm=512)
    q_c, ug_c, kv_c = _l0_inproj(ctx, g1, csh1, csc1, w_in, qg, kg, cos0, sin0, rope=False, tm=n_ctx)
    a_l = _gqa_attention(q_l, [kv_c, kv_l], tq=1024)
    dw_w = l0_dw_w.reshape(CONV_K, CONV_WIDTH)
    dw_b, ln_g, ln_b = row2(l0_dw_b), row2(l0_conv_ln_g), row2(l0_conv_ln_b)
    cb_l = _conv_branch(ug_l, dw_w, dw_b, ln_g, ln_b)
    w_out0 = l0_w_out.astype(BF16)
    g2 = row2(l0_norm2_g)
    router0 = _pad_router(l0_router)
    x1, h2_l, lg_l = _outproj(x, [a_l, cb_l], w_out0, ga1, g2, sh2, sc2, router0, tm=256)
    a_c = _gqa_attention(q_c, [kv_c], tq=n_ctx)
    cb_c = _conv_branch(ug_c, dw_w, dw_b, ln_g, ln_b)
    xc1, h2_c, lg_c = _outproj(ctx, [a_c, cb_c], w_out0, cga1, g2, csh2, csc2, router0, tm=n_ctx)
    x2, xc2 = _ec_moe(x1, ga2, g2, h2_l, lg_l, l0_w_gate, l0_w_up, l0_w_down, final_norm=False, ffn_halves=1,
                      extra=(xc1, cga2, h2_c, lg_c))

    sh1, sc1, ga1, sh2, sc2, ga2_1 = _mods(mod1, 0, bsz)
    csh1, csc1 = _mods(mod1, bsz, bsz + 1)[:2]
    w_dqkv = jnp.pad(l1_w_dqkv, ((0, 0), (0, LANES - ROPE_DIM))).astype(BF16)
    cos1, sin1 = _rope_tables(seq, ROPE_DIM)
    g1 = row2(l1_norm1_g)
    qlg, kvlg = row2(l1_q_lora_norm_g), row2(l1_kv_lora_norm_g)
    cq, ckv_l, kpe_l = _l1_dqkv(x2, g1, sh1, sc1, w_dqkv, qlg, kvlg, cos1, sin1, rope=True, with_q=True, tm=512)
    ckv_c, kpe_c = _l1_dqkv(xc2, g1, csh1, csc1, w_dqkv, qlg, kvlg, cos1, sin1, rope=False, with_q=False, tm=n_ctx)
    w_uq = l1_w_uq.reshape(Q_LORA, MLA_HEADS, QK_DIM)
    w_uq = jnp.concatenate([w_uq[:, :, :NOPE_DIM].reshape(Q_LORA, -1),
                            w_uq[:, :, NOPE_DIM:].reshape(Q_LORA, -1)], axis=1).astype(BF16)
    w_ukv = l1_w_ukv.reshape(KV_LORA, MLA_HEADS, NOPE_DIM + V_DIM)
    w_ukv = jnp.concatenate([w_ukv[:, :, :NOPE_DIM].reshape(KV_LORA, -1),
                             w_ukv[:, :, NOPE_DIM:].reshape(KV_LORA, -1)], axis=1).astype(BF16)
    qn, qp = _l1_q(cq, w_uq, cos1, sin1, tm=512)
    kv_c1 = _mm(ckv_c, w_ukv, tm=n_ctx)
    kv_l1 = _mm(ckv_l, w_ukv, tm=1024)
    a1 = _mla_attention(qn, qp, [(kv_c1, kpe_c), (kv_l1, kpe_l)], tq=1024)
    x3, h2, lg = _outproj(x2, [a1], l1_w_out.astype(BF16), ga1, row2(l1_norm2_g), sh2, sc2,
                          _pad_router(l1_router), tm=256)
    return _ec_moe(x3, ga2_1, row2(final_norm_g), h2, lg, l1_w_gate, l1_w_up, l1_w_down, final_norm=True,
                   ffn_halves=2)
```

```python
import functools

import jax
import jax.numpy as jnp
from jax import lax
from jax.experimental import pallas as pl
from jax.experimental.pallas import tpu as pltpu

F32 = jnp.float32
BF16 = jnp.bfloat16

D_MODEL = 2048
GRID_W = 64
EPS = 1e-6
ROPE_THETA = 10000.0
HEAD_DIM = 128
N_Q_HEADS = 8
N_KV_HEADS = 2
ATTN_WIDTH = N_Q_HEADS * HEAD_DIM
KV_WIDTH = N_KV_HEADS * HEAD_DIM
CONV_WIDTH = D_MODEL - ATTN_WIDTH
CONV_K = 31
LOG2E = 1.4426950408889634
GQA_SCALE = HEAD_DIM ** -0.5
MLA_HEADS = 16
Q_LORA = 1536
KV_LORA = 512
NOPE_DIM = 128
ROPE_DIM = 64
V_DIM = 128
QK_DIM = NOPE_DIM + ROPE_DIM
MLA_SCALE = QK_DIM ** -0.5
N_EXPERTS = 16
D_EXPERT = 1024
EC_FACTOR = 2

LANES = 128
SUBLANES = 8
VMEM_LIMIT = 56 * 1024 * 1024
CONV_PAD = 16


def _cparams(*sem):
    return pltpu.CompilerParams(dimension_semantics=sem, vmem_limit_bytes=VMEM_LIMIT)


def _dot(a, b):
    return jnp.dot(a, b, preferred_element_type=F32)


def _sigmoid(x):
    return 1.0 / (1.0 + jnp.exp(-x))


def _prenorm(x, g, sh, sc):
    ms = jnp.mean(x * x, axis=-1, keepdims=True)
    y = x * lax.rsqrt(ms + EPS) * g
    return y * (1.0 + sc) + sh


def _mod_spec(arr, b0=0):
    if arr.shape[0] > 1:
        return pl.BlockSpec((1, 1, arr.shape[2]), lambda b, *_: (b0 + b, 0, 0))
    return pl.BlockSpec((1, 1, arr.shape[2]), lambda b, *_: (0, 0, 0))


def _full_spec(arr):
    nd = arr.ndim
    return pl.BlockSpec(arr.shape, lambda *_: (0,) * nd)


def _ada_kernel(c_ref, w_ref, b_ref, o_ref):
    c = c_ref[...]
    s = c * _sigmoid(c)
    s_hi = s.astype(BF16)
    s_lo = (s - s_hi.astype(F32)).astype(BF16)
    w = w_ref[...]
    w_hi = w.astype(BF16)
    w_lo = (w - w_hi.astype(F32)).astype(BF16)
    o_ref[...] = _dot(s_hi, w_hi) + _dot(s_lo, w_hi) + _dot(s_hi, w_lo) + b_ref[...]


def _ada(cond, w, b):
    m, d = cond.shape
    n = w.shape[1]
    tn = 512
    return pl.pallas_call(
        _ada_kernel,
        out_shape=jax.ShapeDtypeStruct((m, n), F32),
        grid=(n // tn,),
        in_specs=[pl.BlockSpec((m, d), lambda j: (0, 0)),
                  pl.BlockSpec((d, tn), lambda j: (0, j)),
                  pl.BlockSpec((1, tn), lambda j: (0, j))],
        out_specs=pl.BlockSpec((m, tn), lambda j: (0, j)),
        compiler_params=_cparams("parallel"),
        name="ada_params",
    )(cond, w, b.reshape(1, n))


def _rope_half(y, cos, sin):
    return y * cos + pltpu.roll(y, HEAD_DIM // 2, 1) * sin


def _l0_inproj_kernel(x_ref, g_ref, sh_ref, sc_ref, w_ref, qg_ref, kg_ref, cos_ref, sin_ref,
                      q_ref, ug_ref, kv_ref, *, rope):
    h = _prenorm(x_ref[0], g_ref[...], sh_ref[0], sc_ref[0]).astype(BF16)
    cos = cos_ref[...]
    sin = sin_ref[...]

    def head(y, gain, scale):
        ms = jnp.mean(y * y, axis=-1, keepdims=True)
        y = y * lax.rsqrt(ms + EPS) * gain
        if rope:
            y = _rope_half(y, cos, sin)
        return y * scale if scale != 1.0 else y

    cw = 4 * HEAD_DIM
    for c in range(ATTN_WIDTH // cw):
        y = _dot(h, w_ref[:, c * cw:(c + 1) * cw])
        parts = [head(y[:, j * HEAD_DIM:(j + 1) * HEAD_DIM], qg_ref[...], GQA_SCALE * LOG2E) for j in range(4)]
        q_ref[0, :, c * cw:(c + 1) * cw] = jnp.concatenate(parts, axis=1).astype(BF16)
    u0 = ATTN_WIDTH
    g0 = ATTN_WIDTH + CONV_WIDTH
    for c in range(CONV_WIDTH // cw):
        u = _dot(h, w_ref[:, u0 + c * cw:u0 + (c + 1) * cw])
        gt = _dot(h, w_ref[:, g0 + c * cw:g0 + (c + 1) * cw])
        ug_ref[0, :, c * cw:(c + 1) * cw] = (u * _sigmoid(gt)).astype(BF16)
    k0 = ATTN_WIDTH + 2 * CONV_WIDTH
    y = _dot(h, w_ref[:, k0:k0 + 2 * KV_WIDTH])
    parts = [head(y[:, j * HEAD_DIM:(j + 1) * HEAD_DIM], kg_ref[...], 1.0) for j in range(N_KV_HEADS)]
    parts.append(y[:, KV_WIDTH:])
    kv_ref[0] = jnp.concatenate(parts, axis=1).astype(BF16)


def _l0_inproj(x, g, sh, sc, w, qg, kg, cos, sin, *, rope, tm):
    b, l, d = x.shape
    n = w.shape[1]
    row = lambda bi, i: (bi, i, 0)
    return pl.pallas_call(
        functools.partial(_l0_inproj_kernel, rope=rope),
        out_shape=(jax.ShapeDtypeStruct((b, l, ATTN_WIDTH), BF16),
                   jax.ShapeDtypeStruct((b, l, CONV_WIDTH), BF16),
                   jax.ShapeDtypeStruct((b, l, 2 * KV_WIDTH), BF16)),
        grid=(b, l // tm),
        in_specs=[pl.BlockSpec((1, tm, d), row), _full_spec(g), _mod_spec(sh), _mod_spec(sc),
                  _full_spec(w), _full_spec(qg), _full_spec(kg),
                  pl.BlockSpec((tm, LANES), lambda bi, i: (i, 0)),
                  pl.BlockSpec((tm, LANES), lambda bi, i: (i, 0))],
        out_specs=(pl.BlockSpec((1, tm, ATTN_WIDTH), row),
                   pl.BlockSpec((1, tm, CONV_WIDTH), row),
                   pl.BlockSpec((1, tm, 2 * KV_WIDTH), row)),
        compiler_params=_cparams("parallel", "parallel"),
        name="l0_inproj",
    )(x, g, sh, sc, w, qg, kg, cos, sin)


ATT_ROWS = 256


def _scores(q, k):
    return lax.dot_general(q, k, (((1,), (1,)), ((), ())), preferred_element_type=F32)


def _softmax_pv(s, v_ones):
    dv = v_ones.shape[1] // 2
    m = jnp.max(s, axis=-1, keepdims=True)
    o = _dot(jnp.exp2(s - m).astype(BF16), v_ones)
    return o[:, :dv] / o[:, dv:]


def _attend_units(n_units, q_of, k_of, v_of, store):
    s = _scores(q_of(0), k_of(0))
    for n in range(n_units):
        s_cur = s
        if n + 1 < n_units:
            s = _scores(q_of(n + 1), k_of(n + 1))
        store(n, _softmax_pv(s_cur, v_of(n)).astype(BF16))


def _gqa_kernel(q_ref, *refs):
    kv_refs, (o_ref, kall_ref, vones_ref) = refs[:-3], refs[-3:]

    @pl.when(pl.program_id(2) == 0)
    def _():
        r0 = 0
        for k_ref, v_ref in zip(kv_refs[0::2], kv_refs[1::2]):
            rows = slice(r0, r0 + k_ref.shape[1])
            kall_ref[rows, :] = k_ref[0]
            vones_ref[rows, 0:HEAD_DIM] = v_ref[0]
            r0 += k_ref.shape[1]
        vones_ref[:, HEAD_DIM:2 * HEAD_DIM] = jnp.ones((vones_ref.shape[0], HEAD_DIM), BF16)

    k = kall_ref[...]
    group = N_Q_HEADS // N_KV_HEADS
    n_units = q_ref.shape[1] // ATT_ROWS * group

    def where(n):
        r, j = divmod(n, group)
        return slice(r * ATT_ROWS, (r + 1) * ATT_ROWS), slice(j * HEAD_DIM, (j + 1) * HEAD_DIM)

    def store(n, o):
        rows, cols = where(n)
        o_ref[0, rows, cols] = o

    def q_of(n):
        rows, cols = where(n)
        return q_ref[0, rows, cols]

    _attend_units(n_units, q_of, lambda n: k, lambda n: vones_ref[...], store)


def _gqa_attention(q, kv_parts, *, tq):
    b, l, _ = q.shape
    lk = sum(kv.shape[1] for kv in kv_parts)
    gw = ATTN_WIDTH // N_KV_HEADS
    in_specs = [pl.BlockSpec((1, tq, gw), lambda bi, h, i: (bi, i, h))]
    args = [q]
    for kv in kv_parts:
        in_specs += [pl.BlockSpec((1, kv.shape[1], HEAD_DIM), lambda bi, h, i: (bi, 0, h)),
                     pl.BlockSpec((1, kv.shape[1], HEAD_DIM), lambda bi, h, i: (bi, 0, N_KV_HEADS + h))]
        args += [kv, kv]
    return pl.pallas_call(
        _gqa_kernel,
        out_shape=jax.ShapeDtypeStruct((b, l, ATTN_WIDTH), BF16),
        grid=(b, N_KV_HEADS, l // tq),
        in_specs=in_specs,
        out_specs=pl.BlockSpec((1, tq, gw), lambda bi, h, i: (bi, i, h)),
        scratch_shapes=[pltpu.VMEM((lk, HEAD_DIM), BF16), pltpu.VMEM((lk, 2 * HEAD_DIM), BF16)],
        compiler_params=_cparams("parallel", "parallel", "arbitrary"),
        name="gqa_attention",
    )(*args)


MLA_HB = 4


def _mla_kernel(qn_ref, qp_ref, *refs):
    seg_refs, (o_ref, kcat_ref, vones_ref) = refs[:-3], refs[-3:]

    @pl.when(pl.program_id(2) == 0)
    def _():
        r0 = 0
        for k_ref, v_ref, kpe_ref in zip(seg_refs[0::3], seg_refs[1::3], seg_refs[2::3]):
            rows = slice(r0, r0 + k_ref.shape[1])
            for i in range(MLA_HB):
                kcat_ref[i, rows, 0:NOPE_DIM] = k_ref[0, :, i * NOPE_DIM:(i + 1) * NOPE_DIM]
                kcat_ref[i, rows, NOPE_DIM:2 * NOPE_DIM] = kpe_ref[0]
                vones_ref[i, rows, 0:V_DIM] = v_ref[0, :, i * V_DIM:(i + 1) * V_DIM]
            r0 += k_ref.shape[1]
        for i in range(MLA_HB):
            vones_ref[i, :, V_DIM:2 * V_DIM] = jnp.ones((vones_ref.shape[1], V_DIM), BF16)

    lane = lax.broadcasted_iota(jnp.int32, (ATT_ROWS, LANES), 1)
    per_block = LANES // ROPE_DIM
    n_units = qn_ref.shape[1] // ATT_ROWS * MLA_HB

    def q_of(n):
        r, i = divmod(n, MLA_HB)
        rows = slice(r * ATT_ROWS, (r + 1) * ATT_ROWS)
        blk, pos = divmod(i, per_block)
        qp = qp_ref[0, rows, blk * LANES:(blk + 1) * LANES]
        mine = (lane >= pos * ROPE_DIM) & (lane < (pos + 1) * ROPE_DIM)
        return jnp.concatenate(
            [qn_ref[0, rows, i * NOPE_DIM:(i + 1) * NOPE_DIM], jnp.where(mine, qp, jnp.zeros_like(qp))], axis=1)

    def store(n, o):
        r, i = divmod(n, MLA_HB)
        o_ref[0, r * ATT_ROWS:(r + 1) * ATT_ROWS, i * V_DIM:(i + 1) * V_DIM] = o

    _attend_units(n_units, q_of, lambda n: kcat_ref[n % MLA_HB], lambda n: vones_ref[n % MLA_HB], store)


def _mla_attention(qn, qp, segments, *, tq):
    b, l, _ = qn.shape
    lk = sum(kv.shape[1] for kv, _ in segments)
    wn = MLA_HB * NOPE_DIM
    nv0 = MLA_HEADS * NOPE_DIM // wn
    in_specs = [pl.BlockSpec((1, tq, wn), lambda bi, h, i: (bi, i, h)),
                pl.BlockSpec((1, tq, MLA_HB * ROPE_DIM), lambda bi, h, i: (bi, i, h))]
    args = [qn, qp]
    for kv, kpe in segments:
        lj = kv.shape[1]
        in_specs += [pl.BlockSpec((1, lj, wn), lambda bi, h, i: (bi, 0, h)),
                     pl.BlockSpec((1, lj, wn), lambda bi, h, i: (bi, 0, nv0 + h)),
                     pl.BlockSpec((1, lj, LANES), lambda bi, h, i: (bi, 0, 0))]
        args += [kv, kv, kpe]
    return pl.pallas_call(
        _mla_kernel,
        out_shape=jax.ShapeDtypeStruct((b, l, MLA_HEADS * V_DIM), BF16),
        grid=(b, MLA_HEADS // MLA_HB, l // tq),
        in_specs=in_specs,
        out_specs=pl.BlockSpec((1, tq, wn), lambda bi, h, i: (bi, i, h)),
        scratch_shapes=[pltpu.VMEM((MLA_HB, lk, 2 * NOPE_DIM), BF16), pltpu.VMEM((MLA_HB, lk, 2 * V_DIM), BF16)],
        compiler_params=_cparams("parallel", "parallel", "arbitrary"),
        name="mla_attention",
    )(*args)


CONV_ROWS = 64
CONV_COLS = 128


def _conv_kernel(ug_ref, w_ref, b_ref, lg_ref, lb_ref, o_ref, pad_ref, y_ref):
    l, c = ug_ref.shape[1], ug_ref.shape[2]
    pad_ref[0:CONV_PAD, :] = jnp.zeros((CONV_PAD, c), F32)
    pad_ref[CONV_PAD + l:2 * CONV_PAD + l, :] = jnp.zeros((CONV_PAD, c), F32)
    pad_ref[CONV_PAD:CONV_PAD + l, :] = ug_ref[0].astype(F32)
    off = CONV_PAD - CONV_K // 2
    nwin = CONV_ROWS + 2 * CONV_PAD

    def body(r, carry):
        r0 = pl.multiple_of(r * CONV_ROWS, CONV_ROWS)
        for cb in range(c // CONV_COLS):
            cs = slice(cb * CONV_COLS, (cb + 1) * CONV_COLS)
            win = pad_ref[pl.ds(r0, nwin), cs]
            acc = jnp.zeros((CONV_ROWS, CONV_COLS), F32) + b_ref[:, cs]
            for s in range(SUBLANES):
                ws = win if s == 0 else pltpu.roll(win, nwin - s, 0)
                for k in range(CONV_K):
                    if (off + k) % SUBLANES == s:
                        j = (off + k) // SUBLANES * SUBLANES
                        acc = acc + w_ref[k:k + 1, cs] * ws[j:j + CONV_ROWS, :]
            y_ref[:, cs] = acc
        y = y_ref[...]
        mu = jnp.mean(y, axis=-1, keepdims=True)
        yc = y - mu
        var = jnp.mean(yc * yc, axis=-1, keepdims=True)
        z = yc * lax.rsqrt(var + EPS) * lg_ref[...] + lb_ref[...]
        o_ref[0, pl.ds(r0, CONV_ROWS), :] = (z * _sigmoid(z)).astype(BF16)
        return carry

    lax.fori_loop(0, l // CONV_ROWS, body, 0)


def _conv_branch(ug, w, b, lg, lb):
    bsz, l, c = ug.shape
    return pl.pallas_call(
        _conv_kernel,
        out_shape=jax.ShapeDtypeStruct((bsz, l, c), BF16),
        grid=(bsz,),
        in_specs=[pl.BlockSpec((1, l, c), lambda bi: (bi, 0, 0)),
                  _full_spec(w), _full_spec(b), _full_spec(lg), _full_spec(lb)],
        out_specs=pl.BlockSpec((1, l, c), lambda bi: (bi, 0, 0)),
        scratch_shapes=[pltpu.VMEM((l + 2 * CONV_PAD, c), F32), pltpu.VMEM((CONV_ROWS, c), F32)],
        compiler_params=_cparams("parallel"),
        name="conv_branch",
    )(ug, w, b, lg, lb)


OUTPROJ_ROWS = 256


def _outproj_kernel(*refs, n_in):
    x_ref = refs[0]
    a_refs = refs[1:1 + n_in]
    w_refs = refs[1 + n_in:1 + 2 * n_in]
    ga_ref, g2_ref, sh2_ref, sc2_ref, r_ref, xo_ref, h2_ref, lg_ref = refs[1 + 2 * n_in:]
    for r0 in range(0, x_ref.shape[1], OUTPROJ_ROWS):
        rows = slice(r0, r0 + OUTPROJ_ROWS)
        acc = _dot(a_refs[0][0, rows, :], w_refs[0][...])
        for a_ref, w_ref in zip(a_refs[1:], w_refs[1:]):
            acc = acc + _dot(a_ref[0, rows, :], w_ref[...])
        xn = x_ref[0, rows, :] + ga_ref[0] * acc
        xo_ref[0, rows, :] = xn
        h2 = _prenorm(xn, g2_ref[...], sh2_ref[0], sc2_ref[0]).astype(BF16)
        h2_ref[0, rows, :] = h2
        lg_ref[0, rows, :] = _dot(h2, r_ref[...])


def _outproj(x, acts, w, ga, g2, sh2, sc2, router, *, tm):
    b, l, d = x.shape
    n_in = len(acts)
    row = lambda bi, i: (bi, i, 0)
    in_specs = [pl.BlockSpec((1, tm, d), row)]
    in_specs += [pl.BlockSpec((1, tm, a.shape[2]), row) for a in acts]
    in_specs += [pl.BlockSpec((acts[r].shape[2], w.shape[1]), lambda bi, i, r=r: (r, 0)) for r in range(n_in)]
    in_specs += [_mod_spec(ga), _full_spec(g2), _mod_spec(sh2), _mod_spec(sc2), _full_spec(router)]
    return pl.pallas_call(
        functools.partial(_outproj_kernel, n_in=n_in),
        out_shape=(jax.ShapeDtypeStruct((b, l, d), F32),
                   jax.ShapeDtypeStruct((b, l, d), BF16),
                   jax.ShapeDtypeStruct((b, l, LANES), F32)),
        grid=(b, l // tm),
        in_specs=in_specs,
        out_specs=(pl.BlockSpec((1, tm, d), row), pl.BlockSpec((1, tm, d), row),
                   pl.BlockSpec((1, tm, LANES), row)),
        compiler_params=_cparams("parallel", "parallel"),
        name="outproj",
    )(x, *acts, *([w] * n_in), ga, g2, sh2, sc2, router)


MOE_TM = 256
MOE_CHUNKS = 8


def _moe_kernel(*refs, tiles_a, n_chunks):
    if tiles_a is None:
        xs_ref, xs2_ref = refs[0], None
        refs = refs[1:]
    else:
        xs_ref, xs2_ref = refs[:2]
        refs = refs[2:]
    wg_ref, wu_ref, wd_ref, o_ref, wgb_ref, wub_ref, wdb_ref = refs
    p = pl.program_id(0)
    i = pl.program_id(1)
    n_experts = pl.num_programs(0) - 1
    cg = wg_ref.shape[1]
    cd = wd_ref.shape[1]

    @pl.when((p < n_experts) & (i < n_chunks))
    def _():
        slot = p % 2
        rg = pl.multiple_of(i * cg, cg)
        rd = pl.multiple_of(i * cd, cd)
        wgb_ref[slot, pl.ds(rg, cg), :] = wg_ref[0].astype(BF16)
        wub_ref[slot, pl.ds(rg, cg), :] = wu_ref[0].astype(BF16)
        wdb_ref[slot, pl.ds(rd, cd), :] = wd_ref[0].astype(BF16)

    @pl.when(p == 0)
    def _():
        o_ref[...] = jnp.zeros_like(o_ref)

    @pl.when(p > 0)
    def _():
        slot = (p + 1) % 2
        x = xs_ref[0]
        if tiles_a is not None:
            x = jnp.where(i < tiles_a, x, xs2_ref[0])
        g = _dot(x, wgb_ref[slot])
        u = _dot(x, wub_ref[slot])
        hid = (g * _sigmoid(g) * u).astype(BF16)
        o_ref[0] = _dot(hid, wdb_ref[slot]).astype(BF16)


def _moe_ffn(xs, wg, wu, wd, xs2=None):
    e, m, d = xs.shape
    f = wg.shape[2]
    tiles_a = m // MOE_TM
    n_chunks = min(MOE_CHUNKS, tiles_a)
    assert m % MOE_TM == 0 and d % (n_chunks * 16) == 0 and f % (n_chunks * 16) == 0
    last_e, last_c, last_a = e - 1, n_chunks - 1, tiles_a - 1
    rows = lambda p, i: (jnp.maximum(p - 1, 0), jnp.minimum(i, last_a), 0)
    chunk = lambda p, i: (jnp.minimum(p, last_e), jnp.minimum(i, last_c), 0)
    args = [xs]
    in_specs = [pl.BlockSpec((1, MOE_TM, d), rows)]
    if xs2 is not None:
        assert xs2.shape == (e, MOE_TM, d)
        args.append(xs2)
        in_specs.append(pl.BlockSpec((1, MOE_TM, d), lambda p, i: (jnp.maximum(p - 1, 0), 0, 0)))
    tiles = tiles_a + (xs2 is not None)
    in_specs += [pl.BlockSpec((1, d // n_chunks, f), chunk),
                 pl.BlockSpec((1, d // n_chunks, f), chunk),
                 pl.BlockSpec((1, f // n_chunks, d), chunk)]
    return pl.pallas_call(
        functools.partial(_moe_kernel, tiles_a=tiles_a if xs2 is not None else None, n_chunks=n_chunks),
        out_shape=jax.ShapeDtypeStruct((e + 1, tiles * MOE_TM, d), BF16),
        grid=(e + 1, tiles),
        in_specs=in_specs,
        out_specs=pl.BlockSpec((1, MOE_TM, d), lambda p, i: (p, i, 0)),
        scratch_shapes=[pltpu.VMEM((2, d, f), BF16), pltpu.VMEM((2, d, f), BF16), pltpu.VMEM((2, f, d), BF16)],
        compiler_params=_cparams("arbitrary", "arbitrary"),
        name="moe_ffn",
    )(*args, wg, wu, wd)


COMBINE_TT = 256
COMBINE_W = 256
COMBINE_KW = 1024


def _combine_kernel(p0_ref, tok_ref, gate_ref, z_ref, x_ref, ga_ref, g_ref, *rest, final_norm):
    o_ref, acc_ref = rest[-2:]
    bi = pl.program_id(0)
    ti = pl.program_id(1)
    p0 = p0_ref[bi, ti]
    p1 = p0_ref[bi, ti + 1]
    n_pairs = z_ref.shape[1]
    kw = min(COMBINE_KW, n_pairs)
    tok0 = ti * COMBINE_TT

    def window(start, width):
        tok_row = tok0 + lax.broadcasted_iota(jnp.int32, (COMBINE_TT, width), 0)
        toks = tok_ref[0, :, pl.ds(start, width)]
        gates = gate_ref[0, :, pl.ds(start, width)]
        weights = jnp.where(toks == tok_row, gates, 0.0).astype(BF16)
        return _dot(weights, z_ref[0, pl.ds(start, width), :])

    first = jnp.minimum(p0 // COMBINE_W * COMBINE_W, n_pairs - kw)
    fits = p1 <= first + kw

    @pl.when(fits)
    def _():
        acc_ref[...] = window(pl.multiple_of(first, COMBINE_W), kw)

    @pl.when(jnp.logical_not(fits))
    def _():
        acc_ref[...] = jnp.zeros_like(acc_ref)

        def body(w, carry):
            acc_ref[...] += window(pl.multiple_of(w * COMBINE_W, COMBINE_W), COMBINE_W)
            return carry

        lax.fori_loop(p0 // COMBINE_W, (p1 + COMBINE_W - 1) // COMBINE_W, body, 0)

    x = x_ref[0] + ga_ref[0] * acc_ref[...]
    if final_norm:
        ms = jnp.mean(x * x, axis=-1, keepdims=True)
        x = x * lax.rsqrt(ms + EPS) * g_ref[...]
    o_ref[0] = x


def _combine(x, ga, g, tok_sorted, gate_sorted, z, p0, *, final_norm, b0=0, prev=None):
    b, n, d = x.shape
    nb, p, _ = z.shape
    row = lambda bi, ti, *_: (b0 + bi, ti, 0)
    args = [p0, tok_sorted, gate_sorted, z, x, ga, g]
    in_specs = [pl.BlockSpec((1, 1, p), lambda bi, ti, *_: (bi, 0, 0)),
                pl.BlockSpec((1, 1, p), lambda bi, ti, *_: (bi, 0, 0)),
                pl.BlockSpec((1, p, d), lambda bi, ti, *_: (bi, 0, 0)),
                pl.BlockSpec((1, COMBINE_TT, d), row), _mod_spec(ga, b0), _full_spec(g)]
    aliases = {}
    if prev is not None:
        aliases = {len(args): 0}
        args.append(prev)
        in_specs.append(pl.BlockSpec(memory_space=pl.ANY))
    grid_spec = pltpu.PrefetchScalarGridSpec(
        num_scalar_prefetch=1,
        grid=(nb, n // COMBINE_TT),
        in_specs=in_specs,
        out_specs=pl.BlockSpec((1, COMBINE_TT, d), row),
        scratch_shapes=[pltpu.VMEM((COMBINE_TT, d), F32)])
    return pl.pallas_call(
        functools.partial(_combine_kernel, final_norm=final_norm),
        out_shape=jax.ShapeDtypeStruct((b, n, d), F32),
        grid_spec=grid_spec,
        input_output_aliases=aliases,
        compiler_params=_cparams("parallel", "arbitrary"),
        name="moe_combine",
    )(*args)


ONEHOT_GATHER_MAX_TOKENS = 256


def _onehot_gather_kernel(idx_ref, h_ref, o_ref):
    s, n = idx_ref.shape[1], h_ref.shape[1]
    tok = lax.broadcasted_iota(jnp.int32, (s, n), 1)
    onehot = jnp.where(idx_ref[0] == tok, 1.0, 0.0).astype(BF16)
    o_ref[...] = _dot(onehot, h_ref[0]).reshape(o_ref.shape).astype(o_ref.dtype)


def _onehot_gather(h2, idx):
    b, n, d = h2.shape
    _, e, cap = idx.shape
    return pl.pallas_call(
        _onehot_gather_kernel,
        out_shape=jax.ShapeDtypeStruct((e, b * cap, d), h2.dtype),
        grid=(b,),
        in_specs=[pl.BlockSpec((1, e * cap, 1), lambda bi: (bi, 0, 0)),
                  pl.BlockSpec((1, n, d), lambda bi: (bi, 0, 0))],
        out_specs=pl.BlockSpec((e, cap, d), lambda bi: (0, bi, 0)),
        compiler_params=_cparams("parallel"),
        name="onehot_gather",
    )(idx.reshape(b, e * cap, 1), h2)


def _select(logits):
    n = logits.shape[1]
    cap = EC_FACTOR * n // N_EXPERTS
    aff = jax.nn.softmax(logits[..., :N_EXPERTS], axis=-1)
    return lax.top_k(jnp.swapaxes(aff, 1, 2), cap)


def _routed_rows(h2, idx, b0, nb):
    d = h2.shape[2]
    cap = idx.shape[2]
    idx = idx[b0:b0 + nb]
    if h2.shape[1] <= ONEHOT_GATHER_MAX_TOKENS and (b0, nb) == (0, h2.shape[0]):
        return _onehot_gather(h2, idx)
    bidx = b0 + jnp.arange(nb, dtype=idx.dtype)[None, :, None]
    return h2[bidx, jnp.swapaxes(idx, 0, 1)].reshape(N_EXPERTS, nb * cap, d)


def _combine_routed(x, ga, g, gate, idx, ys, row0, *, final_norm, b0=0, prev=None):
    _, n, d = x.shape
    b = idx.shape[0]
    cap = idx.shape[2]
    pairs = N_EXPERTS * cap
    m = ys.shape[1]
    slots = lax.broadcasted_iota(jnp.int32, (b, pairs), 1)
    tok_sorted, order, gate_sorted = lax.sort((idx.reshape(b, pairs), slots, gate.reshape(b, pairs)),
                                              dimension=1, num_keys=1)
    flat = (order // cap + 1) * m + row0 + jnp.arange(b, dtype=jnp.int32)[:, None] * cap + order % cap
    z = ys.reshape((N_EXPERTS + 1) * m, d)[flat]
    bounds = jnp.arange(n // COMBINE_TT + 1, dtype=jnp.int32) * COMBINE_TT
    p0 = jnp.sum(tok_sorted[:, None, :] < bounds[None, :, None], axis=-1).astype(jnp.int32)
    return _combine(x, ga, g, tok_sorted.reshape(b, 1, pairs), gate_sorted.reshape(b, 1, pairs), z, p0,
                    final_norm=final_norm, b0=b0, prev=prev)


def _ec_moe(x, ga, g, h2, logits, wg, wu, wd, *, final_norm, ffn_halves, extra=None):
    bsz = x.shape[0]
    nb = bsz // 2
    gate, idx = _select(logits)
    cap = idx.shape[2]
    xs2 = None
    if extra is not None:
        ex, ega, eh2, elogits = extra
        egate, eidx = _select(elogits)
        xs2 = _routed_rows(eh2, eidx, 0, eh2.shape[0])
    if ffn_halves == 2:
        ys = [_moe_ffn(_routed_rows(h2, idx, 0, nb), wg, wu, wd, xs2),
              _moe_ffn(_routed_rows(h2, idx, nb, nb), wg, wu, wd)]
        row0 = [0, 0]
        extra_row0 = nb * cap
    else:
        ys = [_moe_ffn(_routed_rows(h2, idx, 0, bsz), wg, wu, wd, xs2)] * 2
        row0 = [0, nb * cap]
        extra_row0 = bsz * cap
    out = _combine_routed(x, ga, g, gate[:nb], idx[:nb], ys[0], row0[0], final_norm=final_norm)
    out = _combine_routed(x, ga, g, gate[nb:], idx[nb:], ys[1], row0[1], final_norm=final_norm, b0=nb, prev=out)
    if extra is None:
        return out
    return out, _combine_routed(ex, ega, g, egate, eidx, ys[0], extra_row0, final_norm=final_norm)


def _rope_quarter(y, cos, sin, first):
    rot = jnp.where(first, pltpu.roll(y, LANES - ROPE_DIM // 2, 1), pltpu.roll(y, ROPE_DIM // 2, 1))
    return y * cos + rot * sin


def _l1_dqkv_kernel(x_ref, g_ref, sh_ref, sc_ref, w_ref, qg_ref, kvg_ref, cos_ref, sin_ref, *out_refs,
                    with_q, rope):
    if with_q:
        cq_ref, ckv_ref, kpe_ref = out_refs
    else:
        ckv_ref, kpe_ref = out_refs
    h = _prenorm(x_ref[0], g_ref[...], sh_ref[0], sc_ref[0]).astype(BF16)
    if with_q:
        cq = _dot(h, w_ref[:, 0:Q_LORA])
        ms = jnp.mean(cq * cq, axis=-1, keepdims=True)
        cq_ref[0] = (cq * lax.rsqrt(ms + EPS) * qg_ref[...]).astype(BF16)
    z = _dot(h, w_ref[:, Q_LORA:Q_LORA + KV_LORA + LANES])
    ckv = z[:, 0:KV_LORA]
    ms = jnp.mean(ckv * ckv, axis=-1, keepdims=True)
    ckv_ref[0] = (ckv * lax.rsqrt(ms + EPS) * kvg_ref[...]).astype(BF16)
    pe = z[:, KV_LORA:KV_LORA + LANES]
    if rope:
        lane = lax.broadcasted_iota(jnp.int32, pe.shape, 1)
        pe = _rope_quarter(pe, cos_ref[...], sin_ref[...], (lane % ROPE_DIM) < ROPE_DIM // 2)
    kpe_ref[0] = (pe + pltpu.roll(pe, ROPE_DIM, 1)).astype(BF16)


def _l1_dqkv(x, g, sh, sc, w, qg, kvg, cos, sin, *, rope, with_q, tm):
    b, l, d = x.shape
    row = lambda bi, i: (bi, i, 0)
    args = [x, g, sh, sc, w, qg, kvg, cos, sin]
    in_specs = [pl.BlockSpec((1, tm, d), row), _full_spec(g), _mod_spec(sh), _mod_spec(sc), _full_spec(w),
                _full_spec(qg), _full_spec(kvg),
                pl.BlockSpec((tm, LANES), lambda bi, i: (i, 0)), pl.BlockSpec((tm, LANES), lambda bi, i: (i, 0))]
    out_shape, out_specs = [], []
    if with_q:
        out_shape.append(jax.ShapeDtypeStruct((b, l, Q_LORA), BF16))
        out_specs.append(pl.BlockSpec((1, tm, Q_LORA), row))
    out_shape += [jax.ShapeDtypeStruct((b, l, KV_LORA), BF16), jax.ShapeDtypeStruct((b, l, LANES), BF16)]
    out_specs += [pl.BlockSpec((1, tm, KV_LORA), row), pl.BlockSpec((1, tm, LANES), row)]

    return pl.pallas_call(
        functools.partial(_l1_dqkv_kernel, with_q=with_q, rope=rope),
        out_shape=tuple(out_shape),
        grid=(b, l // tm),
        in_specs=in_specs,
        out_specs=tuple(out_specs),
        compiler_params=_cparams("parallel", "parallel"),
        name="l1_dqkv",
    )(*args)


def _l1_q_kernel(cq_ref, w_ref, cos_ref, sin_ref, qn_ref, qp_ref):
    a = cq_ref[0]
    cw = 4 * LANES
    n_nope = MLA_HEADS * NOPE_DIM
    qscale = MLA_SCALE * LOG2E
    for c in range(n_nope // cw):
        qn_ref[0, :, c * cw:(c + 1) * cw] = (_dot(a, w_ref[:, c * cw:(c + 1) * cw]) * qscale).astype(BF16)
    cos = cos_ref[...]
    sin = sin_ref[...]
    lane = lax.broadcasted_iota(jnp.int32, cos.shape, 1)
    first = (lane % ROPE_DIM) < ROPE_DIM // 2
    for c in range(MLA_HEADS * ROPE_DIM // cw):
        y = _dot(a, w_ref[:, n_nope + c * cw:n_nope + (c + 1) * cw])
        parts = [_rope_quarter(y[:, j * LANES:(j + 1) * LANES], cos, sin, first) * qscale for j in range(4)]
        qp_ref[0, :, c * cw:(c + 1) * cw] = jnp.concatenate(parts, axis=1).astype(BF16)


def _l1_q(cq, w, cos, sin, *, tm):
    b, l, k = cq.shape
    row = lambda bi, i: (bi, i, 0)
    return pl.pallas_call(
        _l1_q_kernel,
        out_shape=(jax.ShapeDtypeStruct((b, l, MLA_HEADS * NOPE_DIM), BF16),
                   jax.ShapeDtypeStruct((b, l, MLA_HEADS * ROPE_DIM), BF16)),
        grid=(b, l // tm),
        in_specs=[pl.BlockSpec((1, tm, k), row), _full_spec(w),
                  pl.BlockSpec((tm, LANES), lambda bi, i: (i, 0)), pl.BlockSpec((tm, LANES), lambda bi, i: (i, 0))],
        out_specs=(pl.BlockSpec((1, tm, MLA_HEADS * NOPE_DIM), row),
                   pl.BlockSpec((1, tm, MLA_HEADS * ROPE_DIM), row)),
        compiler_params=_cparams("parallel", "parallel"),
        name="l1_q",
    )(cq, w, cos, sin)


def _mm_kernel(a_ref, w_ref, o_ref):
    a = a_ref[0]
    cw = 4 * LANES
    for c in range(w_ref.shape[1] // cw):
        o_ref[0, :, c * cw:(c + 1) * cw] = _dot(a, w_ref[:, c * cw:(c + 1) * cw]).astype(o_ref.dtype)


def _mm(a, w, *, tm):
    b, l, k = a.shape
    n = w.shape[1]
    row = lambda bi, i: (bi, i, 0)
    return pl.pallas_call(
        _mm_kernel,
        out_shape=jax.ShapeDtypeStruct((b, l, n), BF16),
        grid=(b, l // tm),
        in_specs=[pl.BlockSpec((1, tm, k), row), _full_spec(w)],
        out_specs=pl.BlockSpec((1, tm, n), row),
        compiler_params=_cparams("parallel", "parallel"),
        name="l1_kv_up",
    )(a, w)


def _rope_tables(n_tokens, d_rot):
    rows = n_tokens // GRID_W
    row = jnp.repeat(jnp.arange(rows, dtype=F32), GRID_W)
    col = jnp.tile(jnp.arange(GRID_W, dtype=F32), rows)
    n_axis = d_rot // 4
    inv_freq = ROPE_THETA ** (-jnp.arange(n_axis, dtype=F32) / n_axis)
    ang = jnp.concatenate([row[:, None] * inv_freq, col[:, None] * inv_freq], axis=-1)
    cos, sin = jnp.cos(ang), jnp.sin(ang)
    cos_t = jnp.concatenate([cos, cos], axis=-1)
    sin_t = jnp.concatenate([-sin, sin], axis=-1)
    reps = LANES // d_rot
    return jnp.tile(cos_t, (1, reps)), jnp.tile(sin_t, (1, reps))


def _mods(mod, lo, hi):
    return [mod[lo:hi, None, i * D_MODEL:(i + 1) * D_MODEL] for i in range(6)]


def _pad_router(router):
    return jnp.pad(router, ((0, 0), (0, LANES - router.shape[1]))).astype(BF16)


def kernel(x, c, ctx, c_ctx, l0_mod_w, l0_mod_b, l0_norm1_g, l0_w_in, l0_q_norm_g, l0_k_norm_g, l0_dw_w, l0_dw_b, l0_conv_ln_g, l0_conv_ln_b, l0_w_out, l0_norm2_g, l0_router, l0_w_gate, l0_w_up, l0_w_down, l1_mod_w, l1_mod_b, l1_norm1_g, l1_w_dqkv, l1_q_lora_norm_g, l1_w_uq, l1_kv_lora_norm_g, l1_w_ukv, l1_w_out, l1_norm2_g, l1_router, l1_w_gate, l1_w_up, l1_w_down, final_norm_g):
    bsz, seq, d = x.shape
    n_ctx = ctx.shape[1]
    row2 = lambda v: v.reshape(1, -1)

    cond = jnp.zeros((16, d), F32).at[:bsz].set(c).at[bsz].set(c_ctx)
    mod0 = _ada(cond, l0_mod_w, l0_mod_b)
    mod1 = _ada(cond, l1_mod_w, l1_mod_b)

    sh1, sc1, ga1, sh2, sc2, ga2 = _mods(mod0, 0, bsz)
    csh1, csc1, cga1, csh2, csc2, cga2 = _mods(mod0, bsz, bsz + 1)
    s_q, s_k, s_v, s_u = ATTN_WIDTH, ATTN_WIDTH + KV_WIDTH, ATTN_WIDTH + 2 * KV_WIDTH, ATTN_WIDTH + 2 * KV_WIDTH + CONV_WIDTH
    w_in = jnp.concatenate([l0_w_in[:, :s_q], l0_w_in[:, s_v:s_u], l0_w_in[:, s_u:],
                            l0_w_in[:, s_q:s_k], l0_w_in[:, s_k:s_v]], axis=1).astype(BF16)
    cos0, sin0 = _rope_tables(seq, HEAD_DIM)
    g1 = row2(l0_norm1_g)
    qg, kg = row2(l0_q_norm_g), row2(l0_k_norm_g)
    q_l, ug_l, kv_l = _l0_inproj(x, g1, sh1, sc1, w_in, qg, kg, cos0, sin0, rope=True, tm=512)
    q_c, ug_c, kv_c = _l0_inproj(ctx, g1, csh1, csc1, w_in, qg, kg, cos0, sin0, rope=False, tm=n_ctx)
    a_l = _gqa_attention(q_l, [kv_c, kv_l], tq=1024)
    dw_w = l0_dw_w.reshape(CONV_K, CONV_WIDTH)
    dw_b, ln_g, ln_b = row2(l0_dw_b), row2(l0_conv_ln_g), row2(l0_conv_ln_b)
    cb_l = _conv_branch(ug_l, dw_w, dw_b, ln_g, ln_b)
    w_out0 = l0_w_out.astype(BF16)
    g2 = row2(l0_norm2_g)
    router0 = _pad_router(l0_router)
    x1, h2_l, lg_l = _outproj(x, [a_l, cb_l], w_out0, ga1, g2, sh2, sc2, router0, tm=512)
    a_c = _gqa_attention(q_c, [kv_c], tq=n_ctx)
    cb_c = _conv_branch(ug_c, dw_w, dw_b, ln_g, ln_b)
    xc1, h2_c, lg_c = _outproj(ctx, [a_c, cb_c], w_out0, cga1, g2, csh2, csc2, router0, tm=n_ctx)
    x2, xc2 = _ec_moe(x1, ga2, g2, h2_l, lg_l, l0_w_gate, l0_w_up, l0_w_down, final_norm=False, ffn_halves=1,
                      extra=(xc1, cga2, h2_c, lg_c))

    sh1, sc1, ga1, sh2, sc2, ga2_1 = _mods(mod1, 0, bsz)
    csh1, csc1 = _mods(mod1, bsz, bsz + 1)[:2]
    w_dqkv = jnp.pad(l1_w_dqkv, ((0, 0), (0, LANES - ROPE_DIM))).astype(BF16)
    cos1, sin1 = _rope_tables(seq, ROPE_DIM)
    g1 = row2(l1_norm1_g)
    qlg, kvlg = row2(l1_q_lora_norm_g), row2(l1_kv_lora_norm_g)
    cq, ckv_l, kpe_l = _l1_dqkv(x2, g1, sh1, sc1, w_dqkv, qlg, kvlg, cos1, sin1, rope=True, with_q=True, tm=512)
    ckv_c, kpe_c = _l1_dqkv(xc2, g1, csh1, csc1, w_dqkv, qlg, kvlg, cos1, sin1, rope=False, with_q=False, tm=n_ctx)
    w_uq = l1_w_uq.reshape(Q_LORA, MLA_HEADS, QK_DIM)
    w_uq = jnp.concatenate([w_uq[:, :, :NOPE_DIM].reshape(Q_LORA, -1),
                            w_uq[:, :, NOPE_DIM:].reshape(Q_LORA, -1)], axis=1).astype(BF16)
    w_ukv = l1_w_ukv.reshape(KV_LORA, MLA_HEADS, NOPE_DIM + V_DIM)
    w_ukv = jnp.concatenate([w_ukv[:, :, :NOPE_DIM].reshape(KV_LORA, -1),
                             w_ukv[:, :, NOPE_DIM:].reshape(KV_LORA, -1)], axis=1).astype(BF16)
    qn, qp = _l1_q(cq, w_uq, cos1, sin1, tm=512)
    kv_c1 = _mm(ckv_c, w_ukv, tm=n_ctx)
    kv_l1 = _mm(ckv_l, w_ukv, tm=1024)
    a1 = _mla_attention(qn, qp, [(kv_c1, kpe_c), (kv_l1, kpe_l)], tq=1024)
    x3, h2, lg = _outproj(x2, [a1], l1_w_out.astype(BF16), ga1, row2(l1_norm2_g), sh2, sc2,
                          _pad_router(l1_router), tm=512)
    return _ec_moe(x3, ga2_1, row2(final_norm_g), h2, lg, l1_w_gate, l1_w_up, l1_w_down, final_norm=True,
                   ffn_halves=2)
```

```python
import functools

import jax
import jax.numpy as jnp
from jax import lax
from jax.experimental import pallas as pl
from jax.experimental.pallas import tpu as pltpu

F32 = jnp.float32
BF16 = jnp.bfloat16

D_MODEL = 2048
GRID_W = 64
EPS = 1e-6
ROPE_THETA = 10000.0
HEAD_DIM = 128
N_Q_HEADS = 8
N_KV_HEADS = 2
ATTN_WIDTH = N_Q_HEADS * HEAD_DIM
KV_WIDTH = N_KV_HEADS * HEAD_DIM
CONV_WIDTH = D_MODEL - ATTN_WIDTH
CONV_K = 31
LOG2E = 1.4426950408889634
GQA_SCALE = HEAD_DIM ** -0.5
MLA_HEADS = 16
Q_LORA = 1536
KV_LORA = 512
NOPE_DIM = 128
ROPE_DIM = 64
V_DIM = 128
QK_DIM = NOPE_DIM + ROPE_DIM
MLA_SCALE = QK_DIM ** -0.5
N_EXPERTS = 16
D_EXPERT = 1024
EC_FACTOR = 2

LANES = 128
SUBLANES = 8
VMEM_LIMIT = 56 * 1024 * 1024
CONV_PAD = 16


def _cparams(*sem):
    return pltpu.CompilerParams(dimension_semantics=sem, vmem_limit_bytes=VMEM_LIMIT)


def _dot(a, b):
    return jnp.dot(a, b, preferred_element_type=F32)


def _sigmoid(x):
    return 1.0 / (1.0 + jnp.exp(-x))


def _prenorm(x, g, sh, sc):
    ms = jnp.mean(x * x, axis=-1, keepdims=True)
    y = x * lax.rsqrt(ms + EPS) * g
    return y * (1.0 + sc) + sh


def _mod_spec(arr, b0=0):
    if arr.shape[0] > 1:
        return pl.BlockSpec((1, 1, arr.shape[2]), lambda b, *_: (b0 + b, 0, 0))
    return pl.BlockSpec((1, 1, arr.shape[2]), lambda b, *_: (0, 0, 0))


def _full_spec(arr):
    nd = arr.ndim
    return pl.BlockSpec(arr.shape, lambda *_: (0,) * nd)


def _ada_kernel(c_ref, w_ref, b_ref, o_ref):
    c = c_ref[...]
    s = c * _sigmoid(c)
    s_hi = s.astype(BF16)
    s_lo = (s - s_hi.astype(F32)).astype(BF16)
    w = w_ref[...]
    w_hi = w.astype(BF16)
    w_lo = (w - w_hi.astype(F32)).astype(BF16)
    o_ref[...] = _dot(s_hi, w_hi) + _dot(s_lo, w_hi) + _dot(s_hi, w_lo) + b_ref[...]


def _ada(cond, w, b):
    m, d = cond.shape
    n = w.shape[1]
    tn = 512
    return pl.pallas_call(
        _ada_kernel,
        out_shape=jax.ShapeDtypeStruct((m, n), F32),
        grid=(n // tn,),
        in_specs=[pl.BlockSpec((m, d), lambda j: (0, 0)),
                  pl.BlockSpec((d, tn), lambda j: (0, j)),
                  pl.BlockSpec((1, tn), lambda j: (0, j))],
        out_specs=pl.BlockSpec((m, tn), lambda j: (0, j)),
        compiler_params=_cparams("parallel"),
        name="ada_params",
    )(cond, w, b.reshape(1, n))


def _rope_half(y, cos, sin):
    return y * cos + pltpu.roll(y, HEAD_DIM // 2, 1) * sin


def _l0_inproj_kernel(x_ref, g_ref, sh_ref, sc_ref, w_ref, qg_ref, kg_ref, cos_ref, sin_ref,
                      q_ref, ug_ref, kv_ref, *, rope):
    h = _prenorm(x_ref[0], g_ref[...], sh_ref[0], sc_ref[0]).astype(BF16)
    cos = cos_ref[...]
    sin = sin_ref[...]

    def head(y, gain, scale):
        ms = jnp.mean(y * y, axis=-1, keepdims=True)
        y = y * lax.rsqrt(ms + EPS) * gain
        if rope:
            y = _rope_half(y, cos, sin)
        return y * scale if scale != 1.0 else y

    cw = 4 * HEAD_DIM
    for c in range(ATTN_WIDTH // cw):
        y = _dot(h, w_ref[:, c * cw:(c + 1) * cw])
        parts = [head(y[:, j * HEAD_DIM:(j + 1) * HEAD_DIM], qg_ref[...], GQA_SCALE * LOG2E) for j in range(4)]
        q_ref[0, :, c * cw:(c + 1) * cw] = jnp.concatenate(parts, axis=1).astype(BF16)
    u0 = ATTN_WIDTH
    g0 = ATTN_WIDTH + CONV_WIDTH
    for c in range(CONV_WIDTH // cw):
        u = _dot(h, w_ref[:, u0 + c * cw:u0 + (c + 1) * cw])
        gt = _dot(h, w_ref[:, g0 + c * cw:g0 + (c + 1) * cw])
        ug_ref[0, :, c * cw:(c + 1) * cw] = (u * _sigmoid(gt)).astype(BF16)
    k0 = ATTN_WIDTH + 2 * CONV_WIDTH
    y = _dot(h, w_ref[:, k0:k0 + 2 * KV_WIDTH])
    parts = [head(y[:, j * HEAD_DIM:(j + 1) * HEAD_DIM], kg_ref[...], 1.0) for j in range(N_KV_HEADS)]
    parts.append(y[:, KV_WIDTH:])
    kv_ref[0] = jnp.concatenate(parts, axis=1).astype(BF16)


def _l0_inproj(x, g, sh, sc, w, qg, kg, cos, sin, *, rope, tm):
    b, l, d = x.shape
    n = w.shape[1]
    row = lambda bi, i: (bi, i, 0)
    return pl.pallas_call(
        functools.partial(_l0_inproj_kernel, rope=rope),
        out_shape=(jax.ShapeDtypeStruct((b, l, ATTN_WIDTH), BF16),
                   jax.ShapeDtypeStruct((b, l, CONV_WIDTH), BF16),
                   jax.ShapeDtypeStruct((b, l, 2 * KV_WIDTH), BF16)),
        grid=(b, l // tm),
        in_specs=[pl.BlockSpec((1, tm, d), row), _full_spec(g), _mod_spec(sh), _mod_spec(sc),
                  _full_spec(w), _full_spec(qg), _full_spec(kg),
                  pl.BlockSpec((tm, LANES), lambda bi, i: (i, 0)),
                  pl.BlockSpec((tm, LANES), lambda bi, i: (i, 0))],
        out_specs=(pl.BlockSpec((1, tm, ATTN_WIDTH), row),
                   pl.BlockSpec((1, tm, CONV_WIDTH), row),
                   pl.BlockSpec((1, tm, 2 * KV_WIDTH), row)),
        compiler_params=_cparams("parallel", "parallel"),
        name="l0_inproj",
    )(x, g, sh, sc, w, qg, kg, cos, sin)


ATT_ROWS = 256


def _scores(q, k):
    return lax.dot_general(q, k, (((1,), (1,)), ((), ())), preferred_element_type=F32)


def _softmax_pv(s, v_ones):
    dv = v_ones.shape[1] // 2
    m = jnp.max(s, axis=-1, keepdims=True)
    o = _dot(jnp.exp2(s - m).astype(BF16), v_ones)
    return o[:, :dv] / o[:, dv:]


def _attend_units(n_units, q_of, k_of, v_of, store):
    s = _scores(q_of(0), k_of(0))
    for n in range(n_units):
        s_cur = s
        if n + 1 < n_units:
            s = _scores(q_of(n + 1), k_of(n + 1))
        store(n, _softmax_pv(s_cur, v_of(n)).astype(BF16))


def _gqa_kernel(q_ref, *refs):
    kv_refs, (o_ref, kall_ref, vones_ref) = refs[:-3], refs[-3:]

    @pl.when(pl.program_id(2) == 0)
    def _():
        r0 = 0
        for k_ref, v_ref in zip(kv_refs[0::2], kv_refs[1::2]):
            rows = slice(r0, r0 + k_ref.shape[1])
            kall_ref[rows, :] = k_ref[0]
            vones_ref[rows, 0:HEAD_DIM] = v_ref[0]
            r0 += k_ref.shape[1]
        vones_ref[:, HEAD_DIM:2 * HEAD_DIM] = jnp.ones((vones_ref.shape[0], HEAD_DIM), BF16)

    k = kall_ref[...]
    group = N_Q_HEADS // N_KV_HEADS
    n_units = q_ref.shape[1] // ATT_ROWS * group

    def where(n):
        r, j = divmod(n, group)
        return slice(r * ATT_ROWS, (r + 1) * ATT_ROWS), slice(j * HEAD_DIM, (j + 1) * HEAD_DIM)

    def store(n, o):
        rows, cols = where(n)
        o_ref[0, rows, cols] = o

    def q_of(n):
        rows, cols = where(n)
        return q_ref[0, rows, cols]

    _attend_units(n_units, q_of, lambda n: k, lambda n: vones_ref[...], store)


def _gqa_attention(q, kv_parts, *, tq):
    b, l, _ = q.shape
    lk = sum(kv.shape[1] for kv in kv_parts)
    gw = ATTN_WIDTH // N_KV_HEADS
    in_specs = [pl.BlockSpec((1, tq, gw), lambda bi, h, i: (bi, i, h))]
    args = [q]
    for kv in kv_parts:
        in_specs += [pl.BlockSpec((1, kv.shape[1], HEAD_DIM), lambda bi, h, i: (bi, 0, h)),
                     pl.BlockSpec((1, kv.shape[1], HEAD_DIM), lambda bi, h, i: (bi, 0, N_KV_HEADS + h))]
        args += [kv, kv]
    return pl.pallas_call(
        _gqa_kernel,
        out_shape=jax.ShapeDtypeStruct((b, l, ATTN_WIDTH), BF16),
        grid=(b, N_KV_HEADS, l // tq),
        in_specs=in_specs,
        out_specs=pl.BlockSpec((1, tq, gw), lambda bi, h, i: (bi, i, h)),
        scratch_shapes=[pltpu.VMEM((lk, HEAD_DIM), BF16), pltpu.VMEM((lk, 2 * HEAD_DIM), BF16)],
        compiler_params=_cparams("parallel", "parallel", "arbitrary"),
        name="gqa_attention",
    )(*args)


MLA_HB = 4


def _mla_kernel(qn_ref, qp_ref, *refs):
    seg_refs, (o_ref, kcat_ref, vones_ref) = refs[:-3], refs[-3:]

    @pl.when(pl.program_id(2) == 0)
    def _():
        r0 = 0
        for k_ref, v_ref, kpe_ref in zip(seg_refs[0::3], seg_refs[1::3], seg_refs[2::3]):
            rows = slice(r0, r0 + k_ref.shape[1])
            for i in range(MLA_HB):
                kcat_ref[i, rows, 0:NOPE_DIM] = k_ref[0, :, i * NOPE_DIM:(i + 1) * NOPE_DIM]
                kcat_ref[i, rows, NOPE_DIM:2 * NOPE_DIM] = kpe_ref[0]
                vones_ref[i, rows, 0:V_DIM] = v_ref[0, :, i * V_DIM:(i + 1) * V_DIM]
            r0 += k_ref.shape[1]
        for i in range(MLA_HB):
            vones_ref[i, :, V_DIM:2 * V_DIM] = jnp.ones((vones_ref.shape[1], V_DIM), BF16)

    lane = lax.broadcasted_iota(jnp.int32, (ATT_ROWS, LANES), 1)
    per_block = LANES // ROPE_DIM
    n_units = qn_ref.shape[1] // ATT_ROWS * MLA_HB

    def q_of(n):
        r, i = divmod(n, MLA_HB)
        rows = slice(r * ATT_ROWS, (r + 1) * ATT_ROWS)
        blk, pos = divmod(i, per_block)
        qp = qp_ref[0, rows, blk * LANES:(blk + 1) * LANES]
        mine = (lane >= pos * ROPE_DIM) & (lane < (pos + 1) * ROPE_DIM)
        return jnp.concatenate(
            [qn_ref[0, rows, i * NOPE_DIM:(i + 1) * NOPE_DIM], jnp.where(mine, qp, jnp.zeros_like(qp))], axis=1)

    def store(n, o):
        r, i = divmod(n, MLA_HB)
        o_ref[0, r * ATT_ROWS:(r + 1) * ATT_ROWS, i * V_DIM:(i + 1) * V_DIM] = o

    _attend_units(n_units, q_of, lambda n: kcat_ref[n % MLA_HB], lambda n: vones_ref[n % MLA_HB], store)


def _mla_attention(qn, qp, segments, *, tq):
    b, l, _ = qn.shape
    lk = sum(kv.shape[1] for kv, _ in segments)
    wn = MLA_HB * NOPE_DIM
    nv0 = MLA_HEADS * NOPE_DIM // wn
    in_specs = [pl.BlockSpec((1, tq, wn), lambda bi, h, i: (bi, i, h)),
                pl.BlockSpec((1, tq, MLA_HB * ROPE_DIM), lambda bi, h, i: (bi, i, h))]
    args = [qn, qp]
    for kv, kpe in segments:
        lj = kv.shape[1]
        in_specs += [pl.BlockSpec((1, lj, wn), lambda bi, h, i: (bi, 0, h)),
                     pl.BlockSpec((1, lj, wn), lambda bi, h, i: (bi, 0, nv0 + h)),
                     pl.BlockSpec((1, lj, LANES), lambda bi, h, i: (bi, 0, 0))]
        args += [kv, kv, kpe]
    return pl.pallas_call(
        _mla_kernel,
        out_shape=jax.ShapeDtypeStruct((b, l, MLA_HEADS * V_DIM), BF16),
        grid=(b, MLA_HEADS // MLA_HB, l // tq),
        in_specs=in_specs,
        out_specs=pl.BlockSpec((1, tq, wn), lambda bi, h, i: (bi, i, h)),
        scratch_shapes=[pltpu.VMEM((MLA_HB, lk, 2 * NOPE_DIM), BF16), pltpu.VMEM((MLA_HB, lk, 2 * V_DIM), BF16)],
        compiler_params=_cparams("parallel", "parallel", "arbitrary"),
        name="mla_attention",
    )(*args)


CONV_ROWS = 64
CONV_COLS = 128


def _conv_kernel(ug_ref, w_ref, b_ref, lg_ref, lb_ref, o_ref, pad_ref, y_ref):
    l, c = ug_ref.shape[1], ug_ref.shape[2]
    pad_ref[0:CONV_PAD, :] = jnp.zeros((CONV_PAD, c), F32)
    pad_ref[CONV_PAD + l:2 * CONV_PAD + l, :] = jnp.zeros((CONV_PAD, c), F32)
    pad_ref[CONV_PAD:CONV_PAD + l, :] = ug_ref[0].astype(F32)
    off = CONV_PAD - CONV_K // 2
    nwin = CONV_ROWS + 2 * CONV_PAD

    def body(r, carry):
        r0 = pl.multiple_of(r * CONV_ROWS, CONV_ROWS)
        for cb in range(c // CONV_COLS):
            cs = slice(cb * CONV_COLS, (cb + 1) * CONV_COLS)
            win = pad_ref[pl.ds(r0, nwin), cs]
            acc = jnp.zeros((CONV_ROWS, CONV_COLS), F32) + b_ref[:, cs]
            for s in range(SUBLANES):
                ws = win if s == 0 else pltpu.roll(win, nwin - s, 0)
                for k in range(CONV_K):
                    if (off + k) % SUBLANES == s:
                        j = (off + k) // SUBLANES * SUBLANES
                        acc = acc + w_ref[k:k + 1, cs] * ws[j:j + CONV_ROWS, :]
            y_ref[:, cs] = acc
        y = y_ref[...]
        mu = jnp.mean(y, axis=-1, keepdims=True)
        yc = y - mu
        var = jnp.mean(yc * yc, axis=-1, keepdims=True)
        z = yc * lax.rsqrt(var + EPS) * lg_ref[...] + lb_ref[...]
        o_ref[0, pl.ds(r0, CONV_ROWS), :] = (z * _sigmoid(z)).astype(BF16)
        return carry

    lax.fori_loop(0, l // CONV_ROWS, body, 0)


def _conv_branch(ug, w, b, lg, lb):
    bsz, l, c = ug.shape
    return pl.pallas_call(
        _conv_kernel,
        out_shape=jax.ShapeDtypeStruct((bsz, l, c), BF16),
        grid=(bsz,),
        in_specs=[pl.BlockSpec((1, l, c), lambda bi: (bi, 0, 0)),
                  _full_spec(w), _full_spec(b), _full_spec(lg), _full_spec(lb)],
        out_specs=pl.BlockSpec((1, l, c), lambda bi: (bi, 0, 0)),
        scratch_shapes=[pltpu.VMEM((l + 2 * CONV_PAD, c), F32), pltpu.VMEM((CONV_ROWS, c), F32)],
        compiler_params=_cparams("parallel"),
        name="conv_branch",
    )(ug, w, b, lg, lb)


OUTPROJ_ROWS = 256


def _outproj_kernel(*refs, n_in):
    x_ref = refs[0]
    a_refs = refs[1:1 + n_in]
    w_refs = refs[1 + n_in:1 + 2 * n_in]
    ga_ref, g2_ref, sh2_ref, sc2_ref, r_ref, xo_ref, h2_ref, lg_ref = refs[1 + 2 * n_in:]
    for r0 in range(0, x_ref.shape[1], OUTPROJ_ROWS):
        rows = slice(r0, r0 + OUTPROJ_ROWS)
        acc = _dot(a_refs[0][0, rows, :], w_refs[0][...])
        for a_ref, w_ref in zip(a_refs[1:], w_refs[1:]):
            acc = acc + _dot(a_ref[0, rows, :], w_ref[...])
        xn = x_ref[0, rows, :] + ga_ref[0] * acc
        xo_ref[0, rows, :] = xn
        h2 = _prenorm(xn, g2_ref[...], sh2_ref[0], sc2_ref[0]).astype(BF16)
        h2_ref[0, rows, :] = h2
        lg_ref[0, rows, :] = _dot(h2, r_ref[...])


def _outproj(x, acts, w, ga, g2, sh2, sc2, router, *, tm):
    b, l, d = x.shape
    n_in = len(acts)
    row = lambda bi, i: (bi, i, 0)
    in_specs = [pl.BlockSpec((1, tm, d), row)]
    in_specs += [pl.BlockSpec((1, tm, a.shape[2]), row) for a in acts]
    in_specs += [pl.BlockSpec((acts[r].shape[2], w.shape[1]), lambda bi, i, r=r: (r, 0)) for r in range(n_in)]
    in_specs += [_mod_spec(ga), _full_spec(g2), _mod_spec(sh2), _mod_spec(sc2), _full_spec(router)]
    return pl.pallas_call(
        functools.partial(_outproj_kernel, n_in=n_in),
        out_shape=(jax.ShapeDtypeStruct((b, l, d), F32),
                   jax.ShapeDtypeStruct((b, l, d), BF16),
                   jax.ShapeDtypeStruct((b, l, LANES), F32)),
        grid=(b, l // tm),
        in_specs=in_specs,
        out_specs=(pl.BlockSpec((1, tm, d), row), pl.BlockSpec((1, tm, d), row),
                   pl.BlockSpec((1, tm, LANES), row)),
        compiler_params=_cparams("parallel", "parallel"),
        name="outproj",
    )(x, *acts, *([w] * n_in), ga, g2, sh2, sc2, router)


MOE_TM = 256
MOE_CHUNKS = 8


def _moe_kernel(*refs, tiles_a, n_chunks):
    if tiles_a is None:
        xs_ref, xs2_ref = refs[0], None
        refs = refs[1:]
    else:
        xs_ref, xs2_ref = refs[:2]
        refs = refs[2:]
    wg_ref, wu_ref, wd_ref, o_ref, wgb_ref, wub_ref, wdb_ref = refs
    p = pl.program_id(0)
    i = pl.program_id(1)
    n_experts = pl.num_programs(0) - 1
    cg = wg_ref.shape[1]
    cd = wd_ref.shape[1]

    @pl.when((p < n_experts) & (i < n_chunks))
    def _():
        slot = p % 2
        rg = pl.multiple_of(i * cg, cg)
        rd = pl.multiple_of(i * cd, cd)
        wgb_ref[slot, pl.ds(rg, cg), :] = wg_ref[0].astype(BF16)
        wub_ref[slot, pl.ds(rg, cg), :] = wu_ref[0].astype(BF16)
        wdb_ref[slot, pl.ds(rd, cd), :] = wd_ref[0].astype(BF16)

    @pl.when(p == 0)
    def _():
        o_ref[...] = jnp.zeros_like(o_ref)

    @pl.when(p > 0)
    def _():
        slot = (p + 1) % 2
        x = xs_ref[0]
        if tiles_a is not None:
            x = jnp.where(i < tiles_a, x, xs2_ref[0])
        g = _dot(x, wgb_ref[slot])
        u = _dot(x, wub_ref[slot])
        hid = (g * _sigmoid(g) * u).astype(BF16)
        o_ref[0] = _dot(hid, wdb_ref[slot]).astype(BF16)


def _moe_ffn(xs, wg, wu, wd, xs2=None):
    e, m, d = xs.shape
    f = wg.shape[2]
    tiles_a = m // MOE_TM
    n_chunks = min(MOE_CHUNKS, tiles_a)
    assert m % MOE_TM == 0 and d % (n_chunks * 16) == 0 and f % (n_chunks * 16) == 0
    last_e, last_c, last_a = e - 1, n_chunks - 1, tiles_a - 1
    rows = lambda p, i: (jnp.maximum(p - 1, 0), jnp.minimum(i, last_a), 0)
    chunk = lambda p, i: (jnp.minimum(p, last_e), jnp.minimum(i, last_c), 0)
    args = [xs]
    in_specs = [pl.BlockSpec((1, MOE_TM, d), rows)]
    if xs2 is not None:
        assert xs2.shape == (e, MOE_TM, d)
        args.append(xs2)
        in_specs.append(pl.BlockSpec((1, MOE_TM, d), lambda p, i: (jnp.maximum(p - 1, 0), 0, 0)))
    tiles = tiles_a + (xs2 is not None)
    in_specs += [pl.BlockSpec((1, d // n_chunks, f), chunk),
                 pl.BlockSpec((1, d // n_chunks, f), chunk),
                 pl.BlockSpec((1, f // n_chunks, d), chunk)]
    return pl.pallas_call(
        functools.partial(_moe_kernel, tiles_a=tiles_a if xs2 is not None else None, n_chunks=n_chunks),
        out_shape=jax.ShapeDtypeStruct((e + 1, tiles * MOE_TM, d), BF16),
        grid=(e + 1, tiles),
        in_specs=in_specs,
        out_specs=pl.BlockSpec((1, MOE_TM, d), lambda p, i: (p, i, 0)),
        scratch_shapes=[pltpu.VMEM((2, d, f), BF16), pltpu.VMEM((2, d, f), BF16), pltpu.VMEM((2, f, d), BF16)],
        compiler_params=_cparams("arbitrary", "arbitrary"),
        name="moe_ffn",
    )(*args, wg, wu, wd)


COMBINE_TT = 256
COMBINE_W = 256
COMBINE_KW = 1024


def _combine_kernel(p0_ref, tok_ref, gate_ref, z_ref, x_ref, ga_ref, g_ref, o_ref, acc_ref, *, final_norm):
    bi = pl.program_id(0)
    ti = pl.program_id(1)
    p0 = p0_ref[bi, ti]
    p1 = p0_ref[bi, ti + 1]
    n_pairs = z_ref.shape[1]
    kw = min(COMBINE_KW, n_pairs)
    tok0 = ti * COMBINE_TT

    def window(start, width):
        tok_row = tok0 + lax.broadcasted_iota(jnp.int32, (COMBINE_TT, width), 0)
        toks = tok_ref[0, :, pl.ds(start, width)]
        gates = gate_ref[0, :, pl.ds(start, width)]
        weights = jnp.where(toks == tok_row, gates, 0.0).astype(BF16)
        return _dot(weights, z_ref[0, pl.ds(start, width), :])

    first = jnp.minimum(p0 // COMBINE_W * COMBINE_W, n_pairs - kw)
    fits = p1 <= first + kw

    @pl.when(fits)
    def _():
        acc_ref[...] = window(pl.multiple_of(first, COMBINE_W), kw)

    @pl.when(jnp.logical_not(fits))
    def _():
        acc_ref[...] = jnp.zeros_like(acc_ref)

        def body(w, carry):
            acc_ref[...] += window(pl.multiple_of(w * COMBINE_W, COMBINE_W), COMBINE_W)
            return carry

        lax.fori_loop(p0 // COMBINE_W, (p1 + COMBINE_W - 1) // COMBINE_W, body, 0)

    x = x_ref[0] + ga_ref[0] * acc_ref[...]
    if final_norm:
        ms = jnp.mean(x * x, axis=-1, keepdims=True)
        x = x * lax.rsqrt(ms + EPS) * g_ref[...]
    o_ref[0] = x


def _combine(x, ga, g, tok_sorted, gate_sorted, z, p0, *, final_norm, b0=0):
    b, n, d = x.shape
    nb, p, _ = z.shape
    row = lambda bi, ti, *_: (b0 + bi, ti, 0)
    args = [p0, tok_sorted, gate_sorted, z, x, ga, g]
    in_specs = [pl.BlockSpec((1, 1, p), lambda bi, ti, *_: (bi, 0, 0)),
                pl.BlockSpec((1, 1, p), lambda bi, ti, *_: (bi, 0, 0)),
                pl.BlockSpec((1, p, d), lambda bi, ti, *_: (bi, 0, 0)),
                pl.BlockSpec((1, COMBINE_TT, d), row), _mod_spec(ga, b0), _full_spec(g)]
    aliases = {4: 0}
    grid_spec = pltpu.PrefetchScalarGridSpec(
        num_scalar_prefetch=1,
        grid=(nb, n // COMBINE_TT),
        in_specs=in_specs,
        out_specs=pl.BlockSpec((1, COMBINE_TT, d), row),
        scratch_shapes=[pltpu.VMEM((COMBINE_TT, d), F32)])
    return pl.pallas_call(
        functools.partial(_combine_kernel, final_norm=final_norm),
        out_shape=jax.ShapeDtypeStruct((b, n, d), F32),
        grid_spec=grid_spec,
        input_output_aliases=aliases,
        compiler_params=_cparams("parallel", "arbitrary"),
        name="moe_combine",
    )(*args)


ONEHOT_GATHER_MAX_TOKENS = 256


def _onehot_gather_kernel(idx_ref, h_ref, o_ref):
    s, n = idx_ref.shape[1], h_ref.shape[1]
    tok = lax.broadcasted_iota(jnp.int32, (s, n), 1)
    onehot = jnp.where(idx_ref[0] == tok, 1.0, 0.0).astype(BF16)
    o_ref[...] = _dot(onehot, h_ref[0]).reshape(o_ref.shape).astype(o_ref.dtype)


def _onehot_gather(h2, idx):
    b, n, d = h2.shape
    _, e, cap = idx.shape
    return pl.pallas_call(
        _onehot_gather_kernel,
        out_shape=jax.ShapeDtypeStruct((e, b * cap, d), h2.dtype),
        grid=(b,),
        in_specs=[pl.BlockSpec((1, e * cap, 1), lambda bi: (bi, 0, 0)),
                  pl.BlockSpec((1, n, d), lambda bi: (bi, 0, 0))],
        out_specs=pl.BlockSpec((e, cap, d), lambda bi: (0, bi, 0)),
        compiler_params=_cparams("parallel"),
        name="onehot_gather",
    )(idx.reshape(b, e * cap, 1), h2)


def _select(logits):
    n = logits.shape[1]
    cap = EC_FACTOR * n // N_EXPERTS
    aff = jax.nn.softmax(logits[..., :N_EXPERTS], axis=-1)
    return lax.top_k(jnp.swapaxes(aff, 1, 2), cap)


def _routed_rows(h2, idx, b0, nb):
    d = h2.shape[2]
    cap = idx.shape[2]
    idx = idx[b0:b0 + nb]
    if h2.shape[1] <= ONEHOT_GATHER_MAX_TOKENS and (b0, nb) == (0, h2.shape[0]):
        return _onehot_gather(h2, idx)
    bidx = b0 + jnp.arange(nb, dtype=idx.dtype)[None, :, None]
    return h2[bidx, jnp.swapaxes(idx, 0, 1)].reshape(N_EXPERTS, nb * cap, d)


def _combine_routed(x, ga, g, gate, idx, ys, row0, *, final_norm, b0=0):
    _, n, d = x.shape
    b = idx.shape[0]
    cap = idx.shape[2]
    pairs = N_EXPERTS * cap
    m = ys.shape[1]
    slots = lax.broadcasted_iota(jnp.int32, (b, pairs), 1)
    tok_sorted, order, gate_sorted = lax.sort((idx.reshape(b, pairs), slots, gate.reshape(b, pairs)),
                                              dimension=1, num_keys=1)
    flat = (order // cap + 1) * m + row0 + jnp.arange(b, dtype=jnp.int32)[:, None] * cap + order % cap
    z = ys.reshape((N_EXPERTS + 1) * m, d)[flat]
    bounds = jnp.arange(n // COMBINE_TT + 1, dtype=jnp.int32) * COMBINE_TT
    p0 = jnp.sum(tok_sorted[:, None, :] < bounds[None, :, None], axis=-1).astype(jnp.int32)
    return _combine(x, ga, g, tok_sorted.reshape(b, 1, pairs), gate_sorted.reshape(b, 1, pairs), z, p0,
                    final_norm=final_norm, b0=b0)


def _ec_moe(x, ga, g, h2, logits, wg, wu, wd, *, final_norm, ffn_halves, extra=None):
    bsz = x.shape[0]
    nb = bsz // 2
    gate, idx = _select(logits)
    cap = idx.shape[2]
    xs2 = None
    if extra is not None:
        ex, ega, eh2, elogits = extra
        egate, eidx = _select(elogits)
        xs2 = _routed_rows(eh2, eidx, 0, eh2.shape[0])
    if ffn_halves == 2:
        ys = [_moe_ffn(_routed_rows(h2, idx, 0, nb), wg, wu, wd, xs2),
              _moe_ffn(_routed_rows(h2, idx, nb, nb), wg, wu, wd)]
        row0 = [0, 0]
        extra_row0 = nb * cap
    else:
        ys = [_moe_ffn(_routed_rows(h2, idx, 0, bsz), wg, wu, wd, xs2)] * 2
        row0 = [0, nb * cap]
        extra_row0 = bsz * cap
    out = _combine_routed(x, ga, g, gate[:nb], idx[:nb], ys[0], row0[0], final_norm=final_norm)
    out = _combine_routed(out, ga, g, gate[nb:], idx[nb:], ys[1], row0[1], final_norm=final_norm, b0=nb)
    if extra is None:
        return out
    return out, _combine_routed(ex, ega, g, egate, eidx, ys[0], extra_row0, final_norm=final_norm)


def _rope_quarter(y, cos, sin, first):
    rot = jnp.where(first, pltpu.roll(y, LANES - ROPE_DIM // 2, 1), pltpu.roll(y, ROPE_DIM // 2, 1))
    return y * cos + rot * sin


def _l1_dqkv_kernel(x_ref, g_ref, sh_ref, sc_ref, w_ref, qg_ref, kvg_ref, cos_ref, sin_ref, *out_refs,
                    with_q, rope):
    if with_q:
        cq_ref, ckv_ref, kpe_ref = out_refs
    else:
        ckv_ref, kpe_ref = out_refs
    h = _prenorm(x_ref[0], g_ref[...], sh_ref[0], sc_ref[0]).astype(BF16)
    if with_q:
        cq = _dot(h, w_ref[:, 0:Q_LORA])
        ms = jnp.mean(cq * cq, axis=-1, keepdims=True)
        cq_ref[0] = (cq * lax.rsqrt(ms + EPS) * qg_ref[...]).astype(BF16)
    z = _dot(h, w_ref[:, Q_LORA:Q_LORA + KV_LORA + LANES])
    ckv = z[:, 0:KV_LORA]
    ms = jnp.mean(ckv * ckv, axis=-1, keepdims=True)
    ckv_ref[0] = (ckv * lax.rsqrt(ms + EPS) * kvg_ref[...]).astype(BF16)
    pe = z[:, KV_LORA:KV_LORA + LANES]
    if rope:
        lane = lax.broadcasted_iota(jnp.int32, pe.shape, 1)
        pe = _rope_quarter(pe, cos_ref[...], sin_ref[...], (lane % ROPE_DIM) < ROPE_DIM // 2)
    kpe_ref[0] = (pe + pltpu.roll(pe, ROPE_DIM, 1)).astype(BF16)


def _l1_dqkv(x, g, sh, sc, w, qg, kvg, cos, sin, *, rope, with_q, tm):
    b, l, d = x.shape
    row = lambda bi, i: (bi, i, 0)
    args = [x, g, sh, sc, w, qg, kvg, cos, sin]
    in_specs = [pl.BlockSpec((1, tm, d), row), _full_spec(g), _mod_spec(sh), _mod_spec(sc), _full_spec(w),
                _full_spec(qg), _full_spec(kvg),
                pl.BlockSpec((tm, LANES), lambda bi, i: (i, 0)), pl.BlockSpec((tm, LANES), lambda bi, i: (i, 0))]
    out_shape, out_specs = [], []
    if with_q:
        out_shape.append(jax.ShapeDtypeStruct((b, l, Q_LORA), BF16))
        out_specs.append(pl.BlockSpec((1, tm, Q_LORA), row))
    out_shape += [jax.ShapeDtypeStruct((b, l, KV_LORA), BF16), jax.ShapeDtypeStruct((b, l, LANES), BF16)]
    out_specs += [pl.BlockSpec((1, tm, KV_LORA), row), pl.BlockSpec((1, tm, LANES), row)]

    return pl.pallas_call(
        functools.partial(_l1_dqkv_kernel, with_q=with_q, rope=rope),
        out_shape=tuple(out_shape),
        grid=(b, l // tm),
        in_specs=in_specs,
        out_specs=tuple(out_specs),
        compiler_params=_cparams("parallel", "parallel"),
        name="l1_dqkv",
    )(*args)


def _l1_q_kernel(cq_ref, w_ref, cos_ref, sin_ref, qn_ref, qp_ref):
    a = cq_ref[0]
    cw = 4 * LANES
    n_nope = MLA_HEADS * NOPE_DIM
    qscale = MLA_SCALE * LOG2E
    for c in range(n_nope // cw):
        qn_ref[0, :, c * cw:(c + 1) * cw] = (_dot(a, w_ref[:, c * cw:(c + 1) * cw]) * qscale).astype(BF16)
    cos = cos_ref[...]
    sin = sin_ref[...]
    lane = lax.broadcasted_iota(jnp.int32, cos.shape, 1)
    first = (lane % ROPE_DIM) < ROPE_DIM // 2
    for c in range(MLA_HEADS * ROPE_DIM // cw):
        y = _dot(a, w_ref[:, n_nope + c * cw:n_nope + (c + 1) * cw])
        parts = [_rope_quarter(y[:, j * LANES:(j + 1) * LANES], cos, sin, first) * qscale for j in range(4)]
        qp_ref[0, :, c * cw:(c + 1) * cw] = jnp.concatenate(parts, axis=1).astype(BF16)


def _l1_q(cq, w, cos, sin, *, tm):
    b, l, k = cq.shape
    row = lambda bi, i: (bi, i, 0)
    return pl.pallas_call(
        _l1_q_kernel,
        out_shape=(jax.ShapeDtypeStruct((b, l, MLA_HEADS * NOPE_DIM), BF16),
                   jax.ShapeDtypeStruct((b, l, MLA_HEADS * ROPE_DIM), BF16)),
        grid=(b, l // tm),
        in_specs=[pl.BlockSpec((1, tm, k), row), _full_spec(w),
                  pl.BlockSpec((tm, LANES), lambda bi, i: (i, 0)), pl.BlockSpec((tm, LANES), lambda bi, i: (i, 0))],
        out_specs=(pl.BlockSpec((1, tm, MLA_HEADS * NOPE_DIM), row),
                   pl.BlockSpec((1, tm, MLA_HEADS * ROPE_DIM), row)),
        compiler_params=_cparams("parallel", "parallel"),
        name="l1_q",
    )(cq, w, cos, sin)


def _mm_kernel(a_ref, w_ref, o_ref):
    a = a_ref[0]
    cw = 4 * LANES
    for c in range(w_ref.shape[1] // cw):
        o_ref[0, :, c * cw:(c + 1) * cw] = _dot(a, w_ref[:, c * cw:(c + 1) * cw]).astype(o_ref.dtype)


def _mm(a, w, *, tm):
    b, l, k = a.shape
    n = w.shape[1]
    row = lambda bi, i: (bi, i, 0)
    return pl.pallas_call(
        _mm_kernel,
        out_shape=jax.ShapeDtypeStruct((b, l, n), BF16),
        grid=(b, l // tm),
        in_specs=[pl.BlockSpec((1, tm, k), row), _full_spec(w)],
        out_specs=pl.BlockSpec((1, tm, n), row),
        compiler_params=_cparams("parallel", "parallel"),
        name="l1_kv_up",
    )(a, w)


def _rope_tables(n_tokens, d_rot):
    rows = n_tokens // GRID_W
    row = jnp.repeat(jnp.arange(rows, dtype=F32), GRID_W)
    col = jnp.tile(jnp.arange(GRID_W, dtype=F32), rows)
    n_axis = d_rot // 4
    inv_freq = ROPE_THETA ** (-jnp.arange(n_axis, dtype=F32) / n_axis)
    ang = jnp.concatenate([row[:, None] * inv_freq, col[:, None] * inv_freq], axis=-1)
    cos, sin = jnp.cos(ang), jnp.sin(ang)
    cos_t = jnp.concatenate([cos, cos], axis=-1)
    sin_t = jnp.concatenate([-sin, sin], axis=-1)
    reps = LANES // d_rot
    return jnp.tile(cos_t, (1, reps)), jnp.tile(sin_t, (1, reps))


def _mods(mod, lo, hi):
    return [mod[lo:hi, None, i * D_MODEL:(i + 1) * D_MODEL] for i in range(6)]


def _pad_router(router):
    return jnp.pad(router, ((0, 0), (0, LANES - router.shape[1]))).astype(BF16)


def kernel(x, c, ctx, c_ctx, l0_mod_w, l0_mod_b, l0_norm1_g, l0_w_in, l0_q_norm_g, l0_k_norm_g, l0_dw_w, l0_dw_b, l0_conv_ln_g, l0_conv_ln_b, l0_w_out, l0_norm2_g, l0_router, l0_w_gate, l0_w_up, l0_w_down, l1_mod_w, l1_mod_b, l1_norm1_g, l1_w_dqkv, l1_q_lora_norm_g, l1_w_uq, l1_kv_lora_norm_g, l1_w_ukv, l1_w_out, l1_norm2_g, l1_router, l1_w_gate, l1_w_up, l1_w_down, final_norm_g):
    bsz, seq, d = x.shape
    n_ctx = ctx.shape[1]
    row2 = lambda v: v.reshape(1, -1)

    cond = jnp.zeros((16, d), F32).at[:bsz].set(c).at[bsz].set(c_ctx)
    mod0 = _ada(cond, l0_mod_w, l0_mod_b)
    mod1 = _ada(cond, l1_mod_w, l1_mod_b)

    sh1, sc1, ga1, sh2, sc2, ga2 = _mods(mod0, 0, bsz)
    csh1, csc1, cga1, csh2, csc2, cga2 = _mods(mod0, bsz, bsz + 1)
    s_q, s_k, s_v, s_u = ATTN_WIDTH, ATTN_WIDTH + KV_WIDTH, ATTN_WIDTH + 2 * KV_WIDTH, ATTN_WIDTH + 2 * KV_WIDTH + CONV_WIDTH
    w_in = jnp.concatenate([l0_w_in[:, :s_q], l0_w_in[:, s_v:s_u], l0_w_in[:, s_u:],
                            l0_w_in[:, s_q:s_k], l0_w_in[:, s_k:s_v]], axis=1).astype(BF16)
    cos0, sin0 = _rope_tables(seq, HEAD_DIM)
    g1 = row2(l0_norm1_g)
    qg, kg = row2(l0_q_norm_g), row2(l0_k_norm_g)
    q_l, ug_l, kv_l = _l0_inproj(x, g1, sh1, sc1, w_in, qg, kg, cos0, sin0, rope=True, tm=512)
    q_c, ug_c, kv_c = _l0_inproj(ctx, g1, csh1, csc1, w_in, qg, kg, cos0, sin0, rope=False, tm=n_ctx)
    a_l = _gqa_attention(q_l, [kv_c, kv_l], tq=1024)
    dw_w = l0_dw_w.reshape(CONV_K, CONV_WIDTH)
    dw_b, ln_g, ln_b = row2(l0_dw_b), row2(l0_conv_ln_g), row2(l0_conv_ln_b)
    cb_l = _conv_branch(ug_l, dw_w, dw_b, ln_g, ln_b)
    w_out0 = l0_w_out.astype(BF16)
    g2 = row2(l0_norm2_g)
    router0 = _pad_router(l0_router)
    x1, h2_l, lg_l = _outproj(x, [a_l, cb_l], w_out0, ga1, g2, sh2, sc2, router0, tm=512)
    a_c = _gqa_attention(q_c, [kv_c], tq=n_ctx)
    cb_c = _conv_branch(ug_c, dw_w, dw_b, ln_g, ln_b)
    xc1, h2_c, lg_c = _outproj(ctx, [a_c, cb_c], w_out0, cga1, g2, csh2, csc2, router0, tm=n_ctx)
    x2, xc2 = _ec_moe(x1, ga2, g2, h2_l, lg_l, l0_w_gate, l0_w_up, l0_w_down, final_norm=False, ffn_halves=1,
                      extra=(xc1, cga2, h2_c, lg_c))

    sh1, sc1, ga1, sh2, sc2, ga2_1 = _mods(mod1, 0, bsz)
    csh1, csc1 = _mods(mod1, bsz, bsz + 1)[:2]
    w_dqkv = jnp.pad(l1_w_dqkv, ((0, 0), (0, LANES - ROPE_DIM))).astype(BF16)
    cos1, sin1 = _rope_tables(seq, ROPE_DIM)
    g1 = row2(l1_norm1_g)
    qlg, kvlg = row2(l1_q_lora_norm_g), row2(l1_kv_lora_norm_g)
    cq, ckv_l, kpe_l = _l1_dqkv(x2, g1, sh1, sc1, w_dqkv, qlg, kvlg, cos1, sin1, rope=True, with_q=True, tm=512)
    ckv_c, kpe_c = _l1_dqkv(xc2, g1, csh1, csc1, w_dqkv, qlg, kvlg, cos1, sin1, rope=False, with_q=False, tm=n_ctx)
    w_uq = l1_w_uq.reshape(Q_LORA, MLA_HEADS, QK_DIM)
    w_uq = jnp.concatenate([w_uq[:, :, :NOPE_DIM].reshape(Q_LORA, -1),
                            w_uq[:, :, NOPE_DIM:].reshape(Q_LORA, -1)], axis=1).astype(BF16)
    w_ukv = l1_w_ukv.reshape(KV_LORA, MLA_HEADS, NOPE_DIM + V_DIM)
    w_ukv = jnp.concatenate([w_ukv[:, :, :NOPE_DIM].reshape(KV_LORA, -1),
                             w_ukv[:, :, NOPE_DIM:].reshape(KV_LORA, -1)], axis=1).astype(BF16)
    qn, qp = _l1_q(cq, w_uq, cos1, sin1, tm=512)
    kv_c1 = _mm(ckv_c, w_ukv, tm=n_ctx)
    kv_l1 = _mm(ckv_l, w_ukv, tm=1024)
    a1 = _mla_attention(qn, qp, [(kv_c1, kpe_c), (kv_l1, kpe_l)], tq=1024)
    x3, h2, lg = _outproj(x2, [a1], l1_w_out.astype(BF16), ga1, row2(l1_norm2_g), sh2, sc2,
                          _pad_router(l1_router), tm=512)
    return _ec_moe(x3, ga2_1, row2(final_norm_g), h2, lg, l1_w_gate, l1_w_up, l1_w_down, final_norm=True,
                   ffn_halves=2)
```
